```python
import math
import jax, jax.numpy as jnp
from jax import lax
import numpy as np

D_MODEL = 1024
BATCH = 2
SEQ = 8192
DEPTH = 4

CTX_LEN = 256
GRID_W = 64

GMLP_HEADS = 4
GMLP_HEAD_DIM = 64
GMLP_CHUNK = 128
W_A = GMLP_HEADS * GMLP_HEAD_DIM

DIFF_HEADS = 4
DIFF_HEAD_DIM = 64
DIFF_V_DIM = 2 * DIFF_HEAD_DIM
W_B = DIFF_HEADS * DIFF_V_DIM

CONV_GROUPS = 4
CONV_GROUP_DIM = 64
CONV_WIDTH = 3
W_C = CONV_GROUPS * CONV_GROUP_DIM

D_MIX = W_A + W_B + W_C
D_IN = 3 * W_A + 4 * W_B + 4 * W_C
KV_START = 3 * W_A + W_B
KV_END = KV_START + 2 * W_B

Q_BLOCK = 128
ROPE_BASE = 10000.0
ROPE_PAIRS = DIFF_HEAD_DIM // 4
EPS = 1e-6

kernel_name = "hybrid_parallel_heads_dit_trunk"


def rmsnorm(x, g):
    xf = x.astype(jnp.float32)
    y = xf * lax.rsqrt(jnp.mean(xf * xf, axis=-1, keepdims=True) + EPS)
    return (y * g.astype(jnp.float32)).astype(x.dtype)


def adaln(cond, w_mod, b_mod):
    m = jax.nn.silu(cond) @ w_mod + b_mod
    return jnp.split(m, 3, axis=-1)


def split_cols(p):
    widths = (W_A, W_A, W_A, W_B, W_B, W_B, W_B, W_C, W_C, W_C, W_C)
    idx = [int(i) for i in np.cumsum(widths)[:-1]]
    return jnp.split(p, idx, axis=-1)


def axial_rope_tables(rows):
    row = jnp.broadcast_to(jnp.arange(rows)[:, None], (rows, GRID_W)).reshape(-1).astype(jnp.float32)
    col = jnp.broadcast_to(jnp.arange(GRID_W)[None, :], (rows, GRID_W)).reshape(-1).astype(jnp.float32)
    inv = ROPE_BASE ** (-jnp.arange(ROPE_PAIRS, dtype=jnp.float32) / ROPE_PAIRS)
    ang = jnp.stack([row[:, None] * inv, col[:, None] * inv], axis=1)
    return jnp.cos(ang), jnp.sin(ang)


def apply_rope(x, cos, sin):
    xs = x.reshape(x.shape[:-1] + (2, 2, ROPE_PAIRS))
    x1, x2 = xs[..., 0, :], xs[..., 1, :]
    c = cos[None, :, None, None]
    s = sin[None, :, None, None]
    out = jnp.stack([x1 * c - x2 * s, x2 * c + x1 * s], axis=-2)
    return out.reshape(x.shape).astype(x.dtype)


def short_conv3(x, w, b):
    xp = jnp.pad(x, ((0, 0), (1, 1), (0, 0)))
    return xp[:, :-2] * w[0] + xp[:, 1:-1] * w[1] + xp[:, 2:] * w[2] + b


def chunk_sgu(u, v, g, w_s, b_s):
    bsz, length, _ = v.shape
    v = rmsnorm(v, g)
    vc = v.reshape(bsz, length // GMLP_CHUNK, GMLP_CHUNK, GMLP_HEADS, GMLP_HEAD_DIM)
    mixed = jnp.einsum('hpq,bnqhd->bnphd', w_s, vc) + b_s.T[None, None, :, :, None]
    return u * mixed.reshape(bsz, length, W_A)


def diff_attention(q, k_all, v_all, lam):
    bsz, length = q.shape[:2]
    nb = length // Q_BLOCK
    scale = 1.0 / math.sqrt(DIFF_HEAD_DIM)
    qb = q.reshape(bsz, nb, Q_BLOCK, DIFF_HEADS, 2, DIFF_HEAD_DIM).transpose(1, 0, 2, 3, 4, 5)

    def block(qblk):
        s = jnp.einsum('bqhmd,bkhmd->bhmqk', qblk, k_all).astype(jnp.float32) * scale
        p = jax.nn.softmax(s, axis=-1)
        a = p[:, :, 0] - lam * p[:, :, 1]
        return jnp.einsum('bhqk,bkhe->bqhe', a.astype(v_all.dtype), v_all)

    o = lax.map(block, qb)
    return o.transpose(1, 0, 2, 3, 4).reshape(bsz, length, DIFF_HEADS, DIFF_V_DIM)


def mixer(parts, k_all, v_all, rope, lam, lam_init, w_out, sgu_g, sgu_w, sgu_b, subln_g, conv_w, conv_b):
    au, av, az, bq, _, _, bz, cb, cc, cx, cz = parts
    bsz, length, _ = au.shape
    ya = chunk_sgu(jax.nn.gelu(au), jax.nn.gelu(av), sgu_g, sgu_w, sgu_b) * jax.nn.silu(az)
    q = bq.reshape(bsz, length, DIFF_HEADS, 2, DIFF_HEAD_DIM)
    if rope is not None:
        q = apply_rope(q, rope[0], rope[1])
    ob = rmsnorm(diff_attention(q, k_all, v_all, lam), subln_g) * (1.0 - lam_init)
    yb = ob.reshape(bsz, length, W_B) * jax.nn.silu(bz)
    yc = cb * short_conv3(cc * cx, conv_w, conv_b) * jax.nn.silu(cz)
    return jnp.concatenate([ya, yb, yc], axis=-1) @ w_out


def setup_inputs(seed: int = 0) -> dict:
    key = jax.random.key(seed)
    ks = jax.random.split(key, 24)
    f32 = jnp.float32
    nrm = lambda k, shape, s: jax.random.normal(k, shape, f32) * s
    return {
        "x": nrm(ks[0], (BATCH, SEQ, D_MODEL), 1.0),
        "c": nrm(ks[1], (BATCH, D_MODEL), 1.0),
        "ctx": nrm(ks[2], (BATCH, CTX_LEN, D_MODEL), 1.0),
        "c_ctx": nrm(ks[3], (D_MODEL,), 1.0),
        "w_mod": nrm(ks[4], (DEPTH, D_MODEL, 3 * D_MODEL), 0.5 * D_MODEL ** -0.5),
        "b_mod": nrm(ks[5], (DEPTH, 3 * D_MODEL), 0.02),
        "norm_g": 1.0 + nrm(ks[6], (DEPTH, D_MODEL), 0.02),
        "w_in": nrm(ks[7], (DEPTH, D_MODEL, D_IN), D_MODEL ** -0.5),
        "w_out": nrm(ks[8], (DEPTH, D_MIX, D_MODEL), D_MIX ** -0.5),
        "sgu_norm_g": 1.0 + nrm(ks[9], (DEPTH, W_A), 0.02),
        "sgu_w": nrm(ks[10], (DEPTH, GMLP_HEADS, GMLP_CHUNK, GMLP_CHUNK), GMLP_CHUNK ** -0.5),
        "sgu_b": 1.0 + nrm(ks[11], (DEPTH, GMLP_HEADS, GMLP_CHUNK), 0.02),
        "lambda_q1": nrm(ks[12], (DEPTH, DIFF_HEAD_DIM), 0.1),
        "lambda_k1": nrm(ks[13], (DEPTH, DIFF_HEAD_DIM), 0.1),
        "lambda_q2": nrm(ks[14], (DEPTH, DIFF_HEAD_DIM), 0.1),
        "lambda_k2": nrm(ks[15], (DEPTH, DIFF_HEAD_DIM), 0.1),
        "subln_g": 1.0 + nrm(ks[16], (DEPTH, DIFF_V_DIM), 0.02),
        "conv_w": nrm(ks[17], (DEPTH, CONV_WIDTH, W_C), CONV_WIDTH ** -0.5),
        "conv_b": nrm(ks[18], (DEPTH, W_C), 0.02),
        "final_g": 1.0 + nrm(ks[19], (D_MODEL,), 0.02),
    }


def reference(x, c, ctx, c_ctx, w_mod, b_mod, norm_g, w_in, w_out, sgu_norm_g, sgu_w, sgu_b,
              lambda_q1, lambda_k1, lambda_q2, lambda_k2, subln_g, conv_w, conv_b, final_g):
    bsz, length, _ = x.shape
    ctx_len = ctx.shape[1]
    rows = length // GRID_W
    cos, sin = axial_rope_tables(rows)
    for l in range(DEPTH):
        lam_init = 0.8 - 0.6 * math.exp(-0.3 * l)
        update_ctx = l < DEPTH - 1
        lam = (jnp.exp(jnp.sum(lambda_q1[l].astype(jnp.float32) * lambda_k1[l].astype(jnp.float32)))
               - jnp.exp(jnp.sum(lambda_q2[l].astype(jnp.float32) * lambda_k2[l].astype(jnp.float32)))
               + lam_init)
        sh, sc, gt = adaln(c, w_mod[l], b_mod[l])
        sh_c, sc_c, gt_c = adaln(c_ctx, w_mod[l], b_mod[l])
        h = rmsnorm(x, norm_g[l]) * (1.0 + sc[:, None]) + sh[:, None]
        hc = rmsnorm(ctx, norm_g[l]) * (1.0 + sc_c) + sh_c

        parts = split_cols(h @ w_in[l])
        k_lat = apply_rope(parts[4].reshape(bsz, length, DIFF_HEADS, 2, DIFF_HEAD_DIM), cos, sin)
        v_lat = parts[5].reshape(bsz, length, DIFF_HEADS, DIFF_V_DIM)
        if update_ctx:
            cparts = split_cols(hc @ w_in[l])
            kc_flat, vc_flat = cparts[4], cparts[5]
        else:
            kc_flat, vc_flat = jnp.split(hc @ w_in[l][:, KV_START:KV_END], 2, axis=-1)
        k_ctx = kc_flat.reshape(bsz, ctx_len, DIFF_HEADS, 2, DIFF_HEAD_DIM)
        v_ctx = vc_flat.reshape(bsz, ctx_len, DIFF_HEADS, DIFF_V_DIM)
        k_all = jnp.concatenate([k_ctx, k_lat], axis=1)
        v_all = jnp.concatenate([v_ctx, v_lat], axis=1)

        lw = (w_out[l], sgu_norm_g[l], sgu_w[l], sgu_b[l], subln_g[l], conv_w[l], conv_b[l])
        y_lat = mixer(parts, k_all, v_all, (cos, sin), lam, lam_init, *lw)
        if update_ctx:
            y_ctx = mixer(cparts, k_ctx, v_ctx, None, lam, lam_init, *lw)
            ctx = ctx + gt_c * y_ctx
        x = x + gt[:, None] * y_lat
    return rmsnorm(x, final_g)
```

```python
import functools
import math

import jax
import jax.numpy as jnp
from jax import lax
from jax.experimental import pallas as pl
from jax.experimental.pallas import tpu as pltpu

D_MODEL = 1024
DEPTH = 4
CTX_LEN = 256
GRID_W = 64

GMLP_HEADS = 4
GMLP_HEAD_DIM = 64
GMLP_CHUNK = 128
W_A = GMLP_HEADS * GMLP_HEAD_DIM
DIFF_HEADS = 4
DIFF_HEAD_DIM = 64
HEAD_W = 2 * DIFF_HEAD_DIM
W_B = DIFF_HEADS * HEAD_W
W_C = 256
D_MIX = W_A + W_B + W_C
D_IN = 3 * W_A + 4 * W_B + 4 * W_C

COL_A = 0
COL_Q = 3 * W_A
COL_K = COL_Q + W_B
COL_V = COL_K + W_B
COL_C = COL_V + 2 * W_B

ROPE_BASE = 10000.0
ROPE_PAIRS = DIFF_HEAD_DIM // 4
EPS = 1e-6

F32 = jnp.float32
BF16 = jnp.bfloat16

VMEM_LIMIT_BYTES = 56 * 1024 * 1024
MOD_ROWS = 8

IN_TILE = 384
OUT_TILE = 256
Q_TILE = 256
K_TILE = 512
HALO = 16


def _silu(x):
    return x * jax.nn.sigmoid(x)


def _mod_kernel(cond_ref, w_ref, b_ref, o_ref):
    a = _silu(cond_ref[...])
    o_ref[0, 0] = jnp.dot(a, w_ref[0], preferred_element_type=F32) + b_ref[0, 0]


def _modulation(cond, w_mod, b_mod):
    return pl.pallas_call(
        _mod_kernel,
        grid=(DEPTH, 3),
        in_specs=[
            pl.BlockSpec((MOD_ROWS, D_MODEL), lambda l, j: (0, 0)),
            pl.BlockSpec((1, D_MODEL, D_MODEL), lambda l, j: (l, 0, j)),
            pl.BlockSpec((1, 1, 1, D_MODEL), lambda l, j: (l, j, 0, 0)),
        ],
        out_specs=pl.BlockSpec((1, 1, MOD_ROWS, D_MODEL), lambda l, j: (l, j, 0, 0)),
        out_shape=jax.ShapeDtypeStruct((DEPTH, 3, MOD_ROWS, D_MODEL), F32),
        compiler_params=pltpu.CompilerParams(
            dimension_semantics=("arbitrary", "arbitrary"),
            vmem_limit_bytes=VMEM_LIMIT_BYTES),
        name="adaln_modulation",
    )(cond, w_mod, b_mod.reshape(DEPTH, 3, 1, D_MODEL))


def _rope(t, cos, sin_lo, sin_hi):
    return (t * cos + pltpu.roll(t, HEAD_W - ROPE_PAIRS, 1) * sin_lo
            + pltpu.roll(t, ROPE_PAIRS, 1) * sin_hi)


def _inproj_kernel(x_ref, mod_ref, g_ref, w_ref, cos_ref, slo_ref, shi_ref,
                   sgug_ref, ws_ref, bs_ref,
                   ya_ref, q_ref, k_ref, v_ref, gz_ref, p_ref, gc_ref, *, tm):
    i = pl.program_id(1)
    x = x_ref[0]
    row = i * tm + lax.broadcasted_iota(jnp.int32, (tm, 1), 0)
    is_ctx = row < CTX_LEN
    mod = mod_ref[0]
    shift = jnp.where(is_ctx, mod[3:4], mod[0:1])
    scale = jnp.where(is_ctx, mod[4:5], mod[1:2])
    h = x * lax.rsqrt(jnp.mean(x * x, axis=-1, keepdims=True) + EPS) * g_ref[...]
    h = (h * (1.0 + scale) + shift).astype(BF16)

    ra = jnp.dot(h, w_ref[:, COL_A:COL_A + 3 * W_A], preferred_element_type=F32)
    u = jax.nn.gelu(ra[:, 0:W_A])
    v = jax.nn.gelu(ra[:, W_A:2 * W_A])
    gate_a = _silu(ra[:, 2 * W_A:3 * W_A])
    vn = v * lax.rsqrt(jnp.mean(v * v, axis=-1, keepdims=True) + EPS) * sgug_ref[...]
    vn = vn.astype(BF16)
    lane = lax.broadcasted_iota(jnp.int32, (GMLP_CHUNK, W_A), 1)
    for c in range(tm // GMLP_CHUNK):
        rows = slice(c * GMLP_CHUNK, (c + 1) * GMLP_CHUNK)
        r = jnp.dot(ws_ref[...], vn[rows], preferred_element_type=F32)
        mixed = r[0:GMLP_CHUNK]
        for hd in range(1, GMLP_HEADS):
            mixed = jnp.where(lane >= hd * GMLP_HEAD_DIM,
                              r[hd * GMLP_CHUNK:(hd + 1) * GMLP_CHUNK], mixed)
        mixed = mixed + bs_ref[...]
        ya_ref[0, rows, :] = (u[rows] * mixed * gate_a[rows]).astype(BF16)

    cos, slo, shi = cos_ref[...], slo_ref[...], shi_ref[...]
    rq = jnp.dot(h, w_ref[:, COL_Q:COL_Q + W_B], preferred_element_type=F32)
    rk = jnp.dot(h, w_ref[:, COL_K:COL_K + W_B], preferred_element_type=F32)
    q_scale = 1.0 / math.sqrt(DIFF_HEAD_DIM)
    for hd in range(DIFF_HEADS):
        cols = slice(hd * HEAD_W, (hd + 1) * HEAD_W)
        q_ref[0, hd] = (_rope(rq[:, cols], cos, slo, shi) * q_scale).astype(BF16)
        k_ref[0, hd] = _rope(rk[:, cols], cos, slo, shi).astype(BF16)
    rv = jnp.dot(h, w_ref[:, COL_V:COL_V + 2 * W_B], preferred_element_type=F32)
    for hd in range(DIFF_HEADS):
        v_ref[0, hd] = rv[:, hd * HEAD_W:(hd + 1) * HEAD_W].astype(BF16)
    gz_ref[0] = _silu(rv[:, W_B:2 * W_B]).astype(BF16)

    rc = jnp.dot(h, w_ref[:, COL_C:COL_C + 4 * W_C], preferred_element_type=F32)
    p_ref[0] = (rc[:, W_C:2 * W_C] * rc[:, 2 * W_C:3 * W_C]).astype(BF16)
    gc_ref[0] = (rc[:, 0:W_C] * _silu(rc[:, 3 * W_C:4 * W_C])).astype(BF16)


def _inproj(xs, mod_l, norm_g, w_in, cos, slo, shi, sgu_g, ws, bs):
    bsz, lt, _ = xs.shape
    tm = IN_TILE
    tok = lambda width: pl.BlockSpec((1, tm, width), lambda b, i: (b, i, 0))
    head = pl.BlockSpec((1, DIFF_HEADS, tm, HEAD_W), lambda b, i: (b, 0, i, 0))
    const = lambda shape: pl.BlockSpec(shape, lambda b, i: (0,) * len(shape))
    tab = pl.BlockSpec((tm, HEAD_W), lambda b, i: (i, 0))
    tok_shape = lambda width: jax.ShapeDtypeStruct((bsz, lt, width), BF16)
    head_shape = jax.ShapeDtypeStruct((bsz, DIFF_HEADS, lt, HEAD_W), BF16)
    return pl.pallas_call(
        functools.partial(_inproj_kernel, tm=tm),
        grid=(bsz, lt // tm),
        in_specs=[
            tok(D_MODEL),
            pl.BlockSpec((1, MOD_ROWS, D_MODEL), lambda b, i: (b, 0, 0)),
            const((1, D_MODEL)),
            const((D_MODEL, D_IN)),
            tab, tab, tab,
            const((1, W_A)),
            const((GMLP_HEADS * GMLP_CHUNK, GMLP_CHUNK)),
            const((GMLP_CHUNK, W_A)),
        ],
        out_specs=[tok(W_A), head, head, head, tok(W_B), tok(W_C), tok(W_C)],
        out_shape=[tok_shape(W_A), head_shape, head_shape, head_shape,
                   tok_shape(W_B), tok_shape(W_C), tok_shape(W_C)],
        compiler_params=pltpu.CompilerParams(
            dimension_semantics=("arbitrary", "arbitrary"),
            vmem_limit_bytes=VMEM_LIMIT_BYTES),
        name="in_projection",
    )(xs, mod_l, norm_g, w_in, cos, slo, shi, sgu_g, ws, bs)


def _attn_kernel(lq1_ref, lk1_ref, lq2_ref, lk2_ref, q_ref, kc_ref, vc_ref, k_ref, v_ref,
                 gz_ref, sg_ref, o_ref, qz_sc, sc_sc, sl_sc, *, tq, tk, q_off, lam_init):
    i = pl.program_id(2) + q_off
    n_lat = jnp.where(i * tq < CTX_LEN, 0, k_ref.shape[2] // tk)

    q = q_ref[0, 0]
    lane = lax.broadcasted_iota(jnp.int32, (tq, HEAD_W), 1)
    zero = jnp.zeros_like(q)
    qz_sc[0:tq] = jnp.where(lane < DIFF_HEAD_DIM, q, zero)
    qz_sc[tq:2 * tq] = jnp.where(lane >= DIFF_HEAD_DIM, q, zero)
    qz = qz_sc[...]

    nt = (((1,), (1,)), ((), ()))

    def lane_fold(t, op):
        out = t[:, 0:128]
        for c in range(1, t.shape[1] // 128):
            out = op(out, t[:, c * 128:(c + 1) * 128])
        return out

    s0 = lax.dot_general(qz, kc_ref[0, 0], nt, preferred_element_type=F32)
    sc_sc[...] = s0
    m0 = lane_fold(s0, jnp.maximum)

    def score_step(j, m):
        off = pl.multiple_of(CTX_LEN + j * tk, 128)
        s = lax.dot_general(qz, k_ref[0, 0, pl.ds(off, tk), :], nt, preferred_element_type=F32)
        sl_sc[j] = s
        return jnp.maximum(m, lane_fold(s, jnp.maximum))

    m = lax.fori_loop(0, n_lat, score_step, m0)
    m = jnp.max(m, axis=-1, keepdims=True)

    p0 = jnp.exp(sc_sc[...] - m)
    l0 = lane_fold(p0, jnp.add)
    acc0 = jnp.dot(p0.astype(BF16), vc_ref[0, 0], preferred_element_type=F32)

    def value_step(j, carry):
        l, acc = carry
        off = pl.multiple_of(CTX_LEN + j * tk, 128)
        p = jnp.exp(sl_sc[j] - m)
        l = l + lane_fold(p, jnp.add)
        acc = acc + jnp.dot(p.astype(BF16), v_ref[0, 0, pl.ds(off, tk), :],
                            preferred_element_type=F32)
        return l, acc

    l, acc = lax.fori_loop(0, n_lat, value_step, (l0, acc0))
    l = jnp.sum(l, axis=-1, keepdims=True)
    o = acc / l

    lam = (jnp.exp(jnp.sum(lq1_ref[...] * lk1_ref[...]))
           - jnp.exp(jnp.sum(lq2_ref[...] * lk2_ref[...])) + lam_init)
    o = o[0:tq] - lam * o[tq:2 * tq]
    o = o * lax.rsqrt(jnp.mean(o * o, axis=-1, keepdims=True) + EPS) * sg_ref[...]
    o_ref[0] = (o * (1.0 - lam_init) * gz_ref[0].astype(F32)).astype(BF16)


def _attention(q, k, v, gz, lq1, lk1, lq2, lk2, subln_g, lam_init, q_off):
    bsz, _, lt, _ = q.shape
    tq, tk = Q_TILE, K_TILE
    n_q = lt // tq - q_off
    n_lat = (lt - CTX_LEN) // tk
    lam_spec = pl.BlockSpec((1, DIFF_HEAD_DIM), lambda b, h, i: (0, 0))
    seq = pl.BlockSpec((1, 1, lt, HEAD_W), lambda b, h, i: (b, h, 0, 0))
    ctx_blk = pl.BlockSpec((1, 1, CTX_LEN, HEAD_W), lambda b, h, i: (b, h, 0, 0))
    return pl.pallas_call(
        functools.partial(_attn_kernel, tq=tq, tk=tk, q_off=q_off, lam_init=lam_init),
        grid=(bsz, DIFF_HEADS, n_q),
        in_specs=[
            lam_spec, lam_spec, lam_spec, lam_spec,
            pl.BlockSpec((1, 1, tq, HEAD_W), lambda b, h, i: (b, h, i + q_off, 0)),
            ctx_blk, ctx_blk, seq, seq,
            pl.BlockSpec((1, tq, HEAD_W), lambda b, h, i: (b, i + q_off, h)),
            pl.BlockSpec((1, HEAD_W), lambda b, h, i: (0, 0)),
        ],
        out_specs=pl.BlockSpec((1, tq, HEAD_W), lambda b, h, i: (b, i + q_off, h)),
        out_shape=jax.ShapeDtypeStruct((bsz, lt, W_B), BF16),
        scratch_shapes=[
            pltpu.VMEM((2 * tq, HEAD_W), BF16),
            pltpu.VMEM((2 * tq, CTX_LEN), F32),
            pltpu.VMEM((n_lat, 2 * tq, tk), F32),
        ],
        compiler_params=pltpu.CompilerParams(
            dimension_semantics=("arbitrary", "arbitrary", "arbitrary"),
            vmem_limit_bytes=VMEM_LIMIT_BYTES),
        name="diff_attention",
    )(lq1, lk1, lq2, lk2, q, k, v, k, v, gz, subln_g)


def _outproj_kernel(x_ref, mod_ref, ya_ref, yb_ref, p_ref, pprev_ref, pnext_ref, gc_ref,
                    cw_ref, cb_ref, w_ref, fg_ref, o_ref, *, tm, row_off, lt, final):
    i = pl.program_id(1)
    row = row_off + i * tm + lax.broadcasted_iota(jnp.int32, (tm, 1), 0)
    local = lax.broadcasted_iota(jnp.int32, (tm, 1), 0)
    p = p_ref[0].astype(F32)
    prev_row = pprev_ref[0, HALO - 1:HALO, :].astype(F32)
    next_row = pnext_ref[0, 0:1, :].astype(F32)
    up = jnp.where(local == 0, prev_row, pltpu.roll(p, 1, 0))
    dn = jnp.where(local == tm - 1, next_row, pltpu.roll(p, tm - 1, 0))
    up = jnp.where((row == 0) | (row == CTX_LEN), 0.0, up)
    dn = jnp.where((row == CTX_LEN - 1) | (row == lt - 1), 0.0, dn)
    cw = cw_ref[...]
    conv = up * cw[0:1] + p * cw[1:2] + dn * cw[2:3] + cb_ref[...]
    yc = (gc_ref[0].astype(F32) * conv).astype(BF16)

    y = jnp.dot(ya_ref[0], w_ref[0:W_A], preferred_element_type=F32)
    y = y + jnp.dot(yb_ref[0], w_ref[W_A:W_A + W_B], preferred_element_type=F32)
    y = y + jnp.dot(yc, w_ref[W_A + W_B:D_MIX], preferred_element_type=F32)

    mod = mod_ref[0]
    gate = jnp.where(row < CTX_LEN, mod[5:6], mod[2:3])
    x = x_ref[0] + gate * y
    if final:
        x = x * lax.rsqrt(jnp.mean(x * x, axis=-1, keepdims=True) + EPS) * fg_ref[...]
    o_ref[0] = x


def _outproj(xs, mod_l, ya, yb, p, gc, conv_w, conv_b, w_out, final_g, final):
    bsz, lt, _ = xs.shape
    tm = OUT_TILE
    blk_off = CTX_LEN // tm if final else 0
    n_t = lt // tm - blk_off
    hpt = tm // HALO
    n_halo = lt // HALO
    tok = lambda width: pl.BlockSpec((1, tm, width), lambda b, i: (b, i + blk_off, 0))
    const = lambda shape: pl.BlockSpec(shape, lambda b, i: (0,) * len(shape))
    prev = pl.BlockSpec((1, HALO, W_C),
                        lambda b, i: (b, jnp.maximum((i + blk_off) * hpt - 1, 0), 0))
    nxt = pl.BlockSpec((1, HALO, W_C),
                       lambda b, i: (b, jnp.minimum((i + blk_off + 1) * hpt, n_halo - 1), 0))
    out_rows = n_t * tm
    return pl.pallas_call(
        functools.partial(_outproj_kernel, tm=tm, row_off=blk_off * tm, lt=lt, final=final),
        grid=(bsz, n_t),
        in_specs=[
            tok(D_MODEL),
            pl.BlockSpec((1, MOD_ROWS, D_MODEL), lambda b, i: (b, 0, 0)),
            tok(W_A), tok(W_B), tok(W_C), prev, nxt, tok(W_C),
            const((3, W_C)), const((1, W_C)), const((D_MIX, D_MODEL)), const((1, D_MODEL)),
        ],
        out_specs=pl.BlockSpec((1, tm, D_MODEL), lambda b, i: (b, i, 0)),
        out_shape=jax.ShapeDtypeStruct((bsz, out_rows, D_MODEL), F32),
        compiler_params=pltpu.CompilerParams(
            dimension_semantics=("arbitrary", "arbitrary"),
            vmem_limit_bytes=VMEM_LIMIT_BYTES),
        name="out_projection",
    )(xs, mod_l, ya, yb, p, p, p, gc, conv_w, conv_b, w_out, final_g)


def _rope_tables(length):
    rows = length // GRID_W
    t = jnp.arange(rows * GRID_W)
    pos = jnp.stack([t // GRID_W, t % GRID_W], axis=1).astype(F32)
    inv = ROPE_BASE ** (-jnp.arange(ROPE_PAIRS, dtype=F32) / ROPE_PAIRS)
    ang = pos[:, :, None] * inv
    lane = jnp.arange(HEAD_W)
    axis = (lane % DIFF_HEAD_DIM) // (2 * ROPE_PAIRS)
    pair = lane % ROPE_PAIRS
    second = (lane % (2 * ROPE_PAIRS)) >= ROPE_PAIRS
    cos = jnp.cos(ang)[:, axis, pair]
    sin = jnp.sin(ang)[:, axis, pair]
    sin_lo = jnp.where(second, 0.0, -sin)
    sin_hi = jnp.where(second, sin, 0.0)
    ident = jnp.ones((CTX_LEN, HEAD_W), F32)
    zeros = jnp.zeros((CTX_LEN, HEAD_W), F32)
    return (jnp.concatenate([ident, cos]), jnp.concatenate([zeros, sin_lo]),
            jnp.concatenate([zeros, sin_hi]))


def kernel(x, c, ctx, c_ctx, w_mod, b_mod, norm_g, w_in, w_out, sgu_norm_g, sgu_w, sgu_b,
           lambda_q1, lambda_k1, lambda_q2, lambda_k2, subln_g, conv_w, conv_b, final_g):
    bsz, length, _ = x.shape
    assert ctx.shape[1] == CTX_LEN and bsz + 1 <= MOD_ROWS
    lt = CTX_LEN + length
    assert lt % IN_TILE == 0 and lt % OUT_TILE == 0 and CTX_LEN % OUT_TILE == 0
    assert CTX_LEN % Q_TILE == 0 and length % K_TILE == 0

    cond = jnp.zeros((MOD_ROWS, D_MODEL), F32).at[:bsz].set(c).at[bsz].set(c_ctx)
    mod = _modulation(cond, w_mod, b_mod)
    mod_b = jnp.transpose(mod[:, :, :bsz], (0, 2, 1, 3))
    mod_c = jnp.broadcast_to(mod[:, None, :, bsz], (DEPTH, bsz, 3, D_MODEL))
    pad = jnp.zeros((DEPTH, bsz, MOD_ROWS - 6, D_MODEL), F32)
    mod_rows = jnp.concatenate([mod_b, mod_c, pad], axis=2)

    cos, slo, shi = _rope_tables(length)
    w_in_b = w_in.astype(BF16)
    w_out_b = w_out.astype(BF16)
    ws = sgu_w.reshape(DEPTH, GMLP_HEADS * GMLP_CHUNK, GMLP_CHUNK).astype(BF16)
    bs = jnp.repeat(jnp.transpose(sgu_b, (0, 2, 1)), GMLP_HEAD_DIM, axis=2)

    xs = jnp.concatenate([ctx, x], axis=1)
    for l in range(DEPTH):
        final = l == DEPTH - 1
        lam_init = 0.8 - 0.6 * math.exp(-0.3 * l)
        ya, q, k, v, gz, p, gc = _inproj(
            xs, mod_rows[l], norm_g[l][None], w_in_b[l], cos, slo, shi,
            sgu_norm_g[l][None], ws[l], bs[l])
        yb = _attention(q, k, v, gz, lambda_q1[l][None], lambda_k1[l][None],
                        lambda_q2[l][None], lambda_k2[l][None], subln_g[l][None],
                        lam_init, q_off=CTX_LEN // Q_TILE if final else 0)
        xs = _outproj(xs, mod_rows[l], ya, yb, p, gc, conv_w[l], conv_b[l][None],
                      w_out_b[l], final_g[None], final)
    return xs
```

```python
import functools
import math

import jax
import jax.numpy as jnp
from jax import lax
from jax.experimental import pallas as pl
from jax.experimental.pallas import tpu as pltpu

D_MODEL = 1024
DEPTH = 4
CTX_LEN = 256
GRID_W = 64

GMLP_HEADS = 4
GMLP_HEAD_DIM = 64
GMLP_CHUNK = 128
W_A = GMLP_HEADS * GMLP_HEAD_DIM
DIFF_HEADS = 4
DIFF_HEAD_DIM = 64
HEAD_W = 2 * DIFF_HEAD_DIM
W_B = DIFF_HEADS * HEAD_W
W_C = 256
D_MIX = W_A + W_B + W_C
D_IN = 3 * W_A + 4 * W_B + 4 * W_C

COL_A = 0
COL_Q = 3 * W_A
COL_K = COL_Q + W_B
COL_V = COL_K + W_B
COL_C = COL_V + 2 * W_B

ROPE_BASE = 10000.0
ROPE_PAIRS = DIFF_HEAD_DIM // 4
EPS = 1e-6

F32 = jnp.float32
BF16 = jnp.bfloat16

VMEM_LIMIT_BYTES = 56 * 1024 * 1024
MOD_ROWS = 8

IN_TILE = 384
OUT_TILE = 256
Q_TILE = 256
K_TILE = 2816
HALO = 16


def _silu(x):
    return x * jax.nn.sigmoid(x)


def _mod_kernel(cond_ref, w_ref, b_ref, o_ref):
    a = _silu(cond_ref[...])
    o_ref[0, 0] = jnp.dot(a, w_ref[0], preferred_element_type=F32) + b_ref[0, 0]


def _modulation(cond, w_mod, b_mod):
    return pl.pallas_call(
        _mod_kernel,
        grid=(DEPTH, 3),
        in_specs=[
            pl.BlockSpec((MOD_ROWS, D_MODEL), lambda l, j: (0, 0)),
            pl.BlockSpec((1, D_MODEL, D_MODEL), lambda l, j: (l, 0, j)),
            pl.BlockSpec((1, 1, 1, D_MODEL), lambda l, j: (l, j, 0, 0)),
        ],
        out_specs=pl.BlockSpec((1, 1, MOD_ROWS, D_MODEL), lambda l, j: (l, j, 0, 0)),
        out_shape=jax.ShapeDtypeStruct((DEPTH, 3, MOD_ROWS, D_MODEL), F32),
        compiler_params=pltpu.CompilerParams(
            dimension_semantics=("arbitrary", "arbitrary"),
            vmem_limit_bytes=VMEM_LIMIT_BYTES),
        name="adaln_modulation",
    )(cond, w_mod, b_mod.reshape(DEPTH, 3, 1, D_MODEL))


def _rope(t, cos, sin_lo, sin_hi):
    return (t * cos + pltpu.roll(t, HEAD_W - ROPE_PAIRS, 1) * sin_lo
            + pltpu.roll(t, ROPE_PAIRS, 1) * sin_hi)


def _inproj_kernel(x_ref, mod_ref, g_ref, w_ref, cos_ref, slo_ref, shi_ref,
                   sgug_ref, ws_ref, bs_ref,
                   ya_ref, q_ref, k_ref, v_ref, gz_ref, p_ref, gc_ref, *, tm):
    i = pl.program_id(1)
    x = x_ref[0]
    row = i * tm + lax.broadcasted_iota(jnp.int32, (tm, 1), 0)
    is_ctx = row < CTX_LEN
    mod = mod_ref[0]
    shift = jnp.where(is_ctx, mod[3:4], mod[0:1])
    scale = jnp.where(is_ctx, mod[4:5], mod[1:2])
    h = x * lax.rsqrt(jnp.mean(x * x, axis=-1, keepdims=True) + EPS) * g_ref[...]
    h = (h * (1.0 + scale) + shift).astype(BF16)

    ra = jnp.dot(h, w_ref[:, COL_A:COL_A + 3 * W_A], preferred_element_type=F32)
    u = jax.nn.gelu(ra[:, 0:W_A])
    v = jax.nn.gelu(ra[:, W_A:2 * W_A])
    gate_a = _silu(ra[:, 2 * W_A:3 * W_A])
    vn = v * lax.rsqrt(jnp.mean(v * v, axis=-1, keepdims=True) + EPS) * sgug_ref[...]
    vn = vn.astype(BF16)
    lane = lax.broadcasted_iota(jnp.int32, (GMLP_CHUNK, W_A), 1)
    for c in range(tm // GMLP_CHUNK):
        rows = slice(c * GMLP_CHUNK, (c + 1) * GMLP_CHUNK)
        r = jnp.dot(ws_ref[...], vn[rows], preferred_element_type=F32)
        mixed = r[0:GMLP_CHUNK]
        for hd in range(1, GMLP_HEADS):
            mixed = jnp.where(lane >= hd * GMLP_HEAD_DIM,
                              r[hd * GMLP_CHUNK:(hd + 1) * GMLP_CHUNK], mixed)
        mixed = mixed + bs_ref[...]
        ya_ref[0, rows, :] = (u[rows] * mixed * gate_a[rows]).astype(BF16)

    cos, slo, shi = cos_ref[...], slo_ref[...], shi_ref[...]
    rq = jnp.dot(h, w_ref[:, COL_Q:COL_Q + W_B], preferred_element_type=F32)
    rk = jnp.dot(h, w_ref[:, COL_K:COL_K + W_B], preferred_element_type=F32)
    q_scale = math.log2(math.e) / math.sqrt(DIFF_HEAD_DIM)
    for hd in range(DIFF_HEADS):
        cols = slice(hd * HEAD_W, (hd + 1) * HEAD_W)
        q_ref[0, hd] = (_rope(rq[:, cols], cos, slo, shi) * q_scale).astype(BF16)
        k_ref[0, hd] = _rope(rk[:, cols], cos, slo, shi).astype(BF16)
    rv = jnp.dot(h, w_ref[:, COL_V:COL_V + 2 * W_B], preferred_element_type=F32)
    for hd in range(DIFF_HEADS):
        v_ref[0, hd] = rv[:, hd * HEAD_W:(hd + 1) * HEAD_W].astype(BF16)
    gz_ref[0] = _silu(rv[:, W_B:2 * W_B]).astype(BF16)

    rc = jnp.dot(h, w_ref[:, COL_C:COL_C + 4 * W_C], preferred_element_type=F32)
    p_ref[0] = (rc[:, W_C:2 * W_C] * rc[:, 2 * W_C:3 * W_C]).astype(BF16)
    gc_ref[0] = (rc[:, 0:W_C] * _silu(rc[:, 3 * W_C:4 * W_C])).astype(BF16)


def _inproj(xs, mod_l, norm_g, w_in, cos, slo, shi, sgu_g, ws, bs):
    bsz, lt, _ = xs.shape
    tm = IN_TILE
    tok = lambda width: pl.BlockSpec((1, tm, width), lambda b, i: (b, i, 0))
    head = pl.BlockSpec((1, DIFF_HEADS, tm, HEAD_W), lambda b, i: (b, 0, i, 0))
    const = lambda shape: pl.BlockSpec(shape, lambda b, i: (0,) * len(shape))
    tab = pl.BlockSpec((tm, HEAD_W), lambda b, i: (i, 0))
    tok_shape = lambda width: jax.ShapeDtypeStruct((bsz, lt, width), BF16)
    head_shape = jax.ShapeDtypeStruct((bsz, DIFF_HEADS, lt, HEAD_W), BF16)
    return pl.pallas_call(
        functools.partial(_inproj_kernel, tm=tm),
        grid=(bsz, lt // tm),
        in_specs=[
            tok(D_MODEL),
            pl.BlockSpec((1, MOD_ROWS, D_MODEL), lambda b, i: (b, 0, 0)),
            const((1, D_MODEL)),
            const((D_MODEL, D_IN)),
            tab, tab, tab,
            const((1, W_A)),
            const((GMLP_HEADS * GMLP_CHUNK, GMLP_CHUNK)),
            const((GMLP_CHUNK, W_A)),
        ],
        out_specs=[tok(W_A), head, head, head, tok(W_B), tok(W_C), tok(W_C)],
        out_shape=[tok_shape(W_A), head_shape, head_shape, head_shape,
                   tok_shape(W_B), tok_shape(W_C), tok_shape(W_C)],
        compiler_params=pltpu.CompilerParams(
            dimension_semantics=("arbitrary", "arbitrary"),
            vmem_limit_bytes=VMEM_LIMIT_BYTES),
        name="in_projection",
    )(xs, mod_l, norm_g, w_in, cos, slo, shi, sgu_g, ws, bs)


def _attn_kernel(*refs, tq, tk, n_chunks, lam_init, aliased):
    (lq1_ref, lk1_ref, lq2_ref, lk2_ref, q_ref, k_ref, v_ref, gz_ref, sg_ref) = refs[:9]
    o_ref, qz_sc, v_sc, s_sc = refs[9 + aliased:]

    @pl.when(pl.program_id(2) == 0)
    def _():
        v_sc[:, 0:HEAD_W] = v_ref[0, 0]
        v_sc[:, HEAD_W:2 * HEAD_W] = jnp.ones((v_sc.shape[0], HEAD_W), BF16)

    q = q_ref[0, 0]
    lane = lax.broadcasted_iota(jnp.int32, (tq, HEAD_W), 1)
    zero = jnp.zeros_like(q)
    qz_sc[0:tq] = jnp.where(lane < DIFF_HEAD_DIM, q, zero)
    qz_sc[tq:2 * tq] = jnp.where(lane >= DIFF_HEAD_DIM, q, zero)
    qz = qz_sc[...]

    def score_step(c, m):
        off = pl.multiple_of(c * tk, tk)
        s = lax.dot_general(qz, k_ref[0, 0, pl.ds(off, tk), :], (((1,), (1,)), ((), ())),
                            preferred_element_type=F32)
        s_sc[c] = s
        for t in range(tk // 128):
            m = jnp.maximum(m, s[:, t * 128:(t + 1) * 128])
        return m

    m = lax.fori_loop(0, n_chunks, score_step, jnp.full((2 * tq, 128), -jnp.inf, F32))
    m = jnp.max(m, axis=-1, keepdims=True)

    def value_step(c, acc):
        off = pl.multiple_of(c * tk, tk)
        p = jnp.exp2(s_sc[c] - m).astype(BF16)
        return acc + jnp.dot(p, v_sc[pl.ds(off, tk), :], preferred_element_type=F32)

    acc = lax.fori_loop(0, n_chunks, value_step, jnp.zeros((2 * tq, 2 * HEAD_W), F32))
    o = acc[:, 0:HEAD_W] / acc[:, HEAD_W:2 * HEAD_W]

    lam = (jnp.exp(jnp.sum(lq1_ref[...] * lk1_ref[...]))
           - jnp.exp(jnp.sum(lq2_ref[...] * lk2_ref[...])) + lam_init)
    o = o[0:tq] - lam * o[tq:2 * tq]
    o = o * lax.rsqrt(jnp.mean(o * o, axis=-1, keepdims=True) + EPS) * sg_ref[...]
    o_ref[0] = (o * (1.0 - lam_init) * gz_ref[0].astype(F32)).astype(BF16)


def _attention(q, k, v, gz, lams, subln_g, lam_init, *, q_rows, n_keys, tk, out_init=None):
    bsz, _, lt, _ = q.shape
    tq = Q_TILE
    q_blk0 = q_rows[0] // tq
    n_q = (q_rows[1] - q_rows[0]) // tq
    n_chunks = n_keys // tk
    aliased = out_init is not None
    lam_spec = pl.BlockSpec((1, DIFF_HEAD_DIM), lambda b, h, i: (0, 0))
    keys = pl.BlockSpec((1, 1, n_keys, HEAD_W), lambda b, h, i: (b, h, 0, 0))
    in_specs = [
        lam_spec, lam_spec, lam_spec, lam_spec,
        pl.BlockSpec((1, 1, tq, HEAD_W), lambda b, h, i: (b, h, i + q_blk0, 0)),
        keys, keys,
        pl.BlockSpec((1, tq, HEAD_W), lambda b, h, i: (b, i + q_blk0, h)),
        pl.BlockSpec((1, HEAD_W), lambda b, h, i: (0, 0)),
    ]
    args = [*lams, q, k, v, gz, subln_g]
    if aliased:
        in_specs.append(pl.BlockSpec(memory_space=pl.ANY))
        args.append(out_init)
    return pl.pallas_call(
        functools.partial(_attn_kernel, tq=tq, tk=tk, n_chunks=n_chunks, lam_init=lam_init,
                          aliased=aliased),
        grid=(bsz, DIFF_HEADS, n_q),
        in_specs=in_specs,
        out_specs=pl.BlockSpec((1, tq, HEAD_W), lambda b, h, i: (b, i + q_blk0, h)),
        out_shape=jax.ShapeDtypeStruct((bsz, lt, W_B), BF16),
        input_output_aliases={len(args) - 1: 0} if aliased else {},
        scratch_shapes=[
            pltpu.VMEM((2 * tq, HEAD_W), BF16),
            pltpu.VMEM((n_keys, 2 * HEAD_W), BF16),
            pltpu.VMEM((n_chunks, 2 * tq, tk), F32),
        ],
        compiler_params=pltpu.CompilerParams(
            dimension_semantics=("arbitrary", "arbitrary", "arbitrary"),
            vmem_limit_bytes=VMEM_LIMIT_BYTES),
        name="diff_attention",
    )(*args)


def _outproj_kernel(x_ref, mod_ref, ya_ref, yb_ref, p_ref, pprev_ref, pnext_ref, gc_ref,
                    cw_ref, cb_ref, w_ref, fg_ref, o_ref, *, tm, row_off, lt, final):
    i = pl.program_id(1)
    row = row_off + i * tm + lax.broadcasted_iota(jnp.int32, (tm, 1), 0)
    local = lax.broadcasted_iota(jnp.int32, (tm, 1), 0)
    p = p_ref[0].astype(F32)
    prev_row = pprev_ref[0, HALO - 1:HALO, :].astype(F32)
    next_row = pnext_ref[0, 0:1, :].astype(F32)
    up = jnp.where(local == 0, prev_row, pltpu.roll(p, 1, 0))
    dn = jnp.where(local == tm - 1, next_row, pltpu.roll(p, tm - 1, 0))
    up = jnp.where((row == 0) | (row == CTX_LEN), 0.0, up)
    dn = jnp.where((row == CTX_LEN - 1) | (row == lt - 1), 0.0, dn)
    cw = cw_ref[...]
    conv = up * cw[0:1] + p * cw[1:2] + dn * cw[2:3] + cb_ref[...]
    yc = (gc_ref[0].astype(F32) * conv).astype(BF16)

    y = jnp.dot(ya_ref[0], w_ref[0:W_A], preferred_element_type=F32)
    y = y + jnp.dot(yb_ref[0], w_ref[W_A:W_A + W_B], preferred_element_type=F32)
    y = y + jnp.dot(yc, w_ref[W_A + W_B:D_MIX], preferred_element_type=F32)

    mod = mod_ref[0]
    gate = jnp.where(row < CTX_LEN, mod[5:6], mod[2:3])
    x = x_ref[0] + gate * y
    if final:
        x = x * lax.rsqrt(jnp.mean(x * x, axis=-1, keepdims=True) + EPS) * fg_ref[...]
    o_ref[0] = x


def _outproj(xs, mod_l, ya, yb, p, gc, conv_w, conv_b, w_out, final_g, final):
    bsz, lt, _ = xs.shape
    tm = OUT_TILE
    blk_off = CTX_LEN // tm if final else 0
    n_t = lt // tm - blk_off
    hpt = tm // HALO
    n_halo = lt // HALO
    tok = lambda width: pl.BlockSpec((1, tm, width), lambda b, i: (b, i + blk_off, 0))
    const = lambda shape: pl.BlockSpec(shape, lambda b, i: (0,) * len(shape))
    prev = pl.BlockSpec((1, HALO, W_C),
                        lambda b, i: (b, jnp.maximum((i + blk_off) * hpt - 1, 0), 0))
    nxt = pl.BlockSpec((1, HALO, W_C),
                       lambda b, i: (b, jnp.minimum((i + blk_off + 1) * hpt, n_halo - 1), 0))
    out_rows = n_t * tm
    return pl.pallas_call(
        functools.partial(_outproj_kernel, tm=tm, row_off=blk_off * tm, lt=lt, final=final),
        grid=(bsz, n_t),
        in_specs=[
            tok(D_MODEL),
            pl.BlockSpec((1, MOD_ROWS, D_MODEL), lambda b, i: (b, 0, 0)),
            tok(W_A), tok(W_B), tok(W_C), prev, nxt, tok(W_C),
            const((3, W_C)), const((1, W_C)), const((D_MIX, D_MODEL)), const((1, D_MODEL)),
        ],
        out_specs=pl.BlockSpec((1, tm, D_MODEL), lambda b, i: (b, i, 0)),
        out_shape=jax.ShapeDtypeStruct((bsz, out_rows, D_MODEL), F32),
        compiler_params=pltpu.CompilerParams(
            dimension_semantics=("arbitrary", "arbitrary"),
            vmem_limit_bytes=VMEM_LIMIT_BYTES),
        name="out_projection",
    )(xs, mod_l, ya, yb, p, p, p, gc, conv_w, conv_b, w_out, final_g)


def _rope_tables(length):
    rows = length // GRID_W
    t = jnp.arange(rows * GRID_W)
    pos = jnp.stack([t // GRID_W, t % GRID_W], axis=1).astype(F32)
    inv = ROPE_BASE ** (-jnp.arange(ROPE_PAIRS, dtype=F32) / ROPE_PAIRS)
    ang = pos[:, :, None] * inv
    lane = jnp.arange(HEAD_W)
    axis = (lane % DIFF_HEAD_DIM) // (2 * ROPE_PAIRS)
    pair = lane % ROPE_PAIRS
    second = (lane % (2 * ROPE_PAIRS)) >= ROPE_PAIRS
    cos = jnp.cos(ang)[:, axis, pair]
    sin = jnp.sin(ang)[:, axis, pair]
    sin_lo = jnp.where(second, 0.0, -sin)
    sin_hi = jnp.where(second, sin, 0.0)
    ident = jnp.ones((CTX_LEN, HEAD_W), F32)
    zeros = jnp.zeros((CTX_LEN, HEAD_W), F32)
    return (jnp.concatenate([ident, cos]), jnp.concatenate([zeros, sin_lo]),
            jnp.concatenate([zeros, sin_hi]))


def kernel(x, c, ctx, c_ctx, w_mod, b_mod, norm_g, w_in, w_out, sgu_norm_g, sgu_w, sgu_b,
           lambda_q1, lambda_k1, lambda_q2, lambda_k2, subln_g, conv_w, conv_b, final_g):
    bsz, length, _ = x.shape
    assert ctx.shape[1] == CTX_LEN and bsz + 1 <= MOD_ROWS
    lt = CTX_LEN + length
    assert lt % IN_TILE == 0 and lt % OUT_TILE == 0 and CTX_LEN % OUT_TILE == 0
    assert CTX_LEN % Q_TILE == 0 and lt % K_TILE == 0

    cond = jnp.zeros((MOD_ROWS, D_MODEL), F32).at[:bsz].set(c).at[bsz].set(c_ctx)
    mod = _modulation(cond, w_mod, b_mod)
    mod_b = jnp.transpose(mod[:, :, :bsz], (0, 2, 1, 3))
    mod_c = jnp.broadcast_to(mod[:, None, :, bsz], (DEPTH, bsz, 3, D_MODEL))
    pad = jnp.zeros((DEPTH, bsz, MOD_ROWS - 6, D_MODEL), F32)
    mod_rows = jnp.concatenate([mod_b, mod_c, pad], axis=2)

    cos, slo, shi = _rope_tables(length)
    w_in_b = w_in.astype(BF16)
    w_out_b = w_out.astype(BF16)
    ws = sgu_w.reshape(DEPTH, GMLP_HEADS * GMLP_CHUNK, GMLP_CHUNK).astype(BF16)
    bs = jnp.repeat(jnp.transpose(sgu_b, (0, 2, 1)), GMLP_HEAD_DIM, axis=2)

    xs = jnp.concatenate([ctx, x], axis=1)
    for l in range(DEPTH):
        final = l == DEPTH - 1
        lam_init = 0.8 - 0.6 * math.exp(-0.3 * l)
        ya, q, k, v, gz, p, gc = _inproj(
            xs, mod_rows[l], norm_g[l][None], w_in_b[l], cos, slo, shi,
            sgu_norm_g[l][None], ws[l], bs[l])
        lams = (lambda_q1[l][None], lambda_k1[l][None], lambda_q2[l][None], lambda_k2[l][None])
        yb = None if final else _attention(
            q, k, v, gz, lams, subln_g[l][None], lam_init,
            q_rows=(0, CTX_LEN), n_keys=CTX_LEN, tk=CTX_LEN)
        yb = _attention(q, k, v, gz, lams, subln_g[l][None], lam_init,
                        q_rows=(CTX_LEN, lt), n_keys=lt, tk=K_TILE, out_init=yb)
        xs = _outproj(xs, mod_rows[l], ya, yb, p, gc, conv_w[l], conv_b[l][None],
                      w_out_b[l], final_g[None], final)
    return xs
```

```python
import functools
import math

import jax
import jax.numpy as jnp
from jax import lax
from jax.experimental import pallas as pl
from jax.experimental.pallas import tpu as pltpu

D_MODEL = 1024
DEPTH = 4
CTX_LEN = 256
GRID_W = 64

GMLP_HEADS = 4
GMLP_HEAD_DIM = 64
GMLP_CHUNK = 128
W_A = GMLP_HEADS * GMLP_HEAD_DIM
DIFF_HEADS = 4
DIFF_HEAD_DIM = 64
HEAD_W = 2 * DIFF_HEAD_DIM
W_B = DIFF_HEADS * HEAD_W
W_C = 256
D_MIX = W_A + W_B + W_C
D_IN = 3 * W_A + 4 * W_B + 4 * W_C

COL_A = 0
COL_Q = 3 * W_A
COL_K = COL_Q + W_B
COL_V = COL_K + W_B
COL_C = COL_V + 2 * W_B

ROPE_BASE = 10000.0
ROPE_PAIRS = DIFF_HEAD_DIM // 4
EPS = 1e-6

F32 = jnp.float32
BF16 = jnp.bfloat16

VMEM_LIMIT_BYTES = 56 * 1024 * 1024
MOD_ROWS = 8

IN_TILE = 384
OUT_TILE = 768
FINAL_OUT_TILE = 256
Q_TILE = 256
K_TILE = 2816
V_TILE = 256
HALO = 16


def _silu(x):
    return x * jax.nn.sigmoid(x)


def _mod_kernel(cond_ref, w_ref, b_ref, o_ref):
    a = _silu(cond_ref[...])
    o_ref[0, 0] = jnp.dot(a, w_ref[0], preferred_element_type=F32) + b_ref[0, 0]


def _modulation(cond, w_mod, b_mod):
    return pl.pallas_call(
        _mod_kernel,
        grid=(DEPTH, 3),
        in_specs=[
            pl.BlockSpec((MOD_ROWS, D_MODEL), lambda l, j: (0, 0)),
            pl.BlockSpec((1, D_MODEL, D_MODEL), lambda l, j: (l, 0, j)),
            pl.BlockSpec((1, 1, 1, D_MODEL), lambda l, j: (l, j, 0, 0)),
        ],
        out_specs=pl.BlockSpec((1, 1, MOD_ROWS, D_MODEL), lambda l, j: (l, j, 0, 0)),
        out_shape=jax.ShapeDtypeStruct((DEPTH, 3, MOD_ROWS, D_MODEL), F32),
        compiler_params=pltpu.CompilerParams(
            dimension_semantics=("arbitrary", "arbitrary"),
            vmem_limit_bytes=VMEM_LIMIT_BYTES),
        name="adaln_modulation",
    )(cond, w_mod, b_mod.reshape(DEPTH, 3, 1, D_MODEL))


def _rope(t, cos, sin_lo, sin_hi):
    return (t * cos + pltpu.roll(t, HEAD_W - ROPE_PAIRS, 1) * sin_lo
            + pltpu.roll(t, ROPE_PAIRS, 1) * sin_hi)


def _inproj_kernel(x_ref, mod_ref, g_ref, w_ref, cos_ref, slo_ref, shi_ref,
                   sgug_ref, ws_ref, bs_ref,
                   ya_ref, q_ref, k_ref, v_ref, gz_ref, p_ref, gc_ref, *, tm):
    i = pl.program_id(1)
    x = x_ref[0]
    row = i * tm + lax.broadcasted_iota(jnp.int32, (tm, 1), 0)
    is_ctx = row < CTX_LEN
    mod = mod_ref[0]
    shift = jnp.where(is_ctx, mod[3:4], mod[0:1])
    scale = jnp.where(is_ctx, mod[4:5], mod[1:2])
    h = x * lax.rsqrt(jnp.mean(x * x, axis=-1, keepdims=True) + EPS) * g_ref[...]
    h = (h * (1.0 + scale) + shift).astype(BF16)

    ra = jnp.dot(h, w_ref[:, COL_A:COL_A + 3 * W_A], preferred_element_type=F32)
    u = jax.nn.gelu(ra[:, 0:W_A])
    v = jax.nn.gelu(ra[:, W_A:2 * W_A])
    gate_a = _silu(ra[:, 2 * W_A:3 * W_A])
    vn = v * lax.rsqrt(jnp.mean(v * v, axis=-1, keepdims=True) + EPS) * sgug_ref[...]
    vn = vn.astype(BF16)
    lane = lax.broadcasted_iota(jnp.int32, (GMLP_CHUNK, W_A), 1)
    for c in range(tm // GMLP_CHUNK):
        rows = slice(c * GMLP_CHUNK, (c + 1) * GMLP_CHUNK)
        r = jnp.dot(ws_ref[...], vn[rows], preferred_element_type=F32)
        mixed = r[0:GMLP_CHUNK]
        for hd in range(1, GMLP_HEADS):
            mixed = jnp.where(lane >= hd * GMLP_HEAD_DIM,
                              r[hd * GMLP_CHUNK:(hd + 1) * GMLP_CHUNK], mixed)
        mixed = mixed + bs_ref[...]
        ya_ref[0, rows, :] = (u[rows] * mixed * gate_a[rows]).astype(BF16)

    cos, slo, shi = cos_ref[...], slo_ref[...], shi_ref[...]
    rq = jnp.dot(h, w_ref[:, COL_Q:COL_Q + W_B], preferred_element_type=F32)
    rk = jnp.dot(h, w_ref[:, COL_K:COL_K + W_B], preferred_element_type=F32)
    q_scale = math.log2(math.e) / math.sqrt(DIFF_HEAD_DIM)
    for hd in range(DIFF_HEADS):
        cols = slice(hd * HEAD_W, (hd + 1) * HEAD_W)
        q_ref[0, hd] = (_rope(rq[:, cols], cos, slo, shi) * q_scale).astype(BF16)
        k_ref[0, hd] = _rope(rk[:, cols], cos, slo, shi).astype(BF16)
    rz = jnp.dot(h, w_ref[:, COL_V + W_B:COL_V + 2 * W_B], preferred_element_type=F32)
    gz_ref[0] = _silu(rz).astype(BF16)

    rc = jnp.dot(h, w_ref[:, COL_C:COL_C + 4 * W_C], preferred_element_type=F32)
    p_ref[0] = (rc[:, W_C:2 * W_C] * rc[:, 2 * W_C:3 * W_C]).astype(BF16)
    gc_ref[0] = (rc[:, 0:W_C] * _silu(rc[:, 3 * W_C:4 * W_C])).astype(BF16)

    rv = jnp.dot(h, w_ref[:, COL_V:COL_V + W_B], preferred_element_type=F32)
    for hd in range(DIFF_HEADS):
        v_ref[0, hd] = rv[:, hd * HEAD_W:(hd + 1) * HEAD_W].astype(BF16)


def _inproj(xs, mod_l, norm_g, w_in, cos, slo, shi, sgu_g, ws, bs):
    bsz, lt, _ = xs.shape
    tm = IN_TILE
    tok = lambda width: pl.BlockSpec((1, tm, width), lambda b, i: (b, i, 0))
    head = pl.BlockSpec((1, DIFF_HEADS, tm, HEAD_W), lambda b, i: (b, 0, i, 0))
    const = lambda shape: pl.BlockSpec(shape, lambda b, i: (0,) * len(shape))
    tab = pl.BlockSpec((tm, HEAD_W), lambda b, i: (i, 0))
    tok_shape = lambda width: jax.ShapeDtypeStruct((bsz, lt, width), BF16)
    head_shape = jax.ShapeDtypeStruct((bsz, DIFF_HEADS, lt, HEAD_W), BF16)
    return pl.pallas_call(
        functools.partial(_inproj_kernel, tm=tm),
        grid=(bsz, lt // tm),
        in_specs=[
            tok(D_MODEL),
            pl.BlockSpec((1, MOD_ROWS, D_MODEL), lambda b, i: (b, 0, 0)),
            const((1, D_MODEL)),
            const((D_MODEL, D_IN)),
            tab, tab, tab,
            const((1, W_A)),
            const((GMLP_HEADS * GMLP_CHUNK, GMLP_CHUNK)),
            const((GMLP_CHUNK, W_A)),
        ],
        out_specs=[tok(W_A), head, head, head, tok(W_B), tok(W_C), tok(W_C)],
        out_shape=[tok_shape(W_A), head_shape, head_shape, head_shape,
                   tok_shape(W_B), tok_shape(W_C), tok_shape(W_C)],
        compiler_params=pltpu.CompilerParams(
            dimension_semantics=("arbitrary", "arbitrary"),
            vmem_limit_bytes=VMEM_LIMIT_BYTES),
        name="in_projection",
    )(xs, mod_l, norm_g, w_in, cos, slo, shi, sgu_g, ws, bs)


NT_DIMS = (((1,), (1,)), ((), ()))
GZ_ARG = 7


def _widen_values(v_sc, v_ref):
    v_sc[:, 0:HEAD_W] = v_ref[0, 0]
    v_sc[:, HEAD_W:2 * HEAD_W] = jnp.ones((v_sc.shape[0], HEAD_W), BF16)


def _stack_q(qz_sc, q_ref, tq):
    q = q_ref[0, 0]
    lane = lax.broadcasted_iota(jnp.int32, (tq, HEAD_W), 1)
    zero = jnp.zeros_like(q)
    qz_sc[0:tq] = jnp.where(lane < DIFF_HEAD_DIM, q, zero)
    qz_sc[tq:2 * tq] = jnp.where(lane >= DIFF_HEAD_DIM, q, zero)
    return qz_sc[...]


def _finish_head(acc, lam_refs, sg_ref, gz_ref, o_ref, tq, lam_init):
    lq1_ref, lk1_ref, lq2_ref, lk2_ref = lam_refs
    o = acc[:, 0:HEAD_W] / acc[:, HEAD_W:2 * HEAD_W]
    lam = (jnp.exp(jnp.sum(lq1_ref[...] * lk1_ref[...]))
           - jnp.exp(jnp.sum(lq2_ref[...] * lk2_ref[...])) + lam_init)
    o = o[0:tq] - lam * o[tq:2 * tq]
    o = o * lax.rsqrt(jnp.mean(o * o, axis=-1, keepdims=True) + EPS) * sg_ref[...]
    o_ref[0] = (o * (1.0 - lam_init) * gz_ref[0].astype(F32)).astype(BF16)


def _attn_pipe_kernel(lq1_ref, lk1_ref, lq2_ref, lk2_ref, q_ref, k_ref, v_ref, gz_ref, sg_ref,
                      o_ref, qz_sc, v_sc, s0_sc, s1_sc, m0_sc, m1_sc,
                      *, tq, tk, kt, n_q, lam_init):
    g = pl.program_id(0)
    n_keys = s0_sc.shape[1]

    @pl.when(g == 0)
    def _():
        s1_sc[...] = jnp.zeros_like(s1_sc)
        m1_sc[...] = jnp.zeros_like(m1_sc)

    @pl.when((g == 0) | ((g - 1) % n_q == 0))
    def _():
        _widen_values(v_sc, v_ref)

    def step(s_w, m_w, s_r, m_r):
        qz = _stack_q(qz_sc, q_ref, tq)
        m = None
        for c in range(n_keys // tk):
            s = lax.dot_general(qz, k_ref[0, 0, c * tk:(c + 1) * tk, :], NT_DIMS,
                                preferred_element_type=F32)
            s_w[:, c * tk:(c + 1) * tk] = s
            for t in range(tk // 128):
                blk = s[:, t * 128:(t + 1) * 128]
                m = blk if m is None else jnp.maximum(m, blk)
        m_w[...] = jnp.broadcast_to(jnp.max(m, axis=-1, keepdims=True), m_w.shape)

        m_prev = jnp.concatenate([m_r[...]] * (kt // 128), axis=1)
        acc = None
        for c in range(n_keys // kt):
            p = jnp.exp2(s_r[:, c * kt:(c + 1) * kt] - m_prev).astype(BF16)
            d = jnp.dot(p, v_sc[c * kt:(c + 1) * kt, :], preferred_element_type=F32)
            acc = d if acc is None else acc + d
        _finish_head(acc, (lq1_ref, lk1_ref, lq2_ref, lk2_ref), sg_ref, gz_ref, o_ref,
                     tq, lam_init)

    @pl.when(g % 2 == 0)
    def _():
        step(s0_sc, m0_sc, s1_sc, m1_sc)

    @pl.when(g % 2 == 1)
    def _():
        step(s1_sc, m1_sc, s0_sc, m0_sc)


def _attention_pipelined(q, k, v, gz, lams, subln_g, lam_init, *, q_row0, tk, kt):
    bsz, _, lt, _ = q.shape
    tq = Q_TILE
    q_blk0 = q_row0 // tq
    n_q = (lt - q_row0) // tq
    n_tiles = bsz * DIFF_HEADS * n_q

    def score_tile(g):
        t = jnp.minimum(g, n_tiles - 1)
        return t // (DIFF_HEADS * n_q), (t // n_q) % DIFF_HEADS, t % n_q + q_blk0

    def value_tile(g):
        return score_tile(jnp.maximum(g - 1, 0))

    def by_head(tile, rows):
        def index_map(g):
            b, h, i = tile(g)
            return (b, h, i if rows else 0, 0)
        return index_map

    def by_token(g):
        b, h, i = value_tile(g)
        return (b, i, h)

    lam_spec = pl.BlockSpec((1, DIFF_HEAD_DIM), lambda g: (0, 0))
    seq_blk = (1, 1, lt, HEAD_W)
    return pl.pallas_call(
        functools.partial(_attn_pipe_kernel, tq=tq, tk=tk, kt=kt, n_q=n_q, lam_init=lam_init),
        grid=(n_tiles + 1,),
        in_specs=[
            lam_spec, lam_spec, lam_spec, lam_spec,
            pl.BlockSpec((1, 1, tq, HEAD_W), by_head(score_tile, True)),
            pl.BlockSpec(seq_blk, by_head(score_tile, False)),
            pl.BlockSpec(seq_blk, by_head(value_tile, False)),
            pl.BlockSpec((1, tq, HEAD_W), by_token),
            pl.BlockSpec((1, HEAD_W), lambda g: (0, 0)),
        ],
        out_specs=pl.BlockSpec((1, tq, HEAD_W), by_token),
        out_shape=jax.ShapeDtypeStruct((bsz, lt, W_B), BF16),
        input_output_aliases={GZ_ARG: 0},
        scratch_shapes=[
            pltpu.VMEM((2 * tq, HEAD_W), BF16),
            pltpu.VMEM((lt, 2 * HEAD_W), BF16),
            pltpu.VMEM((2 * tq, lt), F32),
            pltpu.VMEM((2 * tq, lt), F32),
            pltpu.VMEM((2 * tq, 128), F32),
            pltpu.VMEM((2 * tq, 128), F32),
        ],
        compiler_params=pltpu.CompilerParams(
            dimension_semantics=("arbitrary",),
            vmem_limit_bytes=VMEM_LIMIT_BYTES),
        name="diff_attention_latent",
    )(*lams, q, k, v, gz, subln_g)


def _attn_ctx_kernel(lq1_ref, lk1_ref, lq2_ref, lk2_ref, q_ref, k_ref, v_ref, gz_ref, sg_ref,
                     o_ref, qz_sc, v_sc, *, tq, lam_init):
    _widen_values(v_sc, v_ref)
    qz = _stack_q(qz_sc, q_ref, tq)
    s = lax.dot_general(qz, k_ref[0, 0], NT_DIMS, preferred_element_type=F32)
    p = jnp.exp2(s - jnp.max(s, axis=-1, keepdims=True)).astype(BF16)
    acc = jnp.dot(p, v_sc[...], preferred_element_type=F32)
    _finish_head(acc, (lq1_ref, lk1_ref, lq2_ref, lk2_ref), sg_ref, gz_ref, o_ref, tq, lam_init)


def _attention_ctx(q, k, v, gz, lams, subln_g, lam_init):
    bsz, _, lt, _ = q.shape
    tq = CTX_LEN
    lam_spec = pl.BlockSpec((1, DIFF_HEAD_DIM), lambda b, h: (0, 0))
    head_blk = pl.BlockSpec((1, 1, tq, HEAD_W), lambda b, h: (b, h, 0, 0))
    tok_blk = pl.BlockSpec((1, tq, HEAD_W), lambda b, h: (b, 0, h))
    return pl.pallas_call(
        functools.partial(_attn_ctx_kernel, tq=tq, lam_init=lam_init),
        grid=(bsz, DIFF_HEADS),
        in_specs=[lam_spec, lam_spec, lam_spec, lam_spec, head_blk, head_blk, head_blk, tok_blk,
                  pl.BlockSpec((1, HEAD_W), lambda b, h: (0, 0))],
        out_specs=tok_blk,
        out_shape=jax.ShapeDtypeStruct((bsz, lt, W_B), BF16),
        input_output_aliases={GZ_ARG: 0},
        scratch_shapes=[
            pltpu.VMEM((2 * tq, HEAD_W), BF16),
            pltpu.VMEM((tq, 2 * HEAD_W), BF16),
        ],
        compiler_params=pltpu.CompilerParams(
            dimension_semantics=("arbitrary", "arbitrary"),
            vmem_limit_bytes=VMEM_LIMIT_BYTES),
        name="diff_attention_ctx",
    )(*lams, q, k, v, gz, subln_g)


def _outproj_kernel(x_ref, mod_ref, ya_ref, yb_ref, p_ref, pprev_ref, pnext_ref, gc_ref,
                    cw_ref, cb_ref, w_ref, fg_ref, o_ref, *, tm, row_off, lt, final):
    i = pl.program_id(1)
    row = row_off + i * tm + lax.broadcasted_iota(jnp.int32, (tm, 1), 0)
    local = lax.broadcasted_iota(jnp.int32, (tm, 1), 0)
    p = p_ref[0].astype(F32)
    prev_row = pprev_ref[0, HALO - 1:HALO, :].astype(F32)
    next_row = pnext_ref[0, 0:1, :].astype(F32)
    up = jnp.where(local == 0, prev_row, pltpu.roll(p, 1, 0))
    dn = jnp.where(local == tm - 1, next_row, pltpu.roll(p, tm - 1, 0))
    up = jnp.where((row == 0) | (row == CTX_LEN), 0.0, up)
    dn = jnp.where((row == CTX_LEN - 1) | (row == lt - 1), 0.0, dn)
    cw = cw_ref[...]
    conv = up * cw[0:1] + p * cw[1:2] + dn * cw[2:3] + cb_ref[...]
    yc = (gc_ref[0].astype(F32) * conv).astype(BF16)

    y = jnp.dot(ya_ref[0], w_ref[0:W_A], preferred_element_type=F32)
    y = y + jnp.dot(yb_ref[0], w_ref[W_A:W_A + W_B], preferred_element_type=F32)
    y = y + jnp.dot(yc, w_ref[W_A + W_B:D_MIX], preferred_element_type=F32)

    mod = mod_ref[0]
    gate = jnp.where(row < CTX_LEN, mod[5:6], mod[2:3])
    x = x_ref[0] + gate * y
    if final:
        x = x * lax.rsqrt(jnp.mean(x * x, axis=-1, keepdims=True) + EPS) * fg_ref[...]
    o_ref[0] = x


def _outproj(xs, mod_l, ya, yb, p, gc, conv_w, conv_b, w_out, final_g, final):
    bsz, lt, _ = xs.shape
    tm = FINAL_OUT_TILE if final else OUT_TILE
    blk_off = CTX_LEN // tm if final else 0
    n_t = lt // tm - blk_off
    hpt = tm // HALO
    n_halo = lt // HALO
    tok = lambda width: pl.BlockSpec((1, tm, width), lambda b, i: (b, i + blk_off, 0))
    const = lambda shape: pl.BlockSpec(shape, lambda b, i: (0,) * len(shape))
    prev = pl.BlockSpec((1, HALO, W_C),
                        lambda b, i: (b, jnp.maximum((i + blk_off) * hpt - 1, 0), 0))
    nxt = pl.BlockSpec((1, HALO, W_C),
                       lambda b, i: (b, jnp.minimum((i + blk_off + 1) * hpt, n_halo - 1), 0))
    out_rows = n_t * tm
    return pl.pallas_call(
        functools.partial(_outproj_kernel, tm=tm, row_off=blk_off * tm, lt=lt, final=final),
        grid=(bsz, n_t),
        in_specs=[
            tok(D_MODEL),
            pl.BlockSpec((1, MOD_ROWS, D_MODEL), lambda b, i: (b, 0, 0)),
            tok(W_A), tok(W_B), tok(W_C), prev, nxt, tok(W_C),
            const((3, W_C)), const((1, W_C)), const((D_MIX, D_MODEL)), const((1, D_MODEL)),
        ],
        out_specs=pl.BlockSpec((1, tm, D_MODEL), lambda b, i: (b, i, 0)),
        out_shape=jax.ShapeDtypeStruct((bsz, out_rows, D_MODEL), F32),
        compiler_params=pltpu.CompilerParams(
            dimension_semantics=("arbitrary", "arbitrary"),
            vmem_limit_bytes=VMEM_LIMIT_BYTES),
        name="out_projection",
    )(xs, mod_l, ya, yb, p, p, p, gc, conv_w, conv_b, w_out, final_g)


def _rope_tables(length):
    rows = length // GRID_W
    t = jnp.arange(rows * GRID_W)
    pos = jnp.stack([t // GRID_W, t % GRID_W], axis=1).astype(F32)
    inv = ROPE_BASE ** (-jnp.arange(ROPE_PAIRS, dtype=F32) / ROPE_PAIRS)
    ang = pos[:, :, None] * inv
    lane = jnp.arange(HEAD_W)
    axis = (lane % DIFF_HEAD_DIM) // (2 * ROPE_PAIRS)
    pair = lane % ROPE_PAIRS
    second = (lane % (2 * ROPE_PAIRS)) >= ROPE_PAIRS
    cos = jnp.cos(ang)[:, axis, pair]
    sin = jnp.sin(ang)[:, axis, pair]
    sin_lo = jnp.where(second, 0.0, -sin)
    sin_hi = jnp.where(second, sin, 0.0)
    ident = jnp.ones((CTX_LEN, HEAD_W), F32)
    zeros = jnp.zeros((CTX_LEN, HEAD_W), F32)
    return (jnp.concatenate([ident, cos]), jnp.concatenate([zeros, sin_lo]),
            jnp.concatenate([zeros, sin_hi]))


def kernel(x, c, ctx, c_ctx, w_mod, b_mod, norm_g, w_in, w_out, sgu_norm_g, sgu_w, sgu_b,
           lambda_q1, lambda_k1, lambda_q2, lambda_k2, subln_g, conv_w, conv_b, final_g):
    bsz, length, _ = x.shape
    assert ctx.shape[1] == CTX_LEN and bsz + 1 <= MOD_ROWS
    lt = CTX_LEN + length
    assert lt % IN_TILE == 0 and lt % OUT_TILE == 0
    assert lt % FINAL_OUT_TILE == 0 and CTX_LEN % FINAL_OUT_TILE == 0
    assert CTX_LEN % Q_TILE == 0 and lt % K_TILE == 0

    cond = jnp.zeros((MOD_ROWS, D_MODEL), F32).at[:bsz].set(c).at[bsz].set(c_ctx)
    mod = _modulation(cond, w_mod, b_mod)
    mod_b = jnp.transpose(mod[:, :, :bsz], (0, 2, 1, 3))
    mod_c = jnp.broadcast_to(mod[:, None, :, bsz], (DEPTH, bsz, 3, D_MODEL))
    pad = jnp.zeros((DEPTH, bsz, MOD_ROWS - 6, D_MODEL), F32)
    mod_rows = jnp.concatenate([mod_b, mod_c, pad], axis=2)

    cos, slo, shi = _rope_tables(length)
    w_in_b = w_in.astype(BF16)
    w_out_b = w_out.astype(BF16)
    ws = sgu_w.reshape(DEPTH, GMLP_HEADS * GMLP_CHUNK, GMLP_CHUNK).astype(BF16)
    bs = jnp.repeat(jnp.transpose(sgu_b, (0, 2, 1)), GMLP_HEAD_DIM, axis=2)

    xs = jnp.concatenate([ctx, x], axis=1)
    for l in range(DEPTH):
        final = l == DEPTH - 1
        lam_init = 0.8 - 0.6 * math.exp(-0.3 * l)
        ya, q, k, v, gz, p, gc = _inproj(
            xs, mod_rows[l], norm_g[l][None], w_in_b[l], cos, slo, shi,
            sgu_norm_g[l][None], ws[l], bs[l])
        lams = (lambda_q1[l][None], lambda_k1[l][None], lambda_q2[l][None], lambda_k2[l][None])
        yb = _attention_ctx(q, k, v, gz, lams, subln_g[l][None], lam_init)
        yb = _attention_pipelined(q, k, v, yb, lams, subln_g[l][None], lam_init,
                                  q_row0=CTX_LEN, tk=K_TILE, kt=V_TILE)
        xs = _outproj(xs, mod_rows[l], ya, yb, p, gc, conv_w[l], conv_b[l][None],
                      w_out_b[l], final_g[None], final)
    return xs
```

```python
import functools
import math

import jax
import jax.numpy as jnp
from jax import lax
from jax.experimental import pallas as pl
from jax.experimental.pallas import tpu as pltpu

D_MODEL = 1024
DEPTH = 4
CTX_LEN = 256
GRID_W = 64

GMLP_HEADS = 4
GMLP_HEAD_DIM = 64
GMLP_CHUNK = 128
W_A = GMLP_HEADS * GMLP_HEAD_DIM
DIFF_HEADS = 4
DIFF_HEAD_DIM = 64
HEAD_W = 2 * DIFF_HEAD_DIM
W_B = DIFF_HEADS * HEAD_W
W_C = 256
D_MIX = W_A + W_B + W_C
D_IN = 3 * W_A + 4 * W_B + 4 * W_C

COL_A = 0
COL_Q = 3 * W_A
COL_K = COL_Q + W_B
COL_V = COL_K + W_B
COL_C = COL_V + 2 * W_B

ROPE_BASE = 10000.0
ROPE_PAIRS = DIFF_HEAD_DIM // 4
EPS = 1e-6

F32 = jnp.float32
BF16 = jnp.bfloat16

VMEM_LIMIT_BYTES = 56 * 1024 * 1024
MOD_ROWS = 8

IN_TILE = 384
OUT_TILE = 768
FINAL_OUT_TILE = 256
Q_TILE = 256
K_TILE = 2816
V_TILE = 256
HALO = 16


def _silu(x):
    return x * jax.nn.sigmoid(x)


def _mod_kernel(cond_ref, w_ref, b_ref, o_ref):
    a = _silu(cond_ref[...])
    o_ref[0, 0] = jnp.dot(a, w_ref[0], preferred_element_type=F32) + b_ref[0, 0]


def _modulation(cond, w_mod, b_mod):
    return pl.pallas_call(
        _mod_kernel,
        grid=(DEPTH, 3),
        in_specs=[
            pl.BlockSpec((MOD_ROWS, D_MODEL), lambda l, j: (0, 0)),
            pl.BlockSpec((1, D_MODEL, D_MODEL), lambda l, j: (l, 0, j)),
            pl.BlockSpec((1, 1, 1, D_MODEL), lambda l, j: (l, j, 0, 0)),
        ],
        out_specs=pl.BlockSpec((1, 1, MOD_ROWS, D_MODEL), lambda l, j: (l, j, 0, 0)),
        out_shape=jax.ShapeDtypeStruct((DEPTH, 3, MOD_ROWS, D_MODEL), F32),
        compiler_params=pltpu.CompilerParams(
            dimension_semantics=("arbitrary", "arbitrary"),
            vmem_limit_bytes=VMEM_LIMIT_BYTES),
        name="adaln_modulation",
    )(cond, w_mod, b_mod.reshape(DEPTH, 3, 1, D_MODEL))


def _rope(t, cos, sin_lo, sin_hi):
    return (t * cos + pltpu.roll(t, HEAD_W - ROPE_PAIRS, 1) * sin_lo
            + pltpu.roll(t, ROPE_PAIRS, 1) * sin_hi)


def _inproj_kernel(x_ref, mod_ref, g_ref, w_ref, cos_ref, slo_ref, shi_ref,
                   sgug_ref, ws_ref, bs_ref, half_ref,
                   ya_ref, q_ref, k_ref, v_ref, gz_ref, p_ref, gc_ref, qn_ref, kn_ref, *, tm):
    i = pl.program_id(1)
    x = x_ref[0]
    row = i * tm + lax.broadcasted_iota(jnp.int32, (tm, 1), 0)
    is_ctx = row < CTX_LEN
    mod = mod_ref[0]
    shift = jnp.where(is_ctx, mod[3:4], mod[0:1])
    scale = jnp.where(is_ctx, mod[4:5], mod[1:2])
    h = x * lax.rsqrt(jnp.mean(x * x, axis=-1, keepdims=True) + EPS) * g_ref[...]
    h = (h * (1.0 + scale) + shift).astype(BF16)

    ra = jnp.dot(h, w_ref[:, COL_A:COL_A + 3 * W_A], preferred_element_type=F32)
    u = jax.nn.gelu(ra[:, 0:W_A])
    v = jax.nn.gelu(ra[:, W_A:2 * W_A])
    gate_a = _silu(ra[:, 2 * W_A:3 * W_A])
    vn = v * lax.rsqrt(jnp.mean(v * v, axis=-1, keepdims=True) + EPS) * sgug_ref[...]
    vn = vn.astype(BF16)
    lane = lax.broadcasted_iota(jnp.int32, (GMLP_CHUNK, W_A), 1)
    for c in range(tm // GMLP_CHUNK):
        rows = slice(c * GMLP_CHUNK, (c + 1) * GMLP_CHUNK)
        r = jnp.dot(ws_ref[...], vn[rows], preferred_element_type=F32)
        mixed = r[0:GMLP_CHUNK]
        for hd in range(1, GMLP_HEADS):
            mixed = jnp.where(lane >= hd * GMLP_HEAD_DIM,
                              r[hd * GMLP_CHUNK:(hd + 1) * GMLP_CHUNK], mixed)
        mixed = mixed + bs_ref[...]
        ya_ref[0, rows, :] = (u[rows] * mixed * gate_a[rows]).astype(BF16)

    cos, slo, shi = cos_ref[...], slo_ref[...], shi_ref[...]
    rq = jnp.dot(h, w_ref[:, COL_Q:COL_Q + W_B], preferred_element_type=F32)
    rk = jnp.dot(h, w_ref[:, COL_K:COL_K + W_B], preferred_element_type=F32)
    q_scale = math.log2(math.e) / math.sqrt(DIFF_HEAD_DIM)
    q_heads, k_heads = [], []
    for hd in range(DIFF_HEADS):
        cols = slice(hd * HEAD_W, (hd + 1) * HEAD_W)
        q_heads.append(_rope(rq[:, cols], cos, slo, shi) * q_scale)
        k_heads.append(_rope(rk[:, cols], cos, slo, shi))
        q_ref[0, hd] = q_heads[hd].astype(BF16)
        k_ref[0, hd] = k_heads[hd].astype(BF16)
    for heads, n_ref in ((q_heads, qn_ref), (k_heads, kn_ref)):
        t = jnp.concatenate(heads, axis=1)
        n = jnp.dot((t * t).astype(BF16), half_ref[...], preferred_element_type=F32)
        best = n[0:MOD_ROWS]
        for r in range(1, tm // MOD_ROWS):
            best = jnp.maximum(best, n[r * MOD_ROWS:(r + 1) * MOD_ROWS])
        n_ref[0, 0] = best
    rz = jnp.dot(h, w_ref[:, COL_V + W_B:COL_V + 2 * W_B], preferred_element_type=F32)
    gz_ref[0] = _silu(rz).astype(BF16)

    rc = jnp.dot(h, w_ref[:, COL_C:COL_C + 4 * W_C], preferred_element_type=F32)
    p_ref[0] = (rc[:, W_C:2 * W_C] * rc[:, 2 * W_C:3 * W_C]).astype(BF16)
    gc_ref[0] = (rc[:, 0:W_C] * _silu(rc[:, 3 * W_C:4 * W_C])).astype(BF16)

    rv = jnp.dot(h, w_ref[:, COL_V:COL_V + W_B], preferred_element_type=F32)
    for hd in range(DIFF_HEADS):
        v_ref[0, hd] = rv[:, hd * HEAD_W:(hd + 1) * HEAD_W].astype(BF16)


def _inproj(xs, mod_l, norm_g, w_in, cos, slo, shi, sgu_g, ws, bs, half_ind):
    bsz, lt, _ = xs.shape
    tm = IN_TILE
    n_t = lt // tm
    norm_blk = pl.BlockSpec((1, 1, MOD_ROWS, 128), lambda b, i: (b, i, 0, 0))
    norm_shape = jax.ShapeDtypeStruct((bsz, n_t, MOD_ROWS, 128), F32)
    tok = lambda width: pl.BlockSpec((1, tm, width), lambda b, i: (b, i, 0))
    head = pl.BlockSpec((1, DIFF_HEADS, tm, HEAD_W), lambda b, i: (b, 0, i, 0))
    const = lambda shape: pl.BlockSpec(shape, lambda b, i: (0,) * len(shape))
    tab = pl.BlockSpec((tm, HEAD_W), lambda b, i: (i, 0))
    tok_shape = lambda width: jax.ShapeDtypeStruct((bsz, lt, width), BF16)
    head_shape = jax.ShapeDtypeStruct((bsz, DIFF_HEADS, lt, HEAD_W), BF16)
    return pl.pallas_call(
        functools.partial(_inproj_kernel, tm=tm),
        grid=(bsz, lt // tm),
        in_specs=[
            tok(D_MODEL),
            pl.BlockSpec((1, MOD_ROWS, D_MODEL), lambda b, i: (b, 0, 0)),
            const((1, D_MODEL)),
            const((D_MODEL, D_IN)),
            tab, tab, tab,
            const((1, W_A)),
            const((GMLP_HEADS * GMLP_CHUNK, GMLP_CHUNK)),
            const((GMLP_CHUNK, W_A)),
            const((W_B, 128)),
        ],
        out_specs=[tok(W_A), head, head, head, tok(W_B), tok(W_C), tok(W_C), norm_blk, norm_blk],
        out_shape=[tok_shape(W_A), head_shape, head_shape, head_shape,
                   tok_shape(W_B), tok_shape(W_C), tok_shape(W_C), norm_shape, norm_shape],
        compiler_params=pltpu.CompilerParams(
            dimension_semantics=("arbitrary", "arbitrary"),
            vmem_limit_bytes=VMEM_LIMIT_BYTES),
        name="in_projection",
    )(xs, mod_l, norm_g, w_in, cos, slo, shi, sgu_g, ws, bs, half_ind)


NT_DIMS = (((1,), (1,)), ((), ()))
NORM_SLACK = 1.02
MAX_SCORE_BOUND = 50.0
GZ_ARG = 7


def _widen_values(v_sc, v_ref):
    v_sc[:, 0:HEAD_W] = v_ref[0, 0]
    v_sc[:, HEAD_W:2 * HEAD_W] = jnp.ones((v_sc.shape[0], HEAD_W), BF16)


def _stack_q(qz_sc, q_ref, tq):
    q = q_ref[0, 0]
    lane = lax.broadcasted_iota(jnp.int32, (tq, HEAD_W), 1)
    zero = jnp.zeros_like(q)
    qz_sc[0:tq] = jnp.where(lane < DIFF_HEAD_DIM, q, zero)
    qz_sc[tq:2 * tq] = jnp.where(lane >= DIFF_HEAD_DIM, q, zero)
    return qz_sc[...]


def _finish_head(acc, lam_refs, sg_ref, gz_ref, o_ref, tq, lam_init):
    lq1_ref, lk1_ref, lq2_ref, lk2_ref = lam_refs
    o = acc[:, 0:HEAD_W] / acc[:, HEAD_W:2 * HEAD_W]
    lam = (jnp.exp(jnp.sum(lq1_ref[...] * lk1_ref[...]))
           - jnp.exp(jnp.sum(lq2_ref[...] * lk2_ref[...])) + lam_init)
    o = o[0:tq] - lam * o[tq:2 * tq]
    o = o * lax.rsqrt(jnp.mean(o * o, axis=-1, keepdims=True) + EPS) * sg_ref[...]
    o_ref[0] = (o * (1.0 - lam_init) * gz_ref[0].astype(F32)).astype(BF16)


def _pipeline_maps(bsz, n_q, q_blk0):
    n_tiles = bsz * DIFF_HEADS * n_q

    def score_tile(g):
        t = jnp.minimum(g, n_tiles - 1)
        return t // (DIFF_HEADS * n_q), (t // n_q) % DIFF_HEADS, t % n_q + q_blk0

    def value_tile(g):
        return score_tile(jnp.maximum(g - 1, 0))

    def score_head(g):
        b, h, _ = score_tile(g)
        return (b, h, 0, 0)

    def score_rows(g):
        b, h, i = score_tile(g)
        return (b, h, i, 0)

    def value_head(g):
        b, h, _ = value_tile(g)
        return (b, h, 0, 0)

    def value_tok(g):
        b, h, i = value_tile(g)
        return (b, i, h)

    return n_tiles, score_head, score_rows, value_head, value_tok


def _attn_bounded_kernel(lq1_ref, lk1_ref, lq2_ref, lk2_ref, q_ref, k_ref, v_ref, gz_ref, sg_ref,
                         kb_ref, o_ref, qz_sc, e0_sc, e1_sc, l0_sc, l1_sc,
                         *, tq, tk, kt, lam_init):
    g = pl.program_id(0)
    n_keys = e0_sc.shape[1]

    @pl.when(g == 0)
    def _():
        e1_sc[...] = jnp.zeros_like(e1_sc)
        l1_sc[...] = jnp.ones_like(l1_sc)

    def step(e_w, l_w, e_r, l_r):
        qz = _stack_q(qz_sc, q_ref, tq)
        qf = qz.astype(F32)
        q_norm = jnp.sqrt(jnp.sum(qf * qf, axis=-1, keepdims=True))
        row = lax.broadcasted_iota(jnp.int32, (2 * tq, 1), 0)
        kb = kb_ref[0, 0]
        bound = q_norm * jnp.where(row < tq, kb[0:1], kb[1:2])
        l = None
        for c in range(n_keys // tk):
            s = lax.dot_general(qz, k_ref[0, 0, c * tk:(c + 1) * tk, :], NT_DIMS,
                                preferred_element_type=F32)
            for t in range(tk // 128):
                e = jnp.exp2(s[:, t * 128:(t + 1) * 128] - bound)
                l = e if l is None else l + e
                e_w[:, c * tk + t * 128:c * tk + (t + 1) * 128] = e.astype(BF16)
        l_w[...] = jnp.broadcast_to(jnp.sum(l, axis=-1, keepdims=True), l_w.shape)

        l_prev = l_r[...]
        lam = (jnp.exp(jnp.sum(lq1_ref[...] * lk1_ref[...]))
               - jnp.exp(jnp.sum(lq2_ref[...] * lk2_ref[...])) + lam_init)
        ratio = (lam * l_prev[0:tq] / l_prev[tq:2 * tq]).astype(BF16)
        ratio = jnp.concatenate([ratio] * (kt // 128), axis=1)
        acc = None
        for c in range(n_keys // kt):
            a = e_r[0:tq, c * kt:(c + 1) * kt] - ratio * e_r[tq:2 * tq, c * kt:(c + 1) * kt]
            vv = v_ref[0, 0, c * kt:(c + 1) * kt, :]
            d = jnp.dot(a, jnp.concatenate([vv, vv], axis=1), preferred_element_type=F32)
            acc = d if acc is None else acc + d
        o = acc[:, 0:HEAD_W] / l_prev[0:tq]
        o = o * lax.rsqrt(jnp.mean(o * o, axis=-1, keepdims=True) + EPS) * sg_ref[...]
        o_ref[0] = (o * (1.0 - lam_init) * gz_ref[0].astype(F32)).astype(BF16)

    @pl.when(g % 2 == 0)
    def _():
        step(e0_sc, l0_sc, e1_sc, l1_sc)

    @pl.when(g % 2 == 1)
    def _():
        step(e1_sc, l1_sc, e0_sc, l0_sc)


def _attention_bounded(q, k, v, gz, lams, subln_g, k_bound, lam_init, *, q_row0, tk, kt):
    bsz, _, lt, _ = q.shape
    tq = Q_TILE
    n_q = (lt - q_row0) // tq
    n_tiles, score_head, score_rows, value_head, value_tok = _pipeline_maps(bsz, n_q, q_row0 // tq)
    lam_spec = pl.BlockSpec((1, DIFF_HEAD_DIM), lambda g: (0, 0))
    seq_blk = (1, 1, lt, HEAD_W)
    return pl.pallas_call(
        functools.partial(_attn_bounded_kernel, tq=tq, tk=tk, kt=kt, lam_init=lam_init),
        grid=(n_tiles + 1,),
        in_specs=[
            lam_spec, lam_spec, lam_spec, lam_spec,
            pl.BlockSpec((1, 1, tq, HEAD_W), score_rows),
            pl.BlockSpec(seq_blk, score_head),
            pl.BlockSpec(seq_blk, value_head),
            pl.BlockSpec((1, tq, HEAD_W), value_tok),
            pl.BlockSpec((1, HEAD_W), lambda g: (0, 0)),
            pl.BlockSpec((1, 1, MOD_ROWS, 128), score_head),
        ],
        out_specs=pl.BlockSpec((1, tq, HEAD_W), value_tok),
        out_shape=jax.ShapeDtypeStruct((bsz, lt, W_B), BF16),
        input_output_aliases={GZ_ARG: 0},
        scratch_shapes=[
            pltpu.VMEM((2 * tq, HEAD_W), BF16),
            pltpu.VMEM((2 * tq, lt), BF16),
            pltpu.VMEM((2 * tq, lt), BF16),
            pltpu.VMEM((2 * tq, 128), F32),
            pltpu.VMEM((2 * tq, 128), F32),
        ],
        compiler_params=pltpu.CompilerParams(
            dimension_semantics=("arbitrary",),
            vmem_limit_bytes=VMEM_LIMIT_BYTES),
        name="diff_attention_bounded",
    )(*lams, q, k, v, gz, subln_g, k_bound)


def _attn_pipe_kernel(lq1_ref, lk1_ref, lq2_ref, lk2_ref, q_ref, k_ref, v_ref, gz_ref, sg_ref,
                      o_ref, qz_sc, v_sc, s0_sc, s1_sc, m0_sc, m1_sc,
                      *, tq, tk, kt, n_q, lam_init):
    g = pl.program_id(0)
    n_keys = s0_sc.shape[1]

    @pl.when(g == 0)
    def _():
        s1_sc[...] = jnp.zeros_like(s1_sc)
        m1_sc[...] = jnp.zeros_like(m1_sc)

    @pl.when((g == 0) | ((g - 1) % n_q == 0))
    def _():
        _widen_values(v_sc, v_ref)

    def step(s_w, m_w, s_r, m_r):
        qz = _stack_q(qz_sc, q_ref, tq)
        m = None
        for c in range(n_keys // tk):
            s = lax.dot_general(qz, k_ref[0, 0, c * tk:(c + 1) * tk, :], NT_DIMS,
                                preferred_element_type=F32)
            s_w[:, c * tk:(c + 1) * tk] = s
            for t in range(tk // 128):
                blk = s[:, t * 128:(t + 1) * 128]
                m = blk if m is None else jnp.maximum(m, blk)
        m_w[...] = jnp.broadcast_to(jnp.max(m, axis=-1, keepdims=True), m_w.shape)

        m_prev = jnp.concatenate([m_r[...]] * (kt // 128), axis=1)
        acc = None
        for c in range(n_keys // kt):
            p = jnp.exp2(s_r[:, c * kt:(c + 1) * kt] - m_prev).astype(BF16)
            d = jnp.dot(p, v_sc[c * kt:(c + 1) * kt, :], preferred_element_type=F32)
            acc = d if acc is None else acc + d
        _finish_head(acc, (lq1_ref, lk1_ref, lq2_ref, lk2_ref), sg_ref, gz_ref, o_ref,
                     tq, lam_init)

    @pl.when(g % 2 == 0)
    def _():
        step(s0_sc, m0_sc, s1_sc, m1_sc)

    @pl.when(g % 2 == 1)
    def _():
        step(s1_sc, m1_sc, s0_sc, m0_sc)


def _attention_pipelined(q, k, v, gz, lams, subln_g, lam_init, *, q_row0, tk, kt):
    bsz, _, lt, _ = q.shape
    tq = Q_TILE
    n_q = (lt - q_row0) // tq
    n_tiles, score_head, score_rows, value_head, value_tok = _pipeline_maps(bsz, n_q, q_row0 // tq)
    lam_spec = pl.BlockSpec((1, DIFF_HEAD_DIM), lambda g: (0, 0))
    seq_blk = (1, 1, lt, HEAD_W)
    return pl.pallas_call(
        functools.partial(_attn_pipe_kernel, tq=tq, tk=tk, kt=kt, n_q=n_q, lam_init=lam_init),
        grid=(n_tiles + 1,),
        in_specs=[
            lam_spec, lam_spec, lam_spec, lam_spec,
            pl.BlockSpec((1, 1, tq, HEAD_W), score_rows),
            pl.BlockSpec(seq_blk, score_head),
            pl.BlockSpec(seq_blk, value_head),
            pl.BlockSpec((1, tq, HEAD_W), value_tok),
            pl.BlockSpec((1, HEAD_W), lambda g: (0, 0)),
        ],
        out_specs=pl.BlockSpec((1, tq, HEAD_W), value_tok),
        out_shape=jax.ShapeDtypeStruct((bsz, lt, W_B), BF16),
        input_output_aliases={GZ_ARG: 0},
        scratch_shapes=[
            pltpu.VMEM((2 * tq, HEAD_W), BF16),
            pltpu.VMEM((lt, 2 * HEAD_W), BF16),
            pltpu.VMEM((2 * tq, lt), F32),
            pltpu.VMEM((2 * tq, lt), F32),
            pltpu.VMEM((2 * tq, 128), F32),
            pltpu.VMEM((2 * tq, 128), F32),
        ],
        compiler_params=pltpu.CompilerParams(
            dimension_semantics=("arbitrary",),
            vmem_limit_bytes=VMEM_LIMIT_BYTES),
        name="diff_attention_latent",
    )(*lams, q, k, v, gz, subln_g)


def _attn_ctx_kernel(lq1_ref, lk1_ref, lq2_ref, lk2_ref, q_ref, k_ref, v_ref, gz_ref, sg_ref,
                     o_ref, qz_sc, v_sc, *, tq, lam_init):
    _widen_values(v_sc, v_ref)
    qz = _stack_q(qz_sc, q_ref, tq)
    s = lax.dot_general(qz, k_ref[0, 0], NT_DIMS, preferred_element_type=F32)
    p = jnp.exp2(s - jnp.max(s, axis=-1, keepdims=True)).astype(BF16)
    acc = jnp.dot(p, v_sc[...], preferred_element_type=F32)
    _finish_head(acc, (lq1_ref, lk1_ref, lq2_ref, lk2_ref), sg_ref, gz_ref, o_ref, tq, lam_init)


def _attention_ctx(q, k, v, gz, lams, subln_g, lam_init):
    bsz, _, lt, _ = q.shape
    tq = CTX_LEN
    lam_spec = pl.BlockSpec((1, DIFF_HEAD_DIM), lambda b, h: (0, 0))
    head_blk = pl.BlockSpec((1, 1, tq, HEAD_W), lambda b, h: (b, h, 0, 0))
    tok_blk = pl.BlockSpec((1, tq, HEAD_W), lambda b, h: (b, 0, h))
    return pl.pallas_call(
        functools.partial(_attn_ctx_kernel, tq=tq, lam_init=lam_init),
        grid=(bsz, DIFF_HEADS),
        in_specs=[lam_spec, lam_spec, lam_spec, lam_spec, head_blk, head_blk, head_blk, tok_blk,
                  pl.BlockSpec((1, HEAD_W), lambda b, h: (0, 0))],
        out_specs=tok_blk,
        out_shape=jax.ShapeDtypeStruct((bsz, lt, W_B), BF16),
        input_output_aliases={GZ_ARG: 0},
        scratch_shapes=[
            pltpu.VMEM((2 * tq, HEAD_W), BF16),
            pltpu.VMEM((tq, 2 * HEAD_W), BF16),
        ],
        compiler_params=pltpu.CompilerParams(
            dimension_semantics=("arbitrary", "arbitrary"),
            vmem_limit_bytes=VMEM_LIMIT_BYTES),
        name="diff_attention_ctx",
    )(*lams, q, k, v, gz, subln_g)


def _outproj_kernel(x_ref, mod_ref, ya_ref, yb_ref, p_ref, pprev_ref, pnext_ref, gc_ref,
                    cw_ref, cb_ref, w_ref, fg_ref, o_ref, *, tm, row_off, lt, final):
    i = pl.program_id(1)
    row = row_off + i * tm + lax.broadcasted_iota(jnp.int32, (tm, 1), 0)
    local = lax.broadcasted_iota(jnp.int32, (tm, 1), 0)
    p = p_ref[0].astype(F32)
    prev_row = pprev_ref[0, HALO - 1:HALO, :].astype(F32)
    next_row = pnext_ref[0, 0:1, :].astype(F32)
    up = jnp.where(local == 0, prev_row, pltpu.roll(p, 1, 0))
    dn = jnp.where(local == tm - 1, next_row, pltpu.roll(p, tm - 1, 0))
    up = jnp.where((row == 0) | (row == CTX_LEN), 0.0, up)
    dn = jnp.where((row == CTX_LEN - 1) | (row == lt - 1), 0.0, dn)
    cw = cw_ref[...]
    conv = up * cw[0:1] + p * cw[1:2] + dn * cw[2:3] + cb_ref[...]
    yc = (gc_ref[0].astype(F32) * conv).astype(BF16)

    y = jnp.dot(ya_ref[0], w_ref[0:W_A], preferred_element_type=F32)
    y = y + jnp.dot(yb_ref[0], w_ref[W_A:W_A + W_B], preferred_element_type=F32)
    y = y + jnp.dot(yc, w_ref[W_A + W_B:D_MIX], preferred_element_type=F32)

    mod = mod_ref[0]
    gate = jnp.where(row < CTX_LEN, mod[5:6], mod[2:3])
    x = x_ref[0] + gate * y
    if final:
        x = x * lax.rsqrt(jnp.mean(x * x, axis=-1, keepdims=True) + EPS) * fg_ref[...]
    o_ref[0] = x


def _outproj(xs, mod_l, ya, yb, p, gc, conv_w, conv_b, w_out, final_g, final):
    bsz, lt, _ = xs.shape
    tm = FINAL_OUT_TILE if final else OUT_TILE
    blk_off = CTX_LEN // tm if final else 0
    n_t = lt // tm - blk_off
    hpt = tm // HALO
    n_halo = lt // HALO
    tok = lambda width: pl.BlockSpec((1, tm, width), lambda b, i: (b, i + blk_off, 0))
    const = lambda shape: pl.BlockSpec(shape, lambda b, i: (0,) * len(shape))
    prev = pl.BlockSpec((1, HALO, W_C),
                        lambda b, i: (b, jnp.maximum((i + blk_off) * hpt - 1, 0), 0))
    nxt = pl.BlockSpec((1, HALO, W_C),
                       lambda b, i: (b, jnp.minimum((i + blk_off + 1) * hpt, n_halo - 1), 0))
    out_rows = n_t * tm
    return pl.pallas_call(
        functools.partial(_outproj_kernel, tm=tm, row_off=blk_off * tm, lt=lt, final=final),
        grid=(bsz, n_t),
        in_specs=[
            tok(D_MODEL),
            pl.BlockSpec((1, MOD_ROWS, D_MODEL), lambda b, i: (b, 0, 0)),
            tok(W_A), tok(W_B), tok(W_C), prev, nxt, tok(W_C),
            const((3, W_C)), const((1, W_C)), const((D_MIX, D_MODEL)), const((1, D_MODEL)),
        ],
        out_specs=pl.BlockSpec((1, tm, D_MODEL), lambda b, i: (b, i, 0)),
        out_shape=jax.ShapeDtypeStruct((bsz, out_rows, D_MODEL), F32),
        compiler_params=pltpu.CompilerParams(
            dimension_semantics=("arbitrary", "arbitrary"),
            vmem_limit_bytes=VMEM_LIMIT_BYTES),
        name="out_projection",
    )(xs, mod_l, ya, yb, p, p, p, gc, conv_w, conv_b, w_out, final_g)


def _rope_tables(length):
    rows = length // GRID_W
    t = jnp.arange(rows * GRID_W)
    pos = jnp.stack([t // GRID_W, t % GRID_W], axis=1).astype(F32)
    inv = ROPE_BASE ** (-jnp.arange(ROPE_PAIRS, dtype=F32) / ROPE_PAIRS)
    ang = pos[:, :, None] * inv
    lane = jnp.arange(HEAD_W)
    axis = (lane % DIFF_HEAD_DIM) // (2 * ROPE_PAIRS)
    pair = lane % ROPE_PAIRS
    second = (lane % (2 * ROPE_PAIRS)) >= ROPE_PAIRS
    cos = jnp.cos(ang)[:, axis, pair]
    sin = jnp.sin(ang)[:, axis, pair]
    sin_lo = jnp.where(second, 0.0, -sin)
    sin_hi = jnp.where(second, sin, 0.0)
    ident = jnp.ones((CTX_LEN, HEAD_W), F32)
    zeros = jnp.zeros((CTX_LEN, HEAD_W), F32)
    return (jnp.concatenate([ident, cos]), jnp.concatenate([zeros, sin_lo]),
            jnp.concatenate([zeros, sin_hi]))


def kernel(x, c, ctx, c_ctx, w_mod, b_mod, norm_g, w_in, w_out, sgu_norm_g, sgu_w, sgu_b,
           lambda_q1, lambda_k1, lambda_q2, lambda_k2, subln_g, conv_w, conv_b, final_g):
    bsz, length, _ = x.shape
    assert ctx.shape[1] == CTX_LEN and bsz + 1 <= MOD_ROWS
    lt = CTX_LEN + length
    assert lt % IN_TILE == 0 and lt % OUT_TILE == 0
    assert lt % FINAL_OUT_TILE == 0 and CTX_LEN % FINAL_OUT_TILE == 0
    assert CTX_LEN % Q_TILE == 0 and lt % K_TILE == 0

    cond = jnp.zeros((MOD_ROWS, D_MODEL), F32).at[:bsz].set(c).at[bsz].set(c_ctx)
    mod = _modulation(cond, w_mod, b_mod)
    mod_b = jnp.transpose(mod[:, :, :bsz], (0, 2, 1, 3))
    mod_c = jnp.broadcast_to(mod[:, None, :, bsz], (DEPTH, bsz, 3, D_MODEL))
    pad = jnp.zeros((DEPTH, bsz, MOD_ROWS - 6, D_MODEL), F32)
    mod_rows = jnp.concatenate([mod_b, mod_c, pad], axis=2)

    cos, slo, shi = _rope_tables(length)
    w_in_b = w_in.astype(BF16)
    w_out_b = w_out.astype(BF16)
    ws = sgu_w.reshape(DEPTH, GMLP_HEADS * GMLP_CHUNK, GMLP_CHUNK).astype(BF16)
    bs = jnp.repeat(jnp.transpose(sgu_b, (0, 2, 1)), GMLP_HEAD_DIM, axis=2)

    half_ind = (jnp.arange(W_B)[:, None] // DIFF_HEAD_DIM == jnp.arange(128)[None, :]).astype(BF16)
    n_halves = 2 * DIFF_HEADS

    def norm_bounds(n):
        n = jnp.max(n, axis=(1, 2))[:, :n_halves].reshape(bsz, DIFF_HEADS, 2)
        return jnp.sqrt(n) * NORM_SLACK

    xs = jnp.concatenate([ctx, x], axis=1)
    for l in range(DEPTH):
        final = l == DEPTH - 1
        lam_init = 0.8 - 0.6 * math.exp(-0.3 * l)
        ya, q, k, v, gz, p, gc, qn, kn = _inproj(
            xs, mod_rows[l], norm_g[l][None], w_in_b[l], cos, slo, shi,
            sgu_norm_g[l][None], ws[l], bs[l], half_ind)
        lams = (lambda_q1[l][None], lambda_k1[l][None], lambda_q2[l][None], lambda_k2[l][None])
        yb = _attention_ctx(q, k, v, gz, lams, subln_g[l][None], lam_init)
        q_norm, k_norm = norm_bounds(qn), norm_bounds(kn)
        k_bound = jnp.zeros((bsz, DIFF_HEADS, MOD_ROWS, 128), F32).at[:, :, 0:2, :].set(
            jnp.broadcast_to(k_norm[..., None], (bsz, DIFF_HEADS, 2, 128)))
        bounded = jnp.max(q_norm * k_norm) <= MAX_SCORE_BOUND
        yb = lax.cond(
            bounded,
            lambda q, k, v, yb, kb: _attention_bounded(
                q, k, v, yb, lams, subln_g[l][None], kb, lam_init,
                q_row0=CTX_LEN, tk=K_TILE, kt=K_TILE),
            lambda q, k, v, yb, kb: _attention_pipelined(
                q, k, v, yb, lams, subln_g[l][None], lam_init,
                q_row0=CTX_LEN, tk=K_TILE, kt=V_TILE),
            q, k, v, yb, k_bound)
        xs = _outproj(xs, mod_rows[l], ya, yb, p, gc, conv_w[l], conv_b[l][None],
                      w_out_b[l], final_g[None], final)
    return xs
```

```python
import functools
import math

import jax
import jax.numpy as jnp
from jax import lax
from jax.experimental import pallas as pl
from jax.experimental.pallas import tpu as pltpu

D_MODEL = 1024
DEPTH = 4
CTX_LEN = 256
GRID_W = 64

GMLP_HEADS = 4
GMLP_HEAD_DIM = 64
GMLP_CHUNK = 128
W_A = GMLP_HEADS * GMLP_HEAD_DIM
DIFF_HEADS = 4
DIFF_HEAD_DIM = 64
HEAD_W = 2 * DIFF_HEAD_DIM
W_B = DIFF_HEADS * HEAD_W
W_C = 256
D_MIX = W_A + W_B + W_C
D_IN = 3 * W_A + 4 * W_B + 4 * W_C

COL_A = 0
COL_Q = 3 * W_A
COL_K = COL_Q + W_B
COL_V = COL_K + W_B
COL_C = COL_V + 2 * W_B

ROPE_BASE = 10000.0
ROPE_PAIRS = DIFF_HEAD_DIM // 4
EPS = 1e-6

F32 = jnp.float32
BF16 = jnp.bfloat16

VMEM_LIMIT_BYTES = 56 * 1024 * 1024
MOD_ROWS = 8

IN_TILE = 384
OUT_TILE = 768
FINAL_OUT_TILE = 512
Q_TILE = 256
Q_TILE_BOUNDED = 512
K_TILE = 2816
V_TILE = 256
HALO = 16


def _silu(x):
    return x * jax.nn.sigmoid(x)


def _mod_kernel(cond_ref, w_ref, b_ref, o_ref):
    a = _silu(cond_ref[...])
    o_ref[0, 0] = jnp.dot(a, w_ref[0], preferred_element_type=F32) + b_ref[0, 0]


def _modulation(cond, w_mod, b_mod):
    return pl.pallas_call(
        _mod_kernel,
        grid=(DEPTH, 3),
        in_specs=[
            pl.BlockSpec((MOD_ROWS, D_MODEL), lambda l, j: (0, 0)),
            pl.BlockSpec((1, D_MODEL, D_MODEL), lambda l, j: (l, 0, j)),
            pl.BlockSpec((1, 1, 1, D_MODEL), lambda l, j: (l, j, 0, 0)),
        ],
        out_specs=pl.BlockSpec((1, 1, MOD_ROWS, D_MODEL), lambda l, j: (l, j, 0, 0)),
        out_shape=jax.ShapeDtypeStruct((DEPTH, 3, MOD_ROWS, D_MODEL), F32),
        compiler_params=pltpu.CompilerParams(
            dimension_semantics=("arbitrary", "arbitrary"),
            vmem_limit_bytes=VMEM_LIMIT_BYTES),
        name="adaln_modulation",
    )(cond, w_mod, b_mod.reshape(DEPTH, 3, 1, D_MODEL))


def _rope(t, cos, sin_lo, sin_hi):
    return (t * cos + pltpu.roll(t, HEAD_W - ROPE_PAIRS, 1) * sin_lo
            + pltpu.roll(t, ROPE_PAIRS, 1) * sin_hi)


def _inproj_kernel(x_ref, mod_ref, g_ref, w_ref, cos_ref, slo_ref, shi_ref,
                   sgug_ref, ws_ref, bs_ref, half_ref,
                   ya_ref, q_ref, k_ref, v_ref, gz_ref, p_ref, gc_ref, qn_ref, kn_ref, *, tm, lat):
    i = pl.program_id(1)
    x = x_ref[0]
    row = i * tm + lax.broadcasted_iota(jnp.int32, (tm, 1), 0)
    is_ctx = row >= lat
    mod = mod_ref[0]
    shift = jnp.where(is_ctx, mod[3:4], mod[0:1])
    scale = jnp.where(is_ctx, mod[4:5], mod[1:2])
    h = x * lax.rsqrt(jnp.mean(x * x, axis=-1, keepdims=True) + EPS) * g_ref[...]
    h = (h * (1.0 + scale) + shift).astype(BF16)

    ra = jnp.dot(h, w_ref[:, COL_A:COL_A + 3 * W_A], preferred_element_type=F32)
    u = jax.nn.gelu(ra[:, 0:W_A])
    v = jax.nn.gelu(ra[:, W_A:2 * W_A])
    gate_a = _silu(ra[:, 2 * W_A:3 * W_A])
    vn = v * lax.rsqrt(jnp.mean(v * v, axis=-1, keepdims=True) + EPS) * sgug_ref[...]
    vn = vn.astype(BF16)
    lane = lax.broadcasted_iota(jnp.int32, (GMLP_CHUNK, W_A), 1)
    for c in range(tm // GMLP_CHUNK):
        rows = slice(c * GMLP_CHUNK, (c + 1) * GMLP_CHUNK)
        r = jnp.dot(ws_ref[...], vn[rows], preferred_element_type=F32)
        mixed = r[0:GMLP_CHUNK]
        for hd in range(1, GMLP_HEADS):
            mixed = jnp.where(lane >= hd * GMLP_HEAD_DIM,
                              r[hd * GMLP_CHUNK:(hd + 1) * GMLP_CHUNK], mixed)
        mixed = mixed + bs_ref[...]
        ya_ref[0, rows, :] = (u[rows] * mixed * gate_a[rows]).astype(BF16)

    cos, slo, shi = cos_ref[...], slo_ref[...], shi_ref[...]
    rq = jnp.dot(h, w_ref[:, COL_Q:COL_Q + W_B], preferred_element_type=F32)
    rk = jnp.dot(h, w_ref[:, COL_K:COL_K + W_B], preferred_element_type=F32)
    q_scale = math.log2(math.e) / math.sqrt(DIFF_HEAD_DIM)
    q_heads, k_heads = [], []
    for hd in range(DIFF_HEADS):
        cols = slice(hd * HEAD_W, (hd + 1) * HEAD_W)
        q_heads.append(_rope(rq[:, cols], cos, slo, shi) * q_scale)
        k_heads.append(_rope(rk[:, cols], cos, slo, shi))
        q_ref[0, hd] = q_heads[hd].astype(BF16)
        k_ref[0, hd] = k_heads[hd].astype(BF16)
    for heads, n_ref in ((q_heads, qn_ref), (k_heads, kn_ref)):
        t = jnp.concatenate(heads, axis=1)
        n = jnp.dot((t * t).astype(BF16), half_ref[...], preferred_element_type=F32)
        best = n[0:MOD_ROWS]
        for r in range(1, tm // MOD_ROWS):
            best = jnp.maximum(best, n[r * MOD_ROWS:(r + 1) * MOD_ROWS])
        n_ref[0, 0] = best
    rz = jnp.dot(h, w_ref[:, COL_V + W_B:COL_V + 2 * W_B], preferred_element_type=F32)
    gz_ref[0] = _silu(rz).astype(BF16)

    rc = jnp.dot(h, w_ref[:, COL_C:COL_C + 4 * W_C], preferred_element_type=F32)
    p_ref[0] = (rc[:, W_C:2 * W_C] * rc[:, 2 * W_C:3 * W_C]).astype(BF16)
    gc_ref[0] = (rc[:, 0:W_C] * _silu(rc[:, 3 * W_C:4 * W_C])).astype(BF16)

    rv = jnp.dot(h, w_ref[:, COL_V:COL_V + W_B], preferred_element_type=F32)
    for hd in range(DIFF_HEADS):
        v_ref[0, hd] = rv[:, hd * HEAD_W:(hd + 1) * HEAD_W].astype(BF16)


def _inproj(xs, mod_l, norm_g, w_in, cos, slo, shi, sgu_g, ws, bs, half_ind):
    bsz, lt, _ = xs.shape
    tm = IN_TILE
    n_t = lt // tm
    norm_blk = pl.BlockSpec((1, 1, MOD_ROWS, 128), lambda b, i: (b, i, 0, 0))
    norm_shape = jax.ShapeDtypeStruct((bsz, n_t, MOD_ROWS, 128), F32)
    tok = lambda width: pl.BlockSpec((1, tm, width), lambda b, i: (b, i, 0))
    head = pl.BlockSpec((1, DIFF_HEADS, tm, HEAD_W), lambda b, i: (b, 0, i, 0))
    const = lambda shape: pl.BlockSpec(shape, lambda b, i: (0,) * len(shape))
    tab = pl.BlockSpec((tm, HEAD_W), lambda b, i: (i, 0))
    tok_shape = lambda width: jax.ShapeDtypeStruct((bsz, lt, width), BF16)
    head_shape = jax.ShapeDtypeStruct((bsz, DIFF_HEADS, lt, HEAD_W), BF16)
    return pl.pallas_call(
        functools.partial(_inproj_kernel, tm=tm, lat=lt - CTX_LEN),
        grid=(bsz, lt // tm),
        in_specs=[
            tok(D_MODEL),
            pl.BlockSpec((1, MOD_ROWS, D_MODEL), lambda b, i: (b, 0, 0)),
            const((1, D_MODEL)),
            const((D_MODEL, D_IN)),
            tab, tab, tab,
            const((1, W_A)),
            const((GMLP_HEADS * GMLP_CHUNK, GMLP_CHUNK)),
            const((GMLP_CHUNK, W_A)),
            const((W_B, 128)),
        ],
        out_specs=[tok(W_A), head, head, head, tok(W_B), tok(W_C), tok(W_C), norm_blk, norm_blk],
        out_shape=[tok_shape(W_A), head_shape, head_shape, head_shape,
                   tok_shape(W_B), tok_shape(W_C), tok_shape(W_C), norm_shape, norm_shape],
        compiler_params=pltpu.CompilerParams(
            dimension_semantics=("arbitrary", "arbitrary"),
            vmem_limit_bytes=VMEM_LIMIT_BYTES),
        name="in_projection",
    )(xs, mod_l, norm_g, w_in, cos, slo, shi, sgu_g, ws, bs, half_ind)


NT_DIMS = (((1,), (1,)), ((), ()))
NORM_SLACK = 1.02
MAX_SCORE_BOUND = 50.0
GZ_ARG = 7


def _widen_values(v_sc, v_ref):
    v_sc[:, 0:HEAD_W] = v_ref[0, 0]
    v_sc[:, HEAD_W:2 * HEAD_W] = jnp.ones((v_sc.shape[0], HEAD_W), BF16)


def _stack_q(qz_sc, q_ref, tq):
    q = q_ref[0, 0]
    lane = lax.broadcasted_iota(jnp.int32, (tq, HEAD_W), 1)
    zero = jnp.zeros_like(q)
    qz_sc[0:tq] = jnp.where(lane < DIFF_HEAD_DIM, q, zero)
    qz_sc[tq:2 * tq] = jnp.where(lane >= DIFF_HEAD_DIM, q, zero)
    return qz_sc[...]


def _finish_head(acc, lam_refs, sg_ref, gz_ref, o_ref, tq, lam_init):
    lq1_ref, lk1_ref, lq2_ref, lk2_ref = lam_refs
    o = acc[:, 0:HEAD_W] / acc[:, HEAD_W:2 * HEAD_W]
    lam = (jnp.exp(jnp.sum(lq1_ref[...] * lk1_ref[...]))
           - jnp.exp(jnp.sum(lq2_ref[...] * lk2_ref[...])) + lam_init)
    o = o[0:tq] - lam * o[tq:2 * tq]
    o = o * lax.rsqrt(jnp.mean(o * o, axis=-1, keepdims=True) + EPS) * sg_ref[...]
    o_ref[0] = (o * (1.0 - lam_init) * gz_ref[0].astype(F32)).astype(BF16)


def _pipeline_maps(bsz, n_q):
    n_tiles = bsz * DIFF_HEADS * n_q

    def score_tile(g):
        t = jnp.minimum(g, n_tiles - 1)
        return t // (DIFF_HEADS * n_q), (t // n_q) % DIFF_HEADS, t % n_q

    def value_tile(g):
        return score_tile(jnp.maximum(g - 1, 0))

    def score_head(g):
        b, h, _ = score_tile(g)
        return (b, h, 0, 0)

    def score_rows(g):
        b, h, i = score_tile(g)
        return (b, h, i, 0)

    def value_head(g):
        b, h, _ = value_tile(g)
        return (b, h, 0, 0)

    def value_tok(g):
        b, h, i = value_tile(g)
        return (b, i, h)

    return n_tiles, score_head, score_rows, value_head, value_tok


def _attn_bounded_kernel(lq1_ref, lk1_ref, lq2_ref, lk2_ref, q_ref, k_ref, v_ref, gz_ref, sg_ref,
                         kb_ref, o_ref, qz_sc, e0_sc, e1_sc, l0_sc, l1_sc,
                         *, tq, tk, kt, lam_init):
    g = pl.program_id(0)
    n_keys = e0_sc.shape[1]

    @pl.when(g == 0)
    def _():
        e1_sc[...] = jnp.zeros_like(e1_sc)
        l1_sc[...] = jnp.ones_like(l1_sc)

    def step(e_w, l_w, e_r, l_r):
        qz = _stack_q(qz_sc, q_ref, tq)
        qf = qz.astype(F32)
        q_norm = jnp.sqrt(jnp.sum(qf * qf, axis=-1, keepdims=True))
        row = lax.broadcasted_iota(jnp.int32, (2 * tq, 1), 0)
        kb = kb_ref[0, 0]
        bound = q_norm * jnp.where(row < tq, kb[0:1], kb[1:2])
        l = None
        for c in range(n_keys // tk):
            s = lax.dot_general(qz, k_ref[0, 0, c * tk:(c + 1) * tk, :], NT_DIMS,
                                preferred_element_type=F32)
            for t in range(tk // 128):
                e = jnp.exp2(s[:, t * 128:(t + 1) * 128] - bound)
                l = e if l is None else l + e
                e_w[:, c * tk + t * 128:c * tk + (t + 1) * 128] = e.astype(BF16)
        l_w[...] = jnp.broadcast_to(jnp.sum(l, axis=-1, keepdims=True), l_w.shape)

        l_prev = l_r[...]
        lam = (jnp.exp(jnp.sum(lq1_ref[...] * lk1_ref[...]))
               - jnp.exp(jnp.sum(lq2_ref[...] * lk2_ref[...])) + lam_init)
        ratio = (lam * l_prev[0:tq] / l_prev[tq:2 * tq]).astype(BF16)
        ratio = jnp.concatenate([ratio] * (kt // 128), axis=1)
        acc = None
        for c in range(n_keys // kt):
            a = e_r[0:tq, c * kt:(c + 1) * kt] - ratio * e_r[tq:2 * tq, c * kt:(c + 1) * kt]
            vv = v_ref[0, 0, c * kt:(c + 1) * kt, :]
            d = jnp.dot(a, jnp.concatenate([vv, vv], axis=1), preferred_element_type=F32)
            acc = d if acc is None else acc + d
        o = acc[:, 0:HEAD_W] / l_prev[0:tq]
        o = o * lax.rsqrt(jnp.mean(o * o, axis=-1, keepdims=True) + EPS) * sg_ref[...]
        o_ref[0] = (o * (1.0 - lam_init) * gz_ref[0].astype(F32)).astype(BF16)

    @pl.when(g % 2 == 0)
    def _():
        step(e0_sc, l0_sc, e1_sc, l1_sc)

    @pl.when(g % 2 == 1)
    def _():
        step(e1_sc, l1_sc, e0_sc, l0_sc)


def _attention_bounded(q, k, v, gz, lams, subln_g, k_bound, lam_init, *, tk, kt):
    bsz, _, lt, _ = q.shape
    tq = Q_TILE_BOUNDED
    n_q = (lt - CTX_LEN) // tq
    n_tiles, score_head, score_rows, value_head, value_tok = _pipeline_maps(bsz, n_q)
    lam_spec = pl.BlockSpec((1, DIFF_HEAD_DIM), lambda g: (0, 0))
    seq_blk = (1, 1, lt, HEAD_W)
    return pl.pallas_call(
        functools.partial(_attn_bounded_kernel, tq=tq, tk=tk, kt=kt, lam_init=lam_init),
        grid=(n_tiles + 1,),
        in_specs=[
            lam_spec, lam_spec, lam_spec, lam_spec,
            pl.BlockSpec((1, 1, tq, HEAD_W), score_rows),
            pl.BlockSpec(seq_blk, score_head),
            pl.BlockSpec(seq_blk, value_head),
            pl.BlockSpec((1, tq, HEAD_W), value_tok),
            pl.BlockSpec((1, HEAD_W), lambda g: (0, 0)),
            pl.BlockSpec((1, 1, MOD_ROWS, 128), score_head),
        ],
        out_specs=pl.BlockSpec((1, tq, HEAD_W), value_tok),
        out_shape=jax.ShapeDtypeStruct((bsz, lt, W_B), BF16),
        input_output_aliases={GZ_ARG: 0},
        scratch_shapes=[
            pltpu.VMEM((2 * tq, HEAD_W), BF16),
            pltpu.VMEM((2 * tq, lt), BF16),
            pltpu.VMEM((2 * tq, lt), BF16),
            pltpu.VMEM((2 * tq, 128), F32),
            pltpu.VMEM((2 * tq, 128), F32),
        ],
        compiler_params=pltpu.CompilerParams(
            dimension_semantics=("arbitrary",),
            vmem_limit_bytes=VMEM_LIMIT_BYTES),
        name="diff_attention_bounded",
    )(*lams, q, k, v, gz, subln_g, k_bound)


def _attn_pipe_kernel(lq1_ref, lk1_ref, lq2_ref, lk2_ref, q_ref, k_ref, v_ref, gz_ref, sg_ref,
                      o_ref, qz_sc, v_sc, s0_sc, s1_sc, m0_sc, m1_sc,
                      *, tq, tk, kt, n_q, lam_init):
    g = pl.program_id(0)
    n_keys = s0_sc.shape[1]

    @pl.when(g == 0)
    def _():
        s1_sc[...] = jnp.zeros_like(s1_sc)
        m1_sc[...] = jnp.zeros_like(m1_sc)

    @pl.when((g == 0) | ((g - 1) % n_q == 0))
    def _():
        _widen_values(v_sc, v_ref)

    def step(s_w, m_w, s_r, m_r):
        qz = _stack_q(qz_sc, q_ref, tq)
        m = None
        for c in range(n_keys // tk):
            s = lax.dot_general(qz, k_ref[0, 0, c * tk:(c + 1) * tk, :], NT_DIMS,
                                preferred_element_type=F32)
            s_w[:, c * tk:(c + 1) * tk] = s
            for t in range(tk // 128):
                blk = s[:, t * 128:(t + 1) * 128]
                m = blk if m is None else jnp.maximum(m, blk)
        m_w[...] = jnp.broadcast_to(jnp.max(m, axis=-1, keepdims=True), m_w.shape)

        m_prev = jnp.concatenate([m_r[...]] * (kt // 128), axis=1)
        acc = None
        for c in range(n_keys // kt):
            p = jnp.exp2(s_r[:, c * kt:(c + 1) * kt] - m_prev).astype(BF16)
            d = jnp.dot(p, v_sc[c * kt:(c + 1) * kt, :], preferred_element_type=F32)
            acc = d if acc is None else acc + d
        _finish_head(acc, (lq1_ref, lk1_ref, lq2_ref, lk2_ref), sg_ref, gz_ref, o_ref,
                     tq, lam_init)

    @pl.when(g % 2 == 0)
    def _():
        step(s0_sc, m0_sc, s1_sc, m1_sc)

    @pl.when(g % 2 == 1)
    def _():
        step(s1_sc, m1_sc, s0_sc, m0_sc)


def _attention_pipelined(q, k, v, gz, lams, subln_g, lam_init, *, tk, kt):
    bsz, _, lt, _ = q.shape
    tq = Q_TILE
    n_q = (lt - CTX_LEN) // tq
    n_tiles, score_head, score_rows, value_head, value_tok = _pipeline_maps(bsz, n_q)
    lam_spec = pl.BlockSpec((1, DIFF_HEAD_DIM), lambda g: (0, 0))
    seq_blk = (1, 1, lt, HEAD_W)
    return pl.pallas_call(
        functools.partial(_attn_pipe_kernel, tq=tq, tk=tk, kt=kt, n_q=n_q, lam_init=lam_init),
        grid=(n_tiles + 1,),
        in_specs=[
            lam_spec, lam_spec, lam_spec, lam_spec,
            pl.BlockSpec((1, 1, tq, HEAD_W), score_rows),
            pl.BlockSpec(seq_blk, score_head),
            pl.BlockSpec(seq_blk, value_head),
            pl.BlockSpec((1, tq, HEAD_W), value_tok),
            pl.BlockSpec((1, HEAD_W), lambda g: (0, 0)),
        ],
        out_specs=pl.BlockSpec((1, tq, HEAD_W), value_tok),
        out_shape=jax.ShapeDtypeStruct((bsz, lt, W_B), BF16),
        input_output_aliases={GZ_ARG: 0},
        scratch_shapes=[
            pltpu.VMEM((2 * tq, HEAD_W), BF16),
            pltpu.VMEM((lt, 2 * HEAD_W), BF16),
            pltpu.VMEM((2 * tq, lt), F32),
            pltpu.VMEM((2 * tq, lt), F32),
            pltpu.VMEM((2 * tq, 128), F32),
            pltpu.VMEM((2 * tq, 128), F32),
        ],
        compiler_params=pltpu.CompilerParams(
            dimension_semantics=("arbitrary",),
            vmem_limit_bytes=VMEM_LIMIT_BYTES),
        name="diff_attention_latent",
    )(*lams, q, k, v, gz, subln_g)


def _attn_ctx_kernel(lq1_ref, lk1_ref, lq2_ref, lk2_ref, q_ref, k_ref, v_ref, gz_ref, sg_ref,
                     o_ref, qz_sc, v_sc, *, tq, lam_init):
    _widen_values(v_sc, v_ref)
    qz = _stack_q(qz_sc, q_ref, tq)
    s = lax.dot_general(qz, k_ref[0, 0], NT_DIMS, preferred_element_type=F32)
    p = jnp.exp2(s - jnp.max(s, axis=-1, keepdims=True)).astype(BF16)
    acc = jnp.dot(p, v_sc[...], preferred_element_type=F32)
    _finish_head(acc, (lq1_ref, lk1_ref, lq2_ref, lk2_ref), sg_ref, gz_ref, o_ref, tq, lam_init)


def _attention_ctx(q, k, v, gz, lams, subln_g, lam_init):
    bsz, _, lt, _ = q.shape
    tq = CTX_LEN
    blk = (lt - CTX_LEN) // tq
    lam_spec = pl.BlockSpec((1, DIFF_HEAD_DIM), lambda b, h: (0, 0))
    head_blk = pl.BlockSpec((1, 1, tq, HEAD_W), lambda b, h: (b, h, blk, 0))
    tok_blk = pl.BlockSpec((1, tq, HEAD_W), lambda b, h: (b, blk, h))
    return pl.pallas_call(
        functools.partial(_attn_ctx_kernel, tq=tq, lam_init=lam_init),
        grid=(bsz, DIFF_HEADS),
        in_specs=[lam_spec, lam_spec, lam_spec, lam_spec, head_blk, head_blk, head_blk, tok_blk,
                  pl.BlockSpec((1, HEAD_W), lambda b, h: (0, 0))],
        out_specs=tok_blk,
        out_shape=jax.ShapeDtypeStruct((bsz, lt, W_B), BF16),
        input_output_aliases={GZ_ARG: 0},
        scratch_shapes=[
            pltpu.VMEM((2 * tq, HEAD_W), BF16),
            pltpu.VMEM((tq, 2 * HEAD_W), BF16),
        ],
        compiler_params=pltpu.CompilerParams(
            dimension_semantics=("arbitrary", "arbitrary"),
            vmem_limit_bytes=VMEM_LIMIT_BYTES),
        name="diff_attention_ctx",
    )(*lams, q, k, v, gz, subln_g)


def _outproj_kernel(x_ref, mod_ref, ya_ref, yb_ref, p_ref, pprev_ref, pnext_ref, gc_ref,
                    cw_ref, cb_ref, w_ref, fg_ref, o_ref, *, tm, lat, lt, final):
    i = pl.program_id(1)
    row = i * tm + lax.broadcasted_iota(jnp.int32, (tm, 1), 0)
    local = lax.broadcasted_iota(jnp.int32, (tm, 1), 0)
    p = p_ref[0].astype(F32)
    prev_row = pprev_ref[0, HALO - 1:HALO, :].astype(F32)
    next_row = pnext_ref[0, 0:1, :].astype(F32)
    up = jnp.where(local == 0, prev_row, pltpu.roll(p, 1, 0))
    dn = jnp.where(local == tm - 1, next_row, pltpu.roll(p, tm - 1, 0))
    up = jnp.where((row == 0) | (row == lat), 0.0, up)
    dn = jnp.where((row == lat - 1) | (row == lt - 1), 0.0, dn)
    cw = cw_ref[...]
    conv = up * cw[0:1] + p * cw[1:2] + dn * cw[2:3] + cb_ref[...]
    yc = (gc_ref[0].astype(F32) * conv).astype(BF16)

    y = jnp.dot(ya_ref[0], w_ref[0:W_A], preferred_element_type=F32)
    y = y + jnp.dot(yb_ref[0], w_ref[W_A:W_A + W_B], preferred_element_type=F32)
    y = y + jnp.dot(yc, w_ref[W_A + W_B:D_MIX], preferred_element_type=F32)

    mod = mod_ref[0]
    gate = jnp.where(row >= lat, mod[5:6], mod[2:3])
    x = x_ref[0] + gate * y
    if final:
        x = x * lax.rsqrt(jnp.mean(x * x, axis=-1, keepdims=True) + EPS) * fg_ref[...]
    o_ref[0] = x


def _outproj(xs, mod_l, ya, yb, p, gc, conv_w, conv_b, w_out, final_g, final):
    bsz, lt, _ = xs.shape
    lat = lt - CTX_LEN
    tm = FINAL_OUT_TILE if final else OUT_TILE
    n_t = (lat if final else lt) // tm
    hpt = tm // HALO
    n_halo = lt // HALO
    tok = lambda width: pl.BlockSpec((1, tm, width), lambda b, i: (b, i, 0))
    const = lambda shape: pl.BlockSpec(shape, lambda b, i: (0,) * len(shape))
    prev = pl.BlockSpec((1, HALO, W_C),
                        lambda b, i: (b, jnp.maximum(i * hpt - 1, 0), 0))
    nxt = pl.BlockSpec((1, HALO, W_C),
                       lambda b, i: (b, jnp.minimum((i + 1) * hpt, n_halo - 1), 0))
    out_rows = n_t * tm
    return pl.pallas_call(
        functools.partial(_outproj_kernel, tm=tm, lat=lat, lt=lt, final=final),
        grid=(bsz, n_t),
        in_specs=[
            tok(D_MODEL),
            pl.BlockSpec((1, MOD_ROWS, D_MODEL), lambda b, i: (b, 0, 0)),
            tok(W_A), tok(W_B), tok(W_C), prev, nxt, tok(W_C),
            const((3, W_C)), const((1, W_C)), const((D_MIX, D_MODEL)), const((1, D_MODEL)),
        ],
        out_specs=pl.BlockSpec((1, tm, D_MODEL), lambda b, i: (b, i, 0)),
        out_shape=jax.ShapeDtypeStruct((bsz, out_rows, D_MODEL), F32),
        compiler_params=pltpu.CompilerParams(
            dimension_semantics=("arbitrary", "arbitrary"),
            vmem_limit_bytes=VMEM_LIMIT_BYTES),
        name="out_projection",
    )(xs, mod_l, ya, yb, p, p, p, gc, conv_w, conv_b, w_out, final_g)


def _rope_tables(length):
    rows = length // GRID_W
    t = jnp.arange(rows * GRID_W)
    pos = jnp.stack([t // GRID_W, t % GRID_W], axis=1).astype(F32)
    inv = ROPE_BASE ** (-jnp.arange(ROPE_PAIRS, dtype=F32) / ROPE_PAIRS)
    ang = pos[:, :, None] * inv
    lane = jnp.arange(HEAD_W)
    axis = (lane % DIFF_HEAD_DIM) // (2 * ROPE_PAIRS)
    pair = lane % ROPE_PAIRS
    second = (lane % (2 * ROPE_PAIRS)) >= ROPE_PAIRS
    cos = jnp.cos(ang)[:, axis, pair]
    sin = jnp.sin(ang)[:, axis, pair]
    sin_lo = jnp.where(second, 0.0, -sin)
    sin_hi = jnp.where(second, sin, 0.0)
    ident = jnp.ones((CTX_LEN, HEAD_W), F32)
    zeros = jnp.zeros((CTX_LEN, HEAD_W), F32)
    return (jnp.concatenate([cos, ident]), jnp.concatenate([sin_lo, zeros]),
            jnp.concatenate([sin_hi, zeros]))


def kernel(x, c, ctx, c_ctx, w_mod, b_mod, norm_g, w_in, w_out, sgu_norm_g, sgu_w, sgu_b,
           lambda_q1, lambda_k1, lambda_q2, lambda_k2, subln_g, conv_w, conv_b, final_g):
    bsz, length, _ = x.shape
    assert ctx.shape[1] == CTX_LEN and bsz + 1 <= MOD_ROWS
    lt = CTX_LEN + length
    assert lt % IN_TILE == 0 and lt % OUT_TILE == 0
    assert length % FINAL_OUT_TILE == 0 and length % Q_TILE == 0 and length % Q_TILE_BOUNDED == 0
    assert length % CTX_LEN == 0 and lt % K_TILE == 0

    cond = jnp.zeros((MOD_ROWS, D_MODEL), F32).at[:bsz].set(c).at[bsz].set(c_ctx)
    mod = _modulation(cond, w_mod, b_mod)
    mod_b = jnp.transpose(mod[:, :, :bsz], (0, 2, 1, 3))
    mod_c = jnp.broadcast_to(mod[:, None, :, bsz], (DEPTH, bsz, 3, D_MODEL))
    pad = jnp.zeros((DEPTH, bsz, MOD_ROWS - 6, D_MODEL), F32)
    mod_rows = jnp.concatenate([mod_b, mod_c, pad], axis=2)

    cos, slo, shi = _rope_tables(length)
    w_in_b = w_in.astype(BF16)
    w_out_b = w_out.astype(BF16)
    ws = sgu_w.reshape(DEPTH, GMLP_HEADS * GMLP_CHUNK, GMLP_CHUNK).astype(BF16)
    bs = jnp.repeat(jnp.transpose(sgu_b, (0, 2, 1)), GMLP_HEAD_DIM, axis=2)

    half_ind = (jnp.arange(W_B)[:, None] // DIFF_HEAD_DIM == jnp.arange(128)[None, :]).astype(BF16)
    n_halves = 2 * DIFF_HEADS

    def norm_bounds(n):
        n = jnp.max(n, axis=(1, 2))[:, :n_halves].reshape(bsz, DIFF_HEADS, 2)
        return jnp.sqrt(n) * NORM_SLACK

    xs = jnp.concatenate([x, ctx], axis=1)
    for l in range(DEPTH):
        final = l == DEPTH - 1
        lam_init = 0.8 - 0.6 * math.exp(-0.3 * l)
        ya, q, k, v, gz, p, gc, qn, kn = _inproj(
            xs, mod_rows[l], norm_g[l][None], w_in_b[l], cos, slo, shi,
            sgu_norm_g[l][None], ws[l], bs[l], half_ind)
        lams = (lambda_q1[l][None], lambda_k1[l][None], lambda_q2[l][None], lambda_k2[l][None])
        yb = _attention_ctx(q, k, v, gz, lams, subln_g[l][None], lam_init)
        q_norm, k_norm = norm_bounds(qn), norm_bounds(kn)
        k_bound = jnp.zeros((bsz, DIFF_HEADS, MOD_ROWS, 128), F32).at[:, :, 0:2, :].set(
            jnp.broadcast_to(k_norm[..., None], (bsz, DIFF_HEADS, 2, 128)))
        bounded = jnp.max(q_norm * k_norm) <= MAX_SCORE_BOUND
        yb = lax.cond(
            bounded,
            lambda q, k, v, yb, kb: _attention_bounded(
                q, k, v, yb, lams, subln_g[l][None], kb, lam_init,
                tk=K_TILE, kt=K_TILE),
            lambda q, k, v, yb, kb: _attention_pipelined(
                q, k, v, yb, lams, subln_g[l][None], lam_init,
                tk=K_TILE, kt=V_TILE),
            q, k, v, yb, k_bound)
        xs = _outproj(xs, mod_rows[l], ya, yb, p, gc, conv_w[l], conv_b[l][None],
                      w_out_b[l], final_g[None], final)
    return xs
```

```python
import functools
import math

import jax
import jax.numpy as jnp
from jax import lax
from jax.experimental import pallas as pl
from jax.experimental.pallas import tpu as pltpu

D_MODEL = 1024
DEPTH = 4
CTX_LEN = 256
GRID_W = 64

GMLP_HEADS = 4
GMLP_HEAD_DIM = 64
GMLP_CHUNK = 128
W_A = GMLP_HEADS * GMLP_HEAD_DIM
DIFF_HEADS = 4
DIFF_HEAD_DIM = 64
HEAD_W = 2 * DIFF_HEAD_DIM
W_B = DIFF_HEADS * HEAD_W
W_C = 256
D_MIX = W_A + W_B + W_C
D_IN = 3 * W_A + 4 * W_B + 4 * W_C

COL_A = 0
COL_Q = 3 * W_A
COL_K = COL_Q + W_B
COL_V = COL_K + W_B
COL_C = COL_V + 2 * W_B

ROPE_BASE = 10000.0
ROPE_PAIRS = DIFF_HEAD_DIM // 4
EPS = 1e-6

F32 = jnp.float32
BF16 = jnp.bfloat16

VMEM_LIMIT_BYTES = 56 * 1024 * 1024
MOD_ROWS = 8

IN_TILE = 768
OUT_TILE = 768
FINAL_OUT_TILE = 512
Q_TILE = 256
Q_TILE_BOUNDED = 512
K_TILE = 2816
V_TILE = 256
HALO = 16


def _silu(x):
    return x * jax.nn.sigmoid(x)


def _mod_kernel(cond_ref, w_ref, b_ref, o_ref):
    a = _silu(cond_ref[...])
    o_ref[0, 0] = jnp.dot(a, w_ref[0], preferred_element_type=F32) + b_ref[0, 0]


def _modulation(cond, w_mod, b_mod):
    return pl.pallas_call(
        _mod_kernel,
        grid=(DEPTH, 3),
        in_specs=[
            pl.BlockSpec((MOD_ROWS, D_MODEL), lambda l, j: (0, 0)),
            pl.BlockSpec((1, D_MODEL, D_MODEL), lambda l, j: (l, 0, j)),
            pl.BlockSpec((1, 1, 1, D_MODEL), lambda l, j: (l, j, 0, 0)),
        ],
        out_specs=pl.BlockSpec((1, 1, MOD_ROWS, D_MODEL), lambda l, j: (l, j, 0, 0)),
        out_shape=jax.ShapeDtypeStruct((DEPTH, 3, MOD_ROWS, D_MODEL), F32),
        compiler_params=pltpu.CompilerParams(
            dimension_semantics=("arbitrary", "arbitrary"),
            vmem_limit_bytes=VMEM_LIMIT_BYTES),
        name="adaln_modulation",
    )(cond, w_mod, b_mod.reshape(DEPTH, 3, 1, D_MODEL))


def _rope(t, cos, sin_lo, sin_hi):
    return (t * cos + pltpu.roll(t, HEAD_W - ROPE_PAIRS, 1) * sin_lo
            + pltpu.roll(t, ROPE_PAIRS, 1) * sin_hi)


def _inproj_kernel(x_ref, mod_ref, g_ref, w_ref, cos_ref, slo_ref, shi_ref,
                   sgug_ref, ws_ref, bs_ref, half_ref,
                   ya_ref, q_ref, k_ref, v_ref, gz_ref, p_ref, gc_ref, qn_ref, kn_ref,
                   w_sc, *, tm, lat):
    i = pl.program_id(1)

    @pl.when((pl.program_id(0) == 0) & (i == 0))
    def _():
        w_sc[...] = w_ref[0].astype(BF16)

    x = x_ref[0]
    row = i * tm + lax.broadcasted_iota(jnp.int32, (tm, 1), 0)
    is_ctx = row >= lat
    mod = mod_ref[0]
    shift = jnp.where(is_ctx, mod[3:4], mod[0:1])
    scale = jnp.where(is_ctx, mod[4:5], mod[1:2])
    h = x * lax.rsqrt(jnp.mean(x * x, axis=-1, keepdims=True) + EPS) * g_ref[...]
    h = (h * (1.0 + scale) + shift).astype(BF16)

    ra = jnp.dot(h, w_sc[:, COL_A:COL_A + 3 * W_A], preferred_element_type=F32)
    u = jax.nn.gelu(ra[:, 0:W_A])
    v = jax.nn.gelu(ra[:, W_A:2 * W_A])
    gate_a = _silu(ra[:, 2 * W_A:3 * W_A])
    vn = v * lax.rsqrt(jnp.mean(v * v, axis=-1, keepdims=True) + EPS) * sgug_ref[...]
    vn = vn.astype(BF16)
    lane = lax.broadcasted_iota(jnp.int32, (GMLP_CHUNK, W_A), 1)
    for c in range(tm // GMLP_CHUNK):
        rows = slice(c * GMLP_CHUNK, (c + 1) * GMLP_CHUNK)
        r = jnp.dot(ws_ref[...], vn[rows], preferred_element_type=F32)
        mixed = r[0:GMLP_CHUNK]
        for hd in range(1, GMLP_HEADS):
            mixed = jnp.where(lane >= hd * GMLP_HEAD_DIM,
                              r[hd * GMLP_CHUNK:(hd + 1) * GMLP_CHUNK], mixed)
        mixed = mixed + bs_ref[...]
        ya_ref[0, rows, :] = (u[rows] * mixed * gate_a[rows]).astype(BF16)

    cos, slo, shi = cos_ref[...], slo_ref[...], shi_ref[...]
    rq = jnp.dot(h, w_sc[:, COL_Q:COL_Q + W_B], preferred_element_type=F32)
    rk = jnp.dot(h, w_sc[:, COL_K:COL_K + W_B], preferred_element_type=F32)
    q_scale = math.log2(math.e) / math.sqrt(DIFF_HEAD_DIM)
    q_heads, k_heads = [], []
    for hd in range(DIFF_HEADS):
        cols = slice(hd * HEAD_W, (hd + 1) * HEAD_W)
        q_heads.append(_rope(rq[:, cols], cos, slo, shi) * q_scale)
        k_heads.append(_rope(rk[:, cols], cos, slo, shi))
        q_ref[0, hd] = q_heads[hd].astype(BF16)
        k_ref[0, hd] = k_heads[hd].astype(BF16)
    for heads, n_ref in ((q_heads, qn_ref), (k_heads, kn_ref)):
        t = jnp.concatenate(heads, axis=1)
        n = jnp.dot((t * t).astype(BF16), half_ref[...], preferred_element_type=F32)
        best = n[0:MOD_ROWS]
        for r in range(1, tm // MOD_ROWS):
            best = jnp.maximum(best, n[r * MOD_ROWS:(r + 1) * MOD_ROWS])
        n_ref[0, 0] = best
    rz = jnp.dot(h, w_sc[:, COL_V + W_B:COL_V + 2 * W_B], preferred_element_type=F32)
    gz_ref[0] = _silu(rz).astype(BF16)

    rc = jnp.dot(h, w_sc[:, COL_C:COL_C + 4 * W_C], preferred_element_type=F32)
    p_ref[0] = (rc[:, W_C:2 * W_C] * rc[:, 2 * W_C:3 * W_C]).astype(BF16)
    gc_ref[0] = (rc[:, 0:W_C] * _silu(rc[:, 3 * W_C:4 * W_C])).astype(BF16)

    rv = jnp.dot(h, w_sc[:, COL_V:COL_V + W_B], preferred_element_type=F32)
    for hd in range(DIFF_HEADS):
        v_ref[0, hd] = rv[:, hd * HEAD_W:(hd + 1) * HEAD_W].astype(BF16)


def _inproj(xs, mod_l, norm_g, w_in, layer, cos, slo, shi, sgu_g, ws, bs, half_ind):
    bsz, lt, _ = xs.shape
    tm = IN_TILE
    n_t = lt // tm
    norm_blk = pl.BlockSpec((1, 1, MOD_ROWS, 128), lambda b, i: (b, i, 0, 0))
    norm_shape = jax.ShapeDtypeStruct((bsz, n_t, MOD_ROWS, 128), F32)
    tok = lambda width: pl.BlockSpec((1, tm, width), lambda b, i: (b, i, 0))
    head = pl.BlockSpec((1, DIFF_HEADS, tm, HEAD_W), lambda b, i: (b, 0, i, 0))
    const = lambda shape: pl.BlockSpec(shape, lambda b, i: (0,) * len(shape))
    tab = pl.BlockSpec((tm, HEAD_W), lambda b, i: (i, 0))
    tok_shape = lambda width: jax.ShapeDtypeStruct((bsz, lt, width), BF16)
    head_shape = jax.ShapeDtypeStruct((bsz, DIFF_HEADS, lt, HEAD_W), BF16)
    return pl.pallas_call(
        functools.partial(_inproj_kernel, tm=tm, lat=lt - CTX_LEN),
        grid=(bsz, lt // tm),
        in_specs=[
            tok(D_MODEL),
            pl.BlockSpec((1, MOD_ROWS, D_MODEL), lambda b, i: (b, 0, 0)),
            const((1, D_MODEL)),
            pl.BlockSpec((1, D_MODEL, D_IN), lambda b, i: (layer, 0, 0)),
            tab, tab, tab,
            const((1, W_A)),
            const((GMLP_HEADS * GMLP_CHUNK, GMLP_CHUNK)),
            const((GMLP_CHUNK, W_A)),
            const((W_B, 128)),
        ],
        out_specs=[tok(W_A), head, head, head, tok(W_B), tok(W_C), tok(W_C), norm_blk, norm_blk],
        out_shape=[tok_shape(W_A), head_shape, head_shape, head_shape,
                   tok_shape(W_B), tok_shape(W_C), tok_shape(W_C), norm_shape, norm_shape],
        scratch_shapes=[pltpu.VMEM((D_MODEL, D_IN), BF16)],
        compiler_params=pltpu.CompilerParams(
            dimension_semantics=("arbitrary", "arbitrary"),
            vmem_limit_bytes=VMEM_LIMIT_BYTES),
        name="in_projection",
    )(xs, mod_l, norm_g, w_in, cos, slo, shi, sgu_g, ws, bs, half_ind)


NT_DIMS = (((1,), (1,)), ((), ()))
NORM_SLACK = 1.02
MAX_SCORE_BOUND = 50.0
GZ_ARG = 7


def _widen_values(v_sc, v_ref):
    v_sc[:, 0:HEAD_W] = v_ref[0, 0]
    v_sc[:, HEAD_W:2 * HEAD_W] = jnp.ones((v_sc.shape[0], HEAD_W), BF16)


def _stack_q(qz_sc, q_ref, tq):
    q = q_ref[0, 0]
    lane = lax.broadcasted_iota(jnp.int32, (tq, HEAD_W), 1)
    zero = jnp.zeros_like(q)
    qz_sc[0:tq] = jnp.where(lane < DIFF_HEAD_DIM, q, zero)
    qz_sc[tq:2 * tq] = jnp.where(lane >= DIFF_HEAD_DIM, q, zero)
    return qz_sc[...]


def _finish_head(acc, lam_refs, sg_ref, gz_ref, o_ref, tq, lam_init):
    lq1_ref, lk1_ref, lq2_ref, lk2_ref = lam_refs
    o = acc[:, 0:HEAD_W] / acc[:, HEAD_W:2 * HEAD_W]
    lam = (jnp.exp(jnp.sum(lq1_ref[...] * lk1_ref[...]))
           - jnp.exp(jnp.sum(lq2_ref[...] * lk2_ref[...])) + lam_init)
    o = o[0:tq] - lam * o[tq:2 * tq]
    o = o * lax.rsqrt(jnp.mean(o * o, axis=-1, keepdims=True) + EPS) * sg_ref[...]
    o_ref[0] = (o * (1.0 - lam_init) * gz_ref[0].astype(F32)).astype(BF16)


def _pipeline_maps(bsz, n_q):
    n_tiles = bsz * DIFF_HEADS * n_q

    def score_tile(g):
        t = jnp.minimum(g, n_tiles - 1)
        return t // (DIFF_HEADS * n_q), (t // n_q) % DIFF_HEADS, t % n_q

    def value_tile(g):
        return score_tile(jnp.maximum(g - 1, 0))

    def score_head(g):
        b, h, _ = score_tile(g)
        return (b, h, 0, 0)

    def score_rows(g):
        b, h, i = score_tile(g)
        return (b, h, i, 0)

    def value_head(g):
        b, h, _ = value_tile(g)
        return (b, h, 0, 0)

    def value_tok(g):
        b, h, i = value_tile(g)
        return (b, i, h)

    return n_tiles, score_head, score_rows, value_head, value_tok


def _attn_bounded_kernel(lq1_ref, lk1_ref, lq2_ref, lk2_ref, q_ref, k_ref, v_ref, gz_ref, sg_ref,
                         kb_ref, o_ref, qz_sc, e0_sc, e1_sc, l0_sc, l1_sc,
                         *, tq, tk, kt, lam_init):
    g = pl.program_id(0)
    n_keys = e0_sc.shape[1]

    @pl.when(g == 0)
    def _():
        e1_sc[...] = jnp.zeros_like(e1_sc)
        l1_sc[...] = jnp.ones_like(l1_sc)

    def step(e_w, l_w, e_r, l_r):
        qz = _stack_q(qz_sc, q_ref, tq)
        qf = qz.astype(F32)
        q_norm = jnp.sqrt(jnp.sum(qf * qf, axis=-1, keepdims=True))
        row = lax.broadcasted_iota(jnp.int32, (2 * tq, 1), 0)
        kb = kb_ref[0, 0]
        bound = q_norm * jnp.where(row < tq, kb[0:1], kb[1:2])
        l = None
        for c in range(n_keys // tk):
            s = lax.dot_general(qz, k_ref[0, 0, c * tk:(c + 1) * tk, :], NT_DIMS,
                                preferred_element_type=F32)
            for t in range(tk // 128):
                e = jnp.exp2(s[:, t * 128:(t + 1) * 128] - bound)
                l = e if l is None else l + e
                e_w[:, c * tk + t * 128:c * tk + (t + 1) * 128] = e.astype(BF16)
        l_w[...] = jnp.broadcast_to(jnp.sum(l, axis=-1, keepdims=True), l_w.shape)

        l_prev = l_r[...]
        lam = (jnp.exp(jnp.sum(lq1_ref[...] * lk1_ref[...]))
               - jnp.exp(jnp.sum(lq2_ref[...] * lk2_ref[...])) + lam_init)
        ratio = (lam * l_prev[0:tq] / l_prev[tq:2 * tq]).astype(BF16)
        ratio = jnp.concatenate([ratio] * (kt // 128), axis=1)
        acc = None
        for c in range(n_keys // kt):
            a = e_r[0:tq, c * kt:(c + 1) * kt] - ratio * e_r[tq:2 * tq, c * kt:(c + 1) * kt]
            vv = v_ref[0, 0, c * kt:(c + 1) * kt, :]
            d = jnp.dot(a, jnp.concatenate([vv, vv], axis=1), preferred_element_type=F32)
            acc = d if acc is None else acc + d
        o = acc[:, 0:HEAD_W] / l_prev[0:tq]
        o = o * lax.rsqrt(jnp.mean(o * o, axis=-1, keepdims=True) + EPS) * sg_ref[...]
        o_ref[0] = (o * (1.0 - lam_init) * gz_ref[0].astype(F32)).astype(BF16)

    @pl.when(g % 2 == 0)
    def _():
        step(e0_sc, l0_sc, e1_sc, l1_sc)

    @pl.when(g % 2 == 1)
    def _():
        step(e1_sc, l1_sc, e0_sc, l0_sc)


def _attention_bounded(q, k, v, gz, lams, subln_g, k_bound, lam_init, *, tk, kt):
    bsz, _, lt, _ = q.shape
    tq = Q_TILE_BOUNDED
    n_q = (lt - CTX_LEN) // tq
    n_tiles, score_head, score_rows, value_head, value_tok = _pipeline_maps(bsz, n_q)
    lam_spec = pl.BlockSpec((1, DIFF_HEAD_DIM), lambda g: (0, 0))
    seq_blk = (1, 1, lt, HEAD_W)
    return pl.pallas_call(
        functools.partial(_attn_bounded_kernel, tq=tq, tk=tk, kt=kt, lam_init=lam_init),
        grid=(n_tiles + 1,),
        in_specs=[
            lam_spec, lam_spec, lam_spec, lam_spec,
            pl.BlockSpec((1, 1, tq, HEAD_W), score_rows),
            pl.BlockSpec(seq_blk, score_head),
            pl.BlockSpec(seq_blk, value_head),
            pl.BlockSpec((1, tq, HEAD_W), value_tok),
            pl.BlockSpec((1, HEAD_W), lambda g: (0, 0)),
            pl.BlockSpec((1, 1, MOD_ROWS, 128), score_head),
        ],
        out_specs=pl.BlockSpec((1, tq, HEAD_W), value_tok),
        out_shape=jax.ShapeDtypeStruct((bsz, lt, W_B), BF16),
        input_output_aliases={GZ_ARG: 0},
        scratch_shapes=[
            pltpu.VMEM((2 * tq, HEAD_W), BF16),
            pltpu.VMEM((2 * tq, lt), BF16),
            pltpu.VMEM((2 * tq, lt), BF16),
            pltpu.VMEM((2 * tq, 128), F32),
            pltpu.VMEM((2 * tq, 128), F32),
        ],
        compiler_params=pltpu.CompilerParams(
            dimension_semantics=("arbitrary",),
            vmem_limit_bytes=VMEM_LIMIT_BYTES),
        name="diff_attention_bounded",
    )(*lams, q, k, v, gz, subln_g, k_bound)


def _attn_pipe_kernel(lq1_ref, lk1_ref, lq2_ref, lk2_ref, q_ref, k_ref, v_ref, gz_ref, sg_ref,
                      o_ref, qz_sc, v_sc, s0_sc, s1_sc, m0_sc, m1_sc,
                      *, tq, tk, kt, n_q, lam_init):
    g = pl.program_id(0)
    n_keys = s0_sc.shape[1]

    @pl.when(g == 0)
    def _():
        s1_sc[...] = jnp.zeros_like(s1_sc)
        m1_sc[...] = jnp.zeros_like(m1_sc)

    @pl.when((g == 0) | ((g - 1) % n_q == 0))
    def _():
        _widen_values(v_sc, v_ref)

    def step(s_w, m_w, s_r, m_r):
        qz = _stack_q(qz_sc, q_ref, tq)
        m = None
        for c in range(n_keys // tk):
            s = lax.dot_general(qz, k_ref[0, 0, c * tk:(c + 1) * tk, :], NT_DIMS,
                                preferred_element_type=F32)
            s_w[:, c * tk:(c + 1) * tk] = s
            for t in range(tk // 128):
                blk = s[:, t * 128:(t + 1) * 128]
                m = blk if m is None else jnp.maximum(m, blk)
        m_w[...] = jnp.broadcast_to(jnp.max(m, axis=-1, keepdims=True), m_w.shape)

        m_prev = jnp.concatenate([m_r[...]] * (kt // 128), axis=1)
        acc = None
        for c in range(n_keys // kt):
            p = jnp.exp2(s_r[:, c * kt:(c + 1) * kt] - m_prev).astype(BF16)
            d = jnp.dot(p, v_sc[c * kt:(c + 1) * kt, :], preferred_element_type=F32)
            acc = d if acc is None else acc + d
        _finish_head(acc, (lq1_ref, lk1_ref, lq2_ref, lk2_ref), sg_ref, gz_ref, o_ref,
                     tq, lam_init)

    @pl.when(g % 2 == 0)
    def _():
        step(s0_sc, m0_sc, s1_sc, m1_sc)

    @pl.when(g % 2 == 1)
    def _():
        step(s1_sc, m1_sc, s0_sc, m0_sc)


def _attention_pipelined(q, k, v, gz, lams, subln_g, lam_init, *, tk, kt):
    bsz, _, lt, _ = q.shape
    tq = Q_TILE
    n_q = (lt - CTX_LEN) // tq
    n_tiles, score_head, score_rows, value_head, value_tok = _pipeline_maps(bsz, n_q)
    lam_spec = pl.BlockSpec((1, DIFF_HEAD_DIM), lambda g: (0, 0))
    seq_blk = (1, 1, lt, HEAD_W)
    return pl.pallas_call(
        functools.partial(_attn_pipe_kernel, tq=tq, tk=tk, kt=kt, n_q=n_q, lam_init=lam_init),
        grid=(n_tiles + 1,),
        in_specs=[
            lam_spec, lam_spec, lam_spec, lam_spec,
            pl.BlockSpec((1, 1, tq, HEAD_W), score_rows),
            pl.BlockSpec(seq_blk, score_head),
            pl.BlockSpec(seq_blk, value_head),
            pl.BlockSpec((1, tq, HEAD_W), value_tok),
            pl.BlockSpec((1, HEAD_W), lambda g: (0, 0)),
        ],
        out_specs=pl.BlockSpec((1, tq, HEAD_W), value_tok),
        out_shape=jax.ShapeDtypeStruct((bsz, lt, W_B), BF16),
        input_output_aliases={GZ_ARG: 0},
        scratch_shapes=[
            pltpu.VMEM((2 * tq, HEAD_W), BF16),
            pltpu.VMEM((lt, 2 * HEAD_W), BF16),
            pltpu.VMEM((2 * tq, lt), F32),
            pltpu.VMEM((2 * tq, lt), F32),
            pltpu.VMEM((2 * tq, 128), F32),
            pltpu.VMEM((2 * tq, 128), F32),
        ],
        compiler_params=pltpu.CompilerParams(
            dimension_semantics=("arbitrary",),
            vmem_limit_bytes=VMEM_LIMIT_BYTES),
        name="diff_attention_latent",
    )(*lams, q, k, v, gz, subln_g)


def _attn_ctx_kernel(lq1_ref, lk1_ref, lq2_ref, lk2_ref, q_ref, k_ref, v_ref, gz_ref, sg_ref,
                     o_ref, qz_sc, v_sc, *, tq, lam_init):
    _widen_values(v_sc, v_ref)
    qz = _stack_q(qz_sc, q_ref, tq)
    s = lax.dot_general(qz, k_ref[0, 0], NT_DIMS, preferred_element_type=F32)
    p = jnp.exp2(s - jnp.max(s, axis=-1, keepdims=True)).astype(BF16)
    acc = jnp.dot(p, v_sc[...], preferred_element_type=F32)
    _finish_head(acc, (lq1_ref, lk1_ref, lq2_ref, lk2_ref), sg_ref, gz_ref, o_ref, tq, lam_init)


def _attention_ctx(q, k, v, gz, lams, subln_g, lam_init):
    bsz, _, lt, _ = q.shape
    tq = CTX_LEN
    blk = (lt - CTX_LEN) // tq
    lam_spec = pl.BlockSpec((1, DIFF_HEAD_DIM), lambda b, h: (0, 0))
    head_blk = pl.BlockSpec((1, 1, tq, HEAD_W), lambda b, h: (b, h, blk, 0))
    tok_blk = pl.BlockSpec((1, tq, HEAD_W), lambda b, h: (b, blk, h))
    return pl.pallas_call(
        functools.partial(_attn_ctx_kernel, tq=tq, lam_init=lam_init),
        grid=(bsz, DIFF_HEADS),
        in_specs=[lam_spec, lam_spec, lam_spec, lam_spec, head_blk, head_blk, head_blk, tok_blk,
                  pl.BlockSpec((1, HEAD_W), lambda b, h: (0, 0))],
        out_specs=tok_blk,
        out_shape=jax.ShapeDtypeStruct((bsz, lt, W_B), BF16),
        input_output_aliases={GZ_ARG: 0},
        scratch_shapes=[
            pltpu.VMEM((2 * tq, HEAD_W), BF16),
            pltpu.VMEM((tq, 2 * HEAD_W), BF16),
        ],
        compiler_params=pltpu.CompilerParams(
            dimension_semantics=("arbitrary", "arbitrary"),
            vmem_limit_bytes=VMEM_LIMIT_BYTES),
        name="diff_attention_ctx",
    )(*lams, q, k, v, gz, subln_g)


def _outproj_kernel(x_ref, mod_ref, ya_ref, yb_ref, p_ref, pprev_ref, pnext_ref, gc_ref,
                    cw_ref, cb_ref, w_ref, fg_ref, o_ref, w_sc, *, tm, lat, lt, final):
    i = pl.program_id(1)

    @pl.when((pl.program_id(0) == 0) & (i == 0))
    def _():
        w_sc[...] = w_ref[0].astype(BF16)

    row = i * tm + lax.broadcasted_iota(jnp.int32, (tm, 1), 0)
    local = lax.broadcasted_iota(jnp.int32, (tm, 1), 0)
    p = p_ref[0].astype(F32)
    prev_row = pprev_ref[0, HALO - 1:HALO, :].astype(F32)
    next_row = pnext_ref[0, 0:1, :].astype(F32)
    up = jnp.where(local == 0, prev_row, pltpu.roll(p, 1, 0))
    dn = jnp.where(local == tm - 1, next_row, pltpu.roll(p, tm - 1, 0))
    up = jnp.where((row == 0) | (row == lat), 0.0, up)
    dn = jnp.where((row == lat - 1) | (row == lt - 1), 0.0, dn)
    cw = cw_ref[...]
    conv = up * cw[0:1] + p * cw[1:2] + dn * cw[2:3] + cb_ref[...]
    yc = (gc_ref[0].astype(F32) * conv).astype(BF16)

    y = jnp.dot(ya_ref[0], w_sc[0:W_A], preferred_element_type=F32)
    y = y + jnp.dot(yb_ref[0], w_sc[W_A:W_A + W_B], preferred_element_type=F32)
    y = y + jnp.dot(yc, w_sc[W_A + W_B:D_MIX], preferred_element_type=F32)

    mod = mod_ref[0]
    gate = jnp.where(row >= lat, mod[5:6], mod[2:3])
    x = x_ref[0] + gate * y
    if final:
        x = x * lax.rsqrt(jnp.mean(x * x, axis=-1, keepdims=True) + EPS) * fg_ref[...]
    o_ref[0] = x


def _outproj(xs, mod_l, ya, yb, p, gc, conv_w, conv_b, w_out, layer, final_g, final):
    bsz, lt, _ = xs.shape
    lat = lt - CTX_LEN
    tm = FINAL_OUT_TILE if final else OUT_TILE
    n_t = (lat if final else lt) // tm
    hpt = tm // HALO
    n_halo = lt // HALO
    tok = lambda width: pl.BlockSpec((1, tm, width), lambda b, i: (b, i, 0))
    const = lambda shape: pl.BlockSpec(shape, lambda b, i: (0,) * len(shape))
    prev = pl.BlockSpec((1, HALO, W_C),
                        lambda b, i: (b, jnp.maximum(i * hpt - 1, 0), 0))
    nxt = pl.BlockSpec((1, HALO, W_C),
                       lambda b, i: (b, jnp.minimum((i + 1) * hpt, n_halo - 1), 0))
    out_rows = n_t * tm
    return pl.pallas_call(
        functools.partial(_outproj_kernel, tm=tm, lat=lat, lt=lt, final=final),
        grid=(bsz, n_t),
        in_specs=[
            tok(D_MODEL),
            pl.BlockSpec((1, MOD_ROWS, D_MODEL), lambda b, i: (b, 0, 0)),
            tok(W_A), tok(W_B), tok(W_C), prev, nxt, tok(W_C),
            const((3, W_C)), const((1, W_C)),
            pl.BlockSpec((1, D_MIX, D_MODEL), lambda b, i: (layer, 0, 0)), const((1, D_MODEL)),
        ],
        out_specs=pl.BlockSpec((1, tm, D_MODEL), lambda b, i: (b, i, 0)),
        out_shape=jax.ShapeDtypeStruct((bsz, out_rows, D_MODEL), F32),
        scratch_shapes=[pltpu.VMEM((D_MIX, D_MODEL), BF16)],
        compiler_params=pltpu.CompilerParams(
            dimension_semantics=("arbitrary", "arbitrary"),
            vmem_limit_bytes=VMEM_LIMIT_BYTES),
        name="out_projection",
    )(xs, mod_l, ya, yb, p, p, p, gc, conv_w, conv_b, w_out, final_g)


def _rope_tables(length):
    rows = length // GRID_W
    t = jnp.arange(rows * GRID_W)
    pos = jnp.stack([t // GRID_W, t % GRID_W], axis=1).astype(F32)
    inv = ROPE_BASE ** (-jnp.arange(ROPE_PAIRS, dtype=F32) / ROPE_PAIRS)
    ang = pos[:, :, None] * inv
    lane = jnp.arange(HEAD_W)
    axis = (lane % DIFF_HEAD_DIM) // (2 * ROPE_PAIRS)
    pair = lane % ROPE_PAIRS
    second = (lane % (2 * ROPE_PAIRS)) >= ROPE_PAIRS
    cos = jnp.cos(ang)[:, axis, pair]
    sin = jnp.sin(ang)[:, axis, pair]
    sin_lo = jnp.where(second, 0.0, -sin)
    sin_hi = jnp.where(second, sin, 0.0)
    ident = jnp.ones((CTX_LEN, HEAD_W), F32)
    zeros = jnp.zeros((CTX_LEN, HEAD_W), F32)
    return (jnp.concatenate([cos, ident]), jnp.concatenate([sin_lo, zeros]),
            jnp.concatenate([sin_hi, zeros]))


def kernel(x, c, ctx, c_ctx, w_mod, b_mod, norm_g, w_in, w_out, sgu_norm_g, sgu_w, sgu_b,
           lambda_q1, lambda_k1, lambda_q2, lambda_k2, subln_g, conv_w, conv_b, final_g):
    bsz, length, _ = x.shape
    assert ctx.shape[1] == CTX_LEN and bsz + 1 <= MOD_ROWS
    lt = CTX_LEN + length
    assert lt % IN_TILE == 0 and lt % OUT_TILE == 0
    assert length % FINAL_OUT_TILE == 0 and length % Q_TILE == 0 and length % Q_TILE_BOUNDED == 0
    assert length % CTX_LEN == 0 and lt % K_TILE == 0

    cond = jnp.zeros((MOD_ROWS, D_MODEL), F32).at[:bsz].set(c).at[bsz].set(c_ctx)
    mod = _modulation(cond, w_mod, b_mod)
    mod_b = jnp.transpose(mod[:, :, :bsz], (0, 2, 1, 3))
    mod_c = jnp.broadcast_to(mod[:, None, :, bsz], (DEPTH, bsz, 3, D_MODEL))
    pad = jnp.zeros((DEPTH, bsz, MOD_ROWS - 6, D_MODEL), F32)
    mod_rows = jnp.concatenate([mod_b, mod_c, pad], axis=2)

    cos, slo, shi = _rope_tables(length)
    ws = sgu_w.reshape(DEPTH, GMLP_HEADS * GMLP_CHUNK, GMLP_CHUNK).astype(BF16)
    bs = jnp.repeat(jnp.transpose(sgu_b, (0, 2, 1)), GMLP_HEAD_DIM, axis=2)

    half_ind = (jnp.arange(W_B)[:, None] // DIFF_HEAD_DIM == jnp.arange(128)[None, :]).astype(BF16)
    n_halves = 2 * DIFF_HEADS

    def norm_bounds(n):
        n = jnp.max(n, axis=(1, 2))[:, :n_halves].reshape(bsz, DIFF_HEADS, 2)
        return jnp.sqrt(n) * NORM_SLACK

    xs = jnp.concatenate([x, ctx], axis=1)
    for l in range(DEPTH):
        final = l == DEPTH - 1
        lam_init = 0.8 - 0.6 * math.exp(-0.3 * l)
        ya, q, k, v, gz, p, gc, qn, kn = _inproj(
            xs, mod_rows[l], norm_g[l][None], w_in, l, cos, slo, shi,
            sgu_norm_g[l][None], ws[l], bs[l], half_ind)
        lams = (lambda_q1[l][None], lambda_k1[l][None], lambda_q2[l][None], lambda_k2[l][None])
        yb = _attention_ctx(q, k, v, gz, lams, subln_g[l][None], lam_init)
        q_norm, k_norm = norm_bounds(qn), norm_bounds(kn)
        k_bound = jnp.zeros((bsz, DIFF_HEADS, MOD_ROWS, 128), F32).at[:, :, 0:2, :].set(
            jnp.broadcast_to(k_norm[..., None], (bsz, DIFF_HEADS, 2, 128)))
        bounded = jnp.max(q_norm * k_norm) <= MAX_SCORE_BOUND
        yb = lax.cond(
            bounded,
            lambda q, k, v, yb, kb: _attention_bounded(
                q, k, v, yb, lams, subln_g[l][None], kb, lam_init,
                tk=K_TILE, kt=K_TILE),
            lambda q, k, v, yb, kb: _attention_pipelined(
                q, k, v, yb, lams, subln_g[l][None], lam_init,
                tk=K_TILE, kt=V_TILE),
            q, k, v, yb, k_bound)
        xs = _outproj(xs, mod_rows[l], ya, yb, p, gc, conv_w[l], conv_b[l][None],
                      w_out, l, final_g[None], final)
    return xs
```

```python
import functools
import math

import jax
import jax.numpy as jnp
from jax import lax
from jax.experimental import pallas as pl
from jax.experimental.pallas import tpu as pltpu

D_MODEL = 1024
DEPTH = 4
CTX_LEN = 256
GRID_W = 64

GMLP_HEADS = 4
GMLP_HEAD_DIM = 64
GMLP_CHUNK = 128
W_A = GMLP_HEADS * GMLP_HEAD_DIM
DIFF_HEADS = 4
DIFF_HEAD_DIM = 64
HEAD_W = 2 * DIFF_HEAD_DIM
W_B = DIFF_HEADS * HEAD_W
W_C = 256
D_MIX = W_A + W_B + W_C
D_IN = 3 * W_A + 4 * W_B + 4 * W_C

COL_A = 0
COL_Q = 3 * W_A
COL_K = COL_Q + W_B
COL_V = COL_K + W_B
COL_C = COL_V + 2 * W_B

ROPE_BASE = 10000.0
ROPE_PAIRS = DIFF_HEAD_DIM // 4
EPS = 1e-6

F32 = jnp.float32
BF16 = jnp.bfloat16

VMEM_LIMIT_BYTES = 56 * 1024 * 1024
MOD_ROWS = 8

IN_TILE = 768
MID_TILE = 384
FINAL_OUT_TILE = 512
Q_TILE = 256
Q_TILE_BOUNDED = 512
K_TILE = 2816
V_TILE = 256
HALO = 16


def _silu(x):
    return x * jax.nn.sigmoid(x)


def _mod_kernel(cond_ref, w_ref, b_ref, o_ref):
    a = _silu(cond_ref[...])
    o_ref[0, 0] = jnp.dot(a, w_ref[0], preferred_element_type=F32) + b_ref[0, 0]


def _modulation(cond, w_mod, b_mod):
    return pl.pallas_call(
        _mod_kernel,
        grid=(DEPTH, 3),
        in_specs=[
            pl.BlockSpec((MOD_ROWS, D_MODEL), lambda l, j: (0, 0)),
            pl.BlockSpec((1, D_MODEL, D_MODEL), lambda l, j: (l, 0, j)),
            pl.BlockSpec((1, 1, 1, D_MODEL), lambda l, j: (l, j, 0, 0)),
        ],
        out_specs=pl.BlockSpec((1, 1, MOD_ROWS, D_MODEL), lambda l, j: (l, j, 0, 0)),
        out_shape=jax.ShapeDtypeStruct((DEPTH, 3, MOD_ROWS, D_MODEL), F32),
        compiler_params=pltpu.CompilerParams(
            dimension_semantics=("arbitrary", "arbitrary"),
            vmem_limit_bytes=VMEM_LIMIT_BYTES),
        name="adaln_modulation",
    )(cond, w_mod, b_mod.reshape(DEPTH, 3, 1, D_MODEL))


def _rope(t, cos, sin_lo, sin_hi):
    return (t * cos + pltpu.roll(t, HEAD_W - ROPE_PAIRS, 1) * sin_lo
            + pltpu.roll(t, ROPE_PAIRS, 1) * sin_hi)


def _cast_weights_once(w_sc, w_ref):
    @pl.when((pl.program_id(0) == 0) & (pl.program_id(1) == 0))
    def _():
        w_sc[...] = w_ref[0].astype(BF16)


N_INPROJ_IN = 10
N_INPROJ_OUT = 9


def _inproj_kernel(x_ref, *refs, tm, lat):
    _cast_weights_once(refs[-1], refs[2])
    _inproj_body(x_ref[0], *refs, tm=tm, lat=lat)


def _inproj_body(x, mod_ref, g_ref, w_ref, cos_ref, slo_ref, shi_ref,
                 sgug_ref, ws_ref, bs_ref, half_ref,
                 ya_ref, q_ref, k_ref, v_ref, gz_ref, p_ref, gc_ref, qn_ref, kn_ref,
                 w_sc, *, tm, lat):
    del w_ref
    i = pl.program_id(1)
    row = i * tm + lax.broadcasted_iota(jnp.int32, (tm, 1), 0)
    is_ctx = row >= lat
    mod = mod_ref[0]
    shift = jnp.where(is_ctx, mod[3:4], mod[0:1])
    scale = jnp.where(is_ctx, mod[4:5], mod[1:2])
    h = x * lax.rsqrt(jnp.mean(x * x, axis=-1, keepdims=True) + EPS) * g_ref[...]
    h = (h * (1.0 + scale) + shift).astype(BF16)

    ra = jnp.dot(h, w_sc[:, COL_A:COL_A + 3 * W_A], preferred_element_type=F32)
    u = jax.nn.gelu(ra[:, 0:W_A])
    v = jax.nn.gelu(ra[:, W_A:2 * W_A])
    gate_a = _silu(ra[:, 2 * W_A:3 * W_A])
    vn = v * lax.rsqrt(jnp.mean(v * v, axis=-1, keepdims=True) + EPS) * sgug_ref[...]
    vn = vn.astype(BF16)
    lane = lax.broadcasted_iota(jnp.int32, (GMLP_CHUNK, W_A), 1)
    for c in range(tm // GMLP_CHUNK):
        rows = slice(c * GMLP_CHUNK, (c + 1) * GMLP_CHUNK)
        r = jnp.dot(ws_ref[...], vn[rows], preferred_element_type=F32)
        mixed = r[0:GMLP_CHUNK]
        for hd in range(1, GMLP_HEADS):
            mixed = jnp.where(lane >= hd * GMLP_HEAD_DIM,
                              r[hd * GMLP_CHUNK:(hd + 1) * GMLP_CHUNK], mixed)
        mixed = mixed + bs_ref[...]
        ya_ref[0, rows, :] = (u[rows] * mixed * gate_a[rows]).astype(BF16)

    cos, slo, shi = cos_ref[...], slo_ref[...], shi_ref[...]
    rq = jnp.dot(h, w_sc[:, COL_Q:COL_Q + W_B], preferred_element_type=F32)
    rk = jnp.dot(h, w_sc[:, COL_K:COL_K + W_B], preferred_element_type=F32)
    q_scale = math.log2(math.e) / math.sqrt(DIFF_HEAD_DIM)
    q_heads, k_heads = [], []
    for hd in range(DIFF_HEADS):
        cols = slice(hd * HEAD_W, (hd + 1) * HEAD_W)
        q_heads.append(_rope(rq[:, cols], cos, slo, shi) * q_scale)
        k_heads.append(_rope(rk[:, cols], cos, slo, shi))
        q_ref[0, hd] = q_heads[hd].astype(BF16)
        k_ref[0, hd] = k_heads[hd].astype(BF16)
    for heads, n_ref in ((q_heads, qn_ref), (k_heads, kn_ref)):
        t = jnp.concatenate(heads, axis=1)
        n = jnp.dot((t * t).astype(BF16), half_ref[...], preferred_element_type=F32)
        best = n[0:MOD_ROWS]
        for r in range(1, tm // MOD_ROWS):
            best = jnp.maximum(best, n[r * MOD_ROWS:(r + 1) * MOD_ROWS])
        n_ref[0, 0] = best
    rz = jnp.dot(h, w_sc[:, COL_V + W_B:COL_V + 2 * W_B], preferred_element_type=F32)
    gz_ref[0] = _silu(rz).astype(BF16)

    rc = jnp.dot(h, w_sc[:, COL_C:COL_C + 4 * W_C], preferred_element_type=F32)
    p_ref[0] = (rc[:, W_C:2 * W_C] * rc[:, 2 * W_C:3 * W_C]).astype(BF16)
    gc_ref[0] = (rc[:, 0:W_C] * _silu(rc[:, 3 * W_C:4 * W_C])).astype(BF16)

    rv = jnp.dot(h, w_sc[:, COL_V:COL_V + W_B], preferred_element_type=F32)
    for hd in range(DIFF_HEADS):
        v_ref[0, hd] = rv[:, hd * HEAD_W:(hd + 1) * HEAD_W].astype(BF16)


def _const_spec(shape):
    return pl.BlockSpec(shape, lambda b, i: (0,) * len(shape))


def _tok_spec(tm, width):
    return pl.BlockSpec((1, tm, width), lambda b, i: (b, i, 0))


def _inproj_specs(bsz, lt, tm, layer):
    norm_blk = pl.BlockSpec((1, 1, MOD_ROWS, 128), lambda b, i: (b, i, 0, 0))
    norm_shape = jax.ShapeDtypeStruct((bsz, lt // tm, MOD_ROWS, 128), F32)
    head = pl.BlockSpec((1, DIFF_HEADS, tm, HEAD_W), lambda b, i: (b, 0, i, 0))
    tab = pl.BlockSpec((tm, HEAD_W), lambda b, i: (i, 0))
    tok_shape = lambda width: jax.ShapeDtypeStruct((bsz, lt, width), BF16)
    head_shape = jax.ShapeDtypeStruct((bsz, DIFF_HEADS, lt, HEAD_W), BF16)
    in_specs = [
        pl.BlockSpec((1, MOD_ROWS, D_MODEL), lambda b, i: (b, 0, 0)),
        _const_spec((1, D_MODEL)),
        pl.BlockSpec((1, D_MODEL, D_IN), lambda b, i: (layer, 0, 0)),
        tab, tab, tab,
        _const_spec((1, W_A)),
        _const_spec((GMLP_HEADS * GMLP_CHUNK, GMLP_CHUNK)),
        _const_spec((GMLP_CHUNK, W_A)),
        _const_spec((W_B, 128)),
    ]
    out_specs = [_tok_spec(tm, W_A), head, head, head, _tok_spec(tm, W_B), _tok_spec(tm, W_C),
                 _tok_spec(tm, W_C), norm_blk, norm_blk]
    out_shape = [tok_shape(W_A), head_shape, head_shape, head_shape,
                 tok_shape(W_B), tok_shape(W_C), tok_shape(W_C), norm_shape, norm_shape]
    assert len(in_specs) == N_INPROJ_IN and len(out_specs) == N_INPROJ_OUT
    return in_specs, out_specs, out_shape


def _inproj(xs, mod_l, norm_g, w_in, layer, cos, slo, shi, sgu_g, ws, bs, half_ind):
    bsz, lt, _ = xs.shape
    tm = IN_TILE
    in_specs, out_specs, out_shape = _inproj_specs(bsz, lt, tm, layer)
    return pl.pallas_call(
        functools.partial(_inproj_kernel, tm=tm, lat=lt - CTX_LEN),
        grid=(bsz, lt // tm),
        in_specs=[_tok_spec(tm, D_MODEL)] + in_specs,
        out_specs=out_specs,
        out_shape=out_shape,
        scratch_shapes=[pltpu.VMEM((D_MODEL, D_IN), BF16)],
        compiler_params=pltpu.CompilerParams(
            dimension_semantics=("arbitrary", "arbitrary"),
            vmem_limit_bytes=VMEM_LIMIT_BYTES),
        name="in_projection",
    )(xs, mod_l, norm_g, w_in, cos, slo, shi, sgu_g, ws, bs, half_ind)


NT_DIMS = (((1,), (1,)), ((), ()))
NORM_SLACK = 1.02
MAX_SCORE_BOUND = 50.0
GZ_ARG = 7


def _widen_values(v_sc, v_ref):
    v_sc[:, 0:HEAD_W] = v_ref[0, 0]
    v_sc[:, HEAD_W:2 * HEAD_W] = jnp.ones((v_sc.shape[0], HEAD_W), BF16)


def _stack_q(qz_sc, q_ref, tq):
    q = q_ref[0, 0]
    lane = lax.broadcasted_iota(jnp.int32, (tq, HEAD_W), 1)
    zero = jnp.zeros_like(q)
    qz_sc[0:tq] = jnp.where(lane < DIFF_HEAD_DIM, q, zero)
    qz_sc[tq:2 * tq] = jnp.where(lane >= DIFF_HEAD_DIM, q, zero)
    return qz_sc[...]


def _finish_head(acc, lam_refs, sg_ref, gz_ref, o_ref, tq, lam_init):
    lq1_ref, lk1_ref, lq2_ref, lk2_ref = lam_refs
    o = acc[:, 0:HEAD_W] / acc[:, HEAD_W:2 * HEAD_W]
    lam = (jnp.exp(jnp.sum(lq1_ref[...] * lk1_ref[...]))
           - jnp.exp(jnp.sum(lq2_ref[...] * lk2_ref[...])) + lam_init)
    o = o[0:tq] - lam * o[tq:2 * tq]
    o = o * lax.rsqrt(jnp.mean(o * o, axis=-1, keepdims=True) + EPS) * sg_ref[...]
    o_ref[0] = (o * (1.0 - lam_init) * gz_ref[0].astype(F32)).astype(BF16)


def _pipeline_maps(bsz, n_q):
    n_tiles = bsz * DIFF_HEADS * n_q

    def score_tile(g):
        t = jnp.minimum(g, n_tiles - 1)
        return t // (DIFF_HEADS * n_q), (t // n_q) % DIFF_HEADS, t % n_q

    def value_tile(g):
        return score_tile(jnp.maximum(g - 1, 0))

    def score_head(g):
        b, h, _ = score_tile(g)
        return (b, h, 0, 0)

    def score_rows(g):
        b, h, i = score_tile(g)
        return (b, h, i, 0)

    def value_head(g):
        b, h, _ = value_tile(g)
        return (b, h, 0, 0)

    def value_tok(g):
        b, h, i = value_tile(g)
        return (b, i, h)

    return n_tiles, score_head, score_rows, value_head, value_tok


def _attn_bounded_kernel(lq1_ref, lk1_ref, lq2_ref, lk2_ref, q_ref, k_ref, v_ref, gz_ref, sg_ref,
                         kb_ref, o_ref, qz_sc, e0_sc, e1_sc, l0_sc, l1_sc,
                         *, tq, tk, kt, lam_init):
    g = pl.program_id(0)
    n_keys = e0_sc.shape[1]

    @pl.when(g == 0)
    def _():
        e1_sc[...] = jnp.zeros_like(e1_sc)
        l1_sc[...] = jnp.ones_like(l1_sc)

    def step(e_w, l_w, e_r, l_r):
        qz = _stack_q(qz_sc, q_ref, tq)
        qf = qz.astype(F32)
        q_norm = jnp.sqrt(jnp.sum(qf * qf, axis=-1, keepdims=True))
        row = lax.broadcasted_iota(jnp.int32, (2 * tq, 1), 0)
        kb = kb_ref[0, 0]
        bound = q_norm * jnp.where(row < tq, kb[0:1], kb[1:2])
        l = None
        for c in range(n_keys // tk):
            s = lax.dot_general(qz, k_ref[0, 0, c * tk:(c + 1) * tk, :], NT_DIMS,
                                preferred_element_type=F32)
            for t in range(tk // 128):
                e = jnp.exp2(s[:, t * 128:(t + 1) * 128] - bound)
                l = e if l is None else l + e
                e_w[:, c * tk + t * 128:c * tk + (t + 1) * 128] = e.astype(BF16)
        l_w[...] = jnp.broadcast_to(jnp.sum(l, axis=-1, keepdims=True), l_w.shape)

        l_prev = l_r[...]
        lam = (jnp.exp(jnp.sum(lq1_ref[...] * lk1_ref[...]))
               - jnp.exp(jnp.sum(lq2_ref[...] * lk2_ref[...])) + lam_init)
        ratio = (lam * l_prev[0:tq] / l_prev[tq:2 * tq]).astype(BF16)
        ratio = jnp.concatenate([ratio] * (kt // 128), axis=1)
        acc = None
        for c in range(n_keys // kt):
            a = e_r[0:tq, c * kt:(c + 1) * kt] - ratio * e_r[tq:2 * tq, c * kt:(c + 1) * kt]
            vv = v_ref[0, 0, c * kt:(c + 1) * kt, :]
            d = jnp.dot(a, jnp.concatenate([vv, vv], axis=1), preferred_element_type=F32)
            acc = d if acc is None else acc + d
        o = acc[:, 0:HEAD_W] / l_prev[0:tq]
        o = o * lax.rsqrt(jnp.mean(o * o, axis=-1, keepdims=True) + EPS) * sg_ref[...]
        o_ref[0] = (o * (1.0 - lam_init) * gz_ref[0].astype(F32)).astype(BF16)

    @pl.when(g % 2 == 0)
    def _():
        step(e0_sc, l0_sc, e1_sc, l1_sc)

    @pl.when(g % 2 == 1)
    def _():
        step(e1_sc, l1_sc, e0_sc, l0_sc)


def _attention_bounded(q, k, v, gz, lams, subln_g, k_bound, lam_init, *, tk, kt):
    bsz, _, lt, _ = q.shape
    tq = Q_TILE_BOUNDED
    n_q = (lt - CTX_LEN) // tq
    n_tiles, score_head, score_rows, value_head, value_tok = _pipeline_maps(bsz, n_q)
    lam_spec = pl.BlockSpec((1, DIFF_HEAD_DIM), lambda g: (0, 0))
    seq_blk = (1, 1, lt, HEAD_W)
    return pl.pallas_call(
        functools.partial(_attn_bounded_kernel, tq=tq, tk=tk, kt=kt, lam_init=lam_init),
        grid=(n_tiles + 1,),
        in_specs=[
            lam_spec, lam_spec, lam_spec, lam_spec,
            pl.BlockSpec((1, 1, tq, HEAD_W), score_rows),
            pl.BlockSpec(seq_blk, score_head),
            pl.BlockSpec(seq_blk, value_head),
            pl.BlockSpec((1, tq, HEAD_W), value_tok),
            pl.BlockSpec((1, HEAD_W), lambda g: (0, 0)),
            pl.BlockSpec((1, 1, MOD_ROWS, 128), score_head),
        ],
        out_specs=pl.BlockSpec((1, tq, HEAD_W), value_tok),
        out_shape=jax.ShapeDtypeStruct((bsz, lt, W_B), BF16),
        input_output_aliases={GZ_ARG: 0},
        scratch_shapes=[
            pltpu.VMEM((2 * tq, HEAD_W), BF16),
            pltpu.VMEM((2 * tq, lt), BF16),
            pltpu.VMEM((2 * tq, lt), BF16),
            pltpu.VMEM((2 * tq, 128), F32),
            pltpu.VMEM((2 * tq, 128), F32),
        ],
        compiler_params=pltpu.CompilerParams(
            dimension_semantics=("arbitrary",),
            vmem_limit_bytes=VMEM_LIMIT_BYTES),
        name="diff_attention_bounded",
    )(*lams, q, k, v, gz, subln_g, k_bound)


def _attn_pipe_kernel(lq1_ref, lk1_ref, lq2_ref, lk2_ref, q_ref, k_ref, v_ref, gz_ref, sg_ref,
                      o_ref, qz_sc, v_sc, s0_sc, s1_sc, m0_sc, m1_sc,
                      *, tq, tk, kt, n_q, lam_init):
    g = pl.program_id(0)
    n_keys = s0_sc.shape[1]

    @pl.when(g == 0)
    def _():
        s1_sc[...] = jnp.zeros_like(s1_sc)
        m1_sc[...] = jnp.zeros_like(m1_sc)

    @pl.when((g == 0) | ((g - 1) % n_q == 0))
    def _():
        _widen_values(v_sc, v_ref)

    def step(s_w, m_w, s_r, m_r):
        qz = _stack_q(qz_sc, q_ref, tq)
        m = None
        for c in range(n_keys // tk):
            s = lax.dot_general(qz, k_ref[0, 0, c * tk:(c + 1) * tk, :], NT_DIMS,
                                preferred_element_type=F32)
            s_w[:, c * tk:(c + 1) * tk] = s
            for t in range(tk // 128):
                blk = s[:, t * 128:(t + 1) * 128]
                m = blk if m is None else jnp.maximum(m, blk)
        m_w[...] = jnp.broadcast_to(jnp.max(m, axis=-1, keepdims=True), m_w.shape)

        m_prev = jnp.concatenate([m_r[...]] * (kt // 128), axis=1)
        acc = None
        for c in range(n_keys // kt):
            p = jnp.exp2(s_r[:, c * kt:(c + 1) * kt] - m_prev).astype(BF16)
            d = jnp.dot(p, v_sc[c * kt:(c + 1) * kt, :], preferred_element_type=F32)
            acc = d if acc is None else acc + d
        _finish_head(acc, (lq1_ref, lk1_ref, lq2_ref, lk2_ref), sg_ref, gz_ref, o_ref,
                     tq, lam_init)

    @pl.when(g % 2 == 0)
    def _():
        step(s0_sc, m0_sc, s1_sc, m1_sc)

    @pl.when(g % 2 == 1)
    def _():
        step(s1_sc, m1_sc, s0_sc, m0_sc)


def _attention_pipelined(q, k, v, gz, lams, subln_g, lam_init, *, tk, kt):
    bsz, _, lt, _ = q.shape
    tq = Q_TILE
    n_q = (lt - CTX_LEN) // tq
    n_tiles, score_head, score_rows, value_head, value_tok = _pipeline_maps(bsz, n_q)
    lam_spec = pl.BlockSpec((1, DIFF_HEAD_DIM), lambda g: (0, 0))
    seq_blk = (1, 1, lt, HEAD_W)
    return pl.pallas_call(
        functools.partial(_attn_pipe_kernel, tq=tq, tk=tk, kt=kt, n_q=n_q, lam_init=lam_init),
        grid=(n_tiles + 1,),
        in_specs=[
            lam_spec, lam_spec, lam_spec, lam_spec,
            pl.BlockSpec((1, 1, tq, HEAD_W), score_rows),
            pl.BlockSpec(seq_blk, score_head),
            pl.BlockSpec(seq_blk, value_head),
            pl.BlockSpec((1, tq, HEAD_W), value_tok),
            pl.BlockSpec((1, HEAD_W), lambda g: (0, 0)),
        ],
        out_specs=pl.BlockSpec((1, tq, HEAD_W), value_tok),
        out_shape=jax.ShapeDtypeStruct((bsz, lt, W_B), BF16),
        input_output_aliases={GZ_ARG: 0},
        scratch_shapes=[
            pltpu.VMEM((2 * tq, HEAD_W), BF16),
            pltpu.VMEM((lt, 2 * HEAD_W), BF16),
            pltpu.VMEM((2 * tq, lt), F32),
            pltpu.VMEM((2 * tq, lt), F32),
            pltpu.VMEM((2 * tq, 128), F32),
            pltpu.VMEM((2 * tq, 128), F32),
        ],
        compiler_params=pltpu.CompilerParams(
            dimension_semantics=("arbitrary",),
            vmem_limit_bytes=VMEM_LIMIT_BYTES),
        name="diff_attention_latent",
    )(*lams, q, k, v, gz, subln_g)


def _attn_ctx_kernel(lq1_ref, lk1_ref, lq2_ref, lk2_ref, q_ref, k_ref, v_ref, gz_ref, sg_ref,
                     o_ref, qz_sc, v_sc, *, tq, lam_init):
    _widen_values(v_sc, v_ref)
    qz = _stack_q(qz_sc, q_ref, tq)
    s = lax.dot_general(qz, k_ref[0, 0], NT_DIMS, preferred_element_type=F32)
    p = jnp.exp2(s - jnp.max(s, axis=-1, keepdims=True)).astype(BF16)
    acc = jnp.dot(p, v_sc[...], preferred_element_type=F32)
    _finish_head(acc, (lq1_ref, lk1_ref, lq2_ref, lk2_ref), sg_ref, gz_ref, o_ref, tq, lam_init)


def _attention_ctx(q, k, v, gz, lams, subln_g, lam_init):
    bsz, _, lt, _ = q.shape
    tq = CTX_LEN
    blk = (lt - CTX_LEN) // tq
    lam_spec = pl.BlockSpec((1, DIFF_HEAD_DIM), lambda b, h: (0, 0))
    head_blk = pl.BlockSpec((1, 1, tq, HEAD_W), lambda b, h: (b, h, blk, 0))
    tok_blk = pl.BlockSpec((1, tq, HEAD_W), lambda b, h: (b, blk, h))
    return pl.pallas_call(
        functools.partial(_attn_ctx_kernel, tq=tq, lam_init=lam_init),
        grid=(bsz, DIFF_HEADS),
        in_specs=[lam_spec, lam_spec, lam_spec, lam_spec, head_blk, head_blk, head_blk, tok_blk,
                  pl.BlockSpec((1, HEAD_W), lambda b, h: (0, 0))],
        out_specs=tok_blk,
        out_shape=jax.ShapeDtypeStruct((bsz, lt, W_B), BF16),
        input_output_aliases={GZ_ARG: 0},
        scratch_shapes=[
            pltpu.VMEM((2 * tq, HEAD_W), BF16),
            pltpu.VMEM((tq, 2 * HEAD_W), BF16),
        ],
        compiler_params=pltpu.CompilerParams(
            dimension_semantics=("arbitrary", "arbitrary"),
            vmem_limit_bytes=VMEM_LIMIT_BYTES),
        name="diff_attention_ctx",
    )(*lams, q, k, v, gz, subln_g)


N_OUTPROJ_IN = 12


def _outproj_kernel(*refs, tm, lat, lt, final):
    o_ref, w_sc = refs[N_OUTPROJ_IN:]
    _cast_weights_once(w_sc, refs[10])
    o_ref[0] = _outproj_body(*refs[:N_OUTPROJ_IN], w_sc, tm=tm, lat=lat, lt=lt, final=final)


def _outproj_body(x_ref, mod_ref, ya_ref, yb_ref, p_ref, pprev_ref, pnext_ref, gc_ref,
                  cw_ref, cb_ref, w_ref, fg_ref, w_sc, *, tm, lat, lt, final):
    del w_ref
    i = pl.program_id(1)
    row = i * tm + lax.broadcasted_iota(jnp.int32, (tm, 1), 0)
    local = lax.broadcasted_iota(jnp.int32, (tm, 1), 0)
    p = p_ref[0].astype(F32)
    prev_row = pprev_ref[0, HALO - 1:HALO, :].astype(F32)
    next_row = pnext_ref[0, 0:1, :].astype(F32)
    up = jnp.where(local == 0, prev_row, pltpu.roll(p, 1, 0))
    dn = jnp.where(local == tm - 1, next_row, pltpu.roll(p, tm - 1, 0))
    up = jnp.where((row == 0) | (row == lat), 0.0, up)
    dn = jnp.where((row == lat - 1) | (row == lt - 1), 0.0, dn)
    cw = cw_ref[...]
    conv = up * cw[0:1] + p * cw[1:2] + dn * cw[2:3] + cb_ref[...]
    yc = (gc_ref[0].astype(F32) * conv).astype(BF16)

    y = jnp.dot(ya_ref[0], w_sc[0:W_A], preferred_element_type=F32)
    y = y + jnp.dot(yb_ref[0], w_sc[W_A:W_A + W_B], preferred_element_type=F32)
    y = y + jnp.dot(yc, w_sc[W_A + W_B:D_MIX], preferred_element_type=F32)

    mod = mod_ref[0]
    gate = jnp.where(row >= lat, mod[5:6], mod[2:3])
    x = x_ref[0] + gate * y
    if final:
        x = x * lax.rsqrt(jnp.mean(x * x, axis=-1, keepdims=True) + EPS) * fg_ref[...]
    return x


def _midproj_kernel(*refs, tm, lat, lt):
    out_in = refs[:N_OUTPROJ_IN]
    in_in = refs[N_OUTPROJ_IN:N_OUTPROJ_IN + N_INPROJ_IN]
    x_ref = refs[N_OUTPROJ_IN + N_INPROJ_IN]
    in_out = refs[N_OUTPROJ_IN + N_INPROJ_IN + 1:N_OUTPROJ_IN + N_INPROJ_IN + 1 + N_INPROJ_OUT]
    wo_sc, wi_sc = refs[-2:]
    _cast_weights_once(wo_sc, out_in[10])
    _cast_weights_once(wi_sc, in_in[2])
    x = _outproj_body(*out_in, wo_sc, tm=tm, lat=lat, lt=lt, final=False)
    x_ref[0] = x
    _inproj_body(x, *in_in, *in_out, wi_sc, tm=tm, lat=lat)


def _outproj_specs(lt, tm, layer):
    hpt = tm // HALO
    n_halo = lt // HALO
    prev = pl.BlockSpec((1, HALO, W_C),
                        lambda b, i: (b, jnp.maximum(i * hpt - 1, 0), 0))
    nxt = pl.BlockSpec((1, HALO, W_C),
                       lambda b, i: (b, jnp.minimum((i + 1) * hpt, n_halo - 1), 0))
    in_specs = [
        _tok_spec(tm, D_MODEL),
        pl.BlockSpec((1, MOD_ROWS, D_MODEL), lambda b, i: (b, 0, 0)),
        _tok_spec(tm, W_A), _tok_spec(tm, W_B), _tok_spec(tm, W_C), prev, nxt, _tok_spec(tm, W_C),
        _const_spec((3, W_C)), _const_spec((1, W_C)),
        pl.BlockSpec((1, D_MIX, D_MODEL), lambda b, i: (layer, 0, 0)), _const_spec((1, D_MODEL)),
    ]
    assert len(in_specs) == N_OUTPROJ_IN
    return in_specs


def _outproj_final(xs, mod_l, ya, yb, p, gc, conv_w, conv_b, w_out, layer, final_g):
    bsz, lt, _ = xs.shape
    lat = lt - CTX_LEN
    tm = FINAL_OUT_TILE
    return pl.pallas_call(
        functools.partial(_outproj_kernel, tm=tm, lat=lat, lt=lt, final=True),
        grid=(bsz, lat // tm),
        in_specs=_outproj_specs(lt, tm, layer),
        out_specs=_tok_spec(tm, D_MODEL),
        out_shape=jax.ShapeDtypeStruct((bsz, lat, D_MODEL), F32),
        scratch_shapes=[pltpu.VMEM((D_MIX, D_MODEL), BF16)],
        compiler_params=pltpu.CompilerParams(
            dimension_semantics=("arbitrary", "arbitrary"),
            vmem_limit_bytes=VMEM_LIMIT_BYTES),
        name="out_projection",
    )(xs, mod_l, ya, yb, p, p, p, gc, conv_w, conv_b, w_out, final_g)


def _midproj(xs, out_args, in_args, layer):
    bsz, lt, _ = xs.shape
    tm = MID_TILE
    mod_prev, ya, yb, p, gc, conv_w, conv_b, w_out, final_g = out_args
    in_specs, out_specs, out_shape = _inproj_specs(bsz, lt, tm, layer)
    return pl.pallas_call(
        functools.partial(_midproj_kernel, tm=tm, lat=lt - CTX_LEN, lt=lt),
        grid=(bsz, lt // tm),
        in_specs=_outproj_specs(lt, tm, layer - 1) + in_specs,
        out_specs=[_tok_spec(tm, D_MODEL)] + out_specs,
        out_shape=[jax.ShapeDtypeStruct((bsz, lt, D_MODEL), F32)] + out_shape,
        scratch_shapes=[pltpu.VMEM((D_MIX, D_MODEL), BF16), pltpu.VMEM((D_MODEL, D_IN), BF16)],
        compiler_params=pltpu.CompilerParams(
            dimension_semantics=("arbitrary", "arbitrary"),
            vmem_limit_bytes=VMEM_LIMIT_BYTES),
        name="mid_projection",
    )(xs, mod_prev, ya, yb, p, p, p, gc, conv_w, conv_b, w_out, final_g, *in_args)


def _rope_tables(length):
    rows = length // GRID_W
    t = jnp.arange(rows * GRID_W)
    pos = jnp.stack([t // GRID_W, t % GRID_W], axis=1).astype(F32)
    inv = ROPE_BASE ** (-jnp.arange(ROPE_PAIRS, dtype=F32) / ROPE_PAIRS)
    ang = pos[:, :, None] * inv
    lane = jnp.arange(HEAD_W)
    axis = (lane % DIFF_HEAD_DIM) // (2 * ROPE_PAIRS)
    pair = lane % ROPE_PAIRS
    second = (lane % (2 * ROPE_PAIRS)) >= ROPE_PAIRS
    cos = jnp.cos(ang)[:, axis, pair]
    sin = jnp.sin(ang)[:, axis, pair]
    sin_lo = jnp.where(second, 0.0, -sin)
    sin_hi = jnp.where(second, sin, 0.0)
    ident = jnp.ones((CTX_LEN, HEAD_W), F32)
    zeros = jnp.zeros((CTX_LEN, HEAD_W), F32)
    return (jnp.concatenate([cos, ident]), jnp.concatenate([sin_lo, zeros]),
            jnp.concatenate([sin_hi, zeros]))


def kernel(x, c, ctx, c_ctx, w_mod, b_mod, norm_g, w_in, w_out, sgu_norm_g, sgu_w, sgu_b,
           lambda_q1, lambda_k1, lambda_q2, lambda_k2, subln_g, conv_w, conv_b, final_g):
    bsz, length, _ = x.shape
    assert ctx.shape[1] == CTX_LEN and bsz + 1 <= MOD_ROWS
    lt = CTX_LEN + length
    assert lt % IN_TILE == 0 and lt % MID_TILE == 0
    assert length % FINAL_OUT_TILE == 0 and length % Q_TILE == 0 and length % Q_TILE_BOUNDED == 0
    assert length % CTX_LEN == 0 and lt % K_TILE == 0

    cond = jnp.zeros((MOD_ROWS, D_MODEL), F32).at[:bsz].set(c).at[bsz].set(c_ctx)
    mod = _modulation(cond, w_mod, b_mod)
    mod_b = jnp.transpose(mod[:, :, :bsz], (0, 2, 1, 3))
    mod_c = jnp.broadcast_to(mod[:, None, :, bsz], (DEPTH, bsz, 3, D_MODEL))
    pad = jnp.zeros((DEPTH, bsz, MOD_ROWS - 6, D_MODEL), F32)
    mod_rows = jnp.concatenate([mod_b, mod_c, pad], axis=2)

    cos, slo, shi = _rope_tables(length)
    ws = sgu_w.reshape(DEPTH, GMLP_HEADS * GMLP_CHUNK, GMLP_CHUNK).astype(BF16)
    bs = jnp.repeat(jnp.transpose(sgu_b, (0, 2, 1)), GMLP_HEAD_DIM, axis=2)

    half_ind = (jnp.arange(W_B)[:, None] // DIFF_HEAD_DIM == jnp.arange(128)[None, :]).astype(BF16)
    n_halves = 2 * DIFF_HEADS

    def norm_bounds(n):
        n = jnp.max(n, axis=(1, 2))[:, :n_halves].reshape(bsz, DIFF_HEADS, 2)
        return jnp.sqrt(n) * NORM_SLACK

    def inproj_args(l):
        return (mod_rows[l], norm_g[l][None], w_in, cos, slo, shi,
                sgu_norm_g[l][None], ws[l], bs[l], half_ind)

    xs = jnp.concatenate([x, ctx], axis=1)
    a = inproj_args(0)
    proj = _inproj(xs, *a[:3], 0, *a[3:])
    for l in range(DEPTH):
        lam_init = 0.8 - 0.6 * math.exp(-0.3 * l)
        ya, q, k, v, gz, p, gc, qn, kn = proj
        lams = (lambda_q1[l][None], lambda_k1[l][None], lambda_q2[l][None], lambda_k2[l][None])
        yb = _attention_ctx(q, k, v, gz, lams, subln_g[l][None], lam_init)
        q_norm, k_norm = norm_bounds(qn), norm_bounds(kn)
        k_bound = jnp.zeros((bsz, DIFF_HEADS, MOD_ROWS, 128), F32).at[:, :, 0:2, :].set(
            jnp.broadcast_to(k_norm[..., None], (bsz, DIFF_HEADS, 2, 128)))
        bounded = jnp.max(q_norm * k_norm) <= MAX_SCORE_BOUND
        yb = lax.cond(
            bounded,
            lambda q, k, v, yb, kb: _attention_bounded(
                q, k, v, yb, lams, subln_g[l][None], kb, lam_init,
                tk=K_TILE, kt=K_TILE),
            lambda q, k, v, yb, kb: _attention_pipelined(
                q, k, v, yb, lams, subln_g[l][None], lam_init,
                tk=K_TILE, kt=V_TILE),
            q, k, v, yb, k_bound)
        out_args = (mod_rows[l], ya, yb, p, gc, conv_w[l], conv_b[l][None], w_out, final_g[None])
        if l + 1 < DEPTH:
            xs, *proj = _midproj(xs, out_args, inproj_args(l + 1), l + 1)
        else:
            xs = _outproj_final(xs, *out_args[:-1], l, out_args[-1])
    return xs
```

```python
import functools
import math

import jax
import jax.numpy as jnp
from jax import lax
from jax.experimental import pallas as pl
from jax.experimental.pallas import tpu as pltpu

D_MODEL = 1024
DEPTH = 4
CTX_LEN = 256
GRID_W = 64

GMLP_HEADS = 4
GMLP_HEAD_DIM = 64
GMLP_CHUNK = 128
W_A = GMLP_HEADS * GMLP_HEAD_DIM
DIFF_HEADS = 4
DIFF_HEAD_DIM = 64
HEAD_W = 2 * DIFF_HEAD_DIM
W_B = DIFF_HEADS * HEAD_W
W_C = 256
D_MIX = W_A + W_B + W_C
D_IN = 3 * W_A + 4 * W_B + 4 * W_C

COL_A = 0
COL_Q = 3 * W_A
COL_K = COL_Q + W_B
COL_V = COL_K + W_B
COL_C = COL_V + 2 * W_B

ROPE_BASE = 10000.0
ROPE_PAIRS = DIFF_HEAD_DIM // 4
EPS = 1e-6

F32 = jnp.float32
BF16 = jnp.bfloat16

VMEM_LIMIT_BYTES = 56 * 1024 * 1024
MOD_ROWS = 8
LANES = 128

IN_TILE = 768
MID_TILE = 384
FINAL_OUT_TILE = 512
Q_TILE = 256
Q_TILE_BOUNDED = 512
K_TILE = 2816
V_TILE = 256
HALO = 16


def _silu(x):
    return x * jax.nn.sigmoid(x)


def _mod_kernel(cond_ref, w_ref, b_ref, o_ref):
    a = _silu(cond_ref[...])
    o_ref[0, 0] = jnp.dot(a, w_ref[0], preferred_element_type=F32) + b_ref[0, 0]


def _modulation(cond, w_mod, b_mod):
    return pl.pallas_call(
        _mod_kernel,
        grid=(DEPTH, 3),
        in_specs=[
            pl.BlockSpec((MOD_ROWS, D_MODEL), lambda l, j: (0, 0)),
            pl.BlockSpec((1, D_MODEL, D_MODEL), lambda l, j: (l, 0, j)),
            pl.BlockSpec((1, 1, 1, D_MODEL), lambda l, j: (l, j, 0, 0)),
        ],
        out_specs=pl.BlockSpec((1, 1, MOD_ROWS, D_MODEL), lambda l, j: (l, j, 0, 0)),
        out_shape=jax.ShapeDtypeStruct((DEPTH, 3, MOD_ROWS, D_MODEL), F32),
        compiler_params=pltpu.CompilerParams(
            dimension_semantics=("arbitrary", "arbitrary"),
            vmem_limit_bytes=VMEM_LIMIT_BYTES),
        name="adaln_modulation",
    )(cond, w_mod, b_mod.reshape(DEPTH, 3, 1, D_MODEL))


def _rope(t, cos, sin_lo, sin_hi):
    return (t * cos + pltpu.roll(t, HEAD_W - ROPE_PAIRS, 1) * sin_lo
            + pltpu.roll(t, ROPE_PAIRS, 1) * sin_hi)


def _cast_weights_once(w_sc, w_ref):
    @pl.when((pl.program_id(0) == 0) & (pl.program_id(1) == 0))
    def _():
        w_sc[...] = w_ref[0].astype(BF16)


N_INPROJ_IN = 10
N_INPROJ_OUT = 9


def _inproj_kernel(x_ref, *refs, tm, lat):
    _cast_weights_once(refs[-1], refs[2])
    _inproj_body(x_ref[0], *refs, tm=tm, lat=lat)


def _inproj_body(x, mod_ref, g_ref, w_ref, cos_ref, slo_ref, shi_ref,
                 sgug_ref, ws_ref, bs_ref, half_ref,
                 ya_ref, q_ref, k_ref, v_ref, gz_ref, p_ref, gc_ref, qn_ref, kn_ref,
                 w_sc, *, tm, lat):
    del w_ref
    i = pl.program_id(1)
    row = i * tm + lax.broadcasted_iota(jnp.int32, (tm, 1), 0)
    is_ctx = row >= lat
    mod = mod_ref[0]
    shift = jnp.where(is_ctx, mod[3:4], mod[0:1])
    scale = jnp.where(is_ctx, mod[4:5], mod[1:2])
    h = x * lax.rsqrt(jnp.mean(x * x, axis=-1, keepdims=True) + EPS) * g_ref[...]
    h = (h * (1.0 + scale) + shift).astype(BF16)

    ra = jnp.dot(h, w_sc[:, COL_A:COL_A + 3 * W_A], preferred_element_type=F32)
    u = jax.nn.gelu(ra[:, 0:W_A])
    v = jax.nn.gelu(ra[:, W_A:2 * W_A])
    gate_a = _silu(ra[:, 2 * W_A:3 * W_A])
    vn = v * lax.rsqrt(jnp.mean(v * v, axis=-1, keepdims=True) + EPS) * sgug_ref[...]
    vn = vn.astype(BF16)
    lane = lax.broadcasted_iota(jnp.int32, (GMLP_CHUNK, W_A), 1)
    for c in range(tm // GMLP_CHUNK):
        rows = slice(c * GMLP_CHUNK, (c + 1) * GMLP_CHUNK)
        r = jnp.dot(ws_ref[...], vn[rows], preferred_element_type=F32)
        mixed = r[0:GMLP_CHUNK]
        for hd in range(1, GMLP_HEADS):
            mixed = jnp.where(lane >= hd * GMLP_HEAD_DIM,
                              r[hd * GMLP_CHUNK:(hd + 1) * GMLP_CHUNK], mixed)
        mixed = mixed + bs_ref[...]
        ya_ref[0, rows, :] = (u[rows] * mixed * gate_a[rows]).astype(BF16)

    cos, slo, shi = cos_ref[...], slo_ref[...], shi_ref[...]
    rq = jnp.dot(h, w_sc[:, COL_Q:COL_Q + W_B], preferred_element_type=F32)
    rk = jnp.dot(h, w_sc[:, COL_K:COL_K + W_B], preferred_element_type=F32)
    q_scale = math.log2(math.e) / math.sqrt(DIFF_HEAD_DIM)
    q_heads, k_heads = [], []
    for hd in range(DIFF_HEADS):
        cols = slice(hd * HEAD_W, (hd + 1) * HEAD_W)
        q_heads.append(_rope(rq[:, cols], cos, slo, shi) * q_scale)
        k_heads.append(_rope(rk[:, cols], cos, slo, shi))
        q_ref[0, hd] = q_heads[hd].astype(BF16)
        k_ref[0, hd] = k_heads[hd].astype(BF16)
    for heads, n_ref in ((q_heads, qn_ref), (k_heads, kn_ref)):
        t = jnp.concatenate(heads, axis=1)
        n = jnp.dot((t * t).astype(BF16), half_ref[...], preferred_element_type=F32)
        best = n[0:MOD_ROWS]
        for r in range(1, tm // MOD_ROWS):
            best = jnp.maximum(best, n[r * MOD_ROWS:(r + 1) * MOD_ROWS])
        n_ref[0, 0] = best
    rz = jnp.dot(h, w_sc[:, COL_V + W_B:COL_V + 2 * W_B], preferred_element_type=F32)
    gz_ref[0] = _silu(rz).astype(BF16)

    rc = jnp.dot(h, w_sc[:, COL_C:COL_C + 4 * W_C], preferred_element_type=F32)
    p_ref[0] = (rc[:, W_C:2 * W_C] * rc[:, 2 * W_C:3 * W_C]).astype(BF16)
    gc_ref[0] = (rc[:, 0:W_C] * _silu(rc[:, 3 * W_C:4 * W_C])).astype(BF16)

    rv = jnp.dot(h, w_sc[:, COL_V:COL_V + W_B], preferred_element_type=F32)
    for hd in range(DIFF_HEADS):
        v_ref[0, hd] = rv[:, hd * HEAD_W:(hd + 1) * HEAD_W].astype(BF16)


def _const_spec(shape):
    return pl.BlockSpec(shape, lambda b, i: (0,) * len(shape))


def _tok_spec(tm, width):
    return pl.BlockSpec((1, tm, width), lambda b, i: (b, i, 0))


def _inproj_specs(bsz, lt, tm, layer):
    norm_blk = pl.BlockSpec((1, 1, MOD_ROWS, LANES), lambda b, i: (b, i, 0, 0))
    norm_shape = jax.ShapeDtypeStruct((bsz, lt // tm, MOD_ROWS, LANES), F32)
    head = pl.BlockSpec((1, DIFF_HEADS, tm, HEAD_W), lambda b, i: (b, 0, i, 0))
    tab = pl.BlockSpec((tm, HEAD_W), lambda b, i: (i, 0))
    tok_shape = lambda width: jax.ShapeDtypeStruct((bsz, lt, width), BF16)
    head_shape = jax.ShapeDtypeStruct((bsz, DIFF_HEADS, lt, HEAD_W), BF16)
    in_specs = [
        pl.BlockSpec((1, MOD_ROWS, D_MODEL), lambda b, i: (b, 0, 0)),
        _const_spec((1, D_MODEL)),
        pl.BlockSpec((1, D_MODEL, D_IN), lambda b, i: (layer, 0, 0)),
        tab, tab, tab,
        _const_spec((1, W_A)),
        _const_spec((GMLP_HEADS * GMLP_CHUNK, GMLP_CHUNK)),
        _const_spec((GMLP_CHUNK, W_A)),
        _const_spec((W_B, LANES)),
    ]
    out_specs = [_tok_spec(tm, W_A), head, head, head, _tok_spec(tm, W_B), _tok_spec(tm, W_C),
                 _tok_spec(tm, W_C), norm_blk, norm_blk]
    out_shape = [tok_shape(W_A), head_shape, head_shape, head_shape,
                 tok_shape(W_B), tok_shape(W_C), tok_shape(W_C), norm_shape, norm_shape]
    assert len(in_specs) == N_INPROJ_IN and len(out_specs) == N_INPROJ_OUT
    return in_specs, out_specs, out_shape


def _inproj(xs, mod_l, norm_g, w_in, layer, cos, slo, shi, sgu_g, ws, bs, half_ind):
    bsz, lt, _ = xs.shape
    tm = IN_TILE
    in_specs, out_specs, out_shape = _inproj_specs(bsz, lt, tm, layer)
    return pl.pallas_call(
        functools.partial(_inproj_kernel, tm=tm, lat=lt - CTX_LEN),
        grid=(bsz, lt // tm),
        in_specs=[_tok_spec(tm, D_MODEL)] + in_specs,
        out_specs=out_specs,
        out_shape=out_shape,
        scratch_shapes=[pltpu.VMEM((D_MODEL, D_IN), BF16)],
        compiler_params=pltpu.CompilerParams(
            dimension_semantics=("arbitrary", "arbitrary"),
            vmem_limit_bytes=VMEM_LIMIT_BYTES),
        name="in_projection",
    )(xs, mod_l, norm_g, w_in, cos, slo, shi, sgu_g, ws, bs, half_ind)


NT_DIMS = (((1,), (1,)), ((), ()))
NORM_SLACK = 1.02
MAX_SCORE_BOUND = 50.0
GZ_ARG = 7


def _widen_values(v_sc, v_ref):
    v_sc[:, 0:HEAD_W] = v_ref[0, 0]
    v_sc[:, HEAD_W:2 * HEAD_W] = jnp.ones((v_sc.shape[0], HEAD_W), BF16)


def _stack_q(qz_sc, q_ref, tq):
    q = q_ref[0, 0]
    lane = lax.broadcasted_iota(jnp.int32, (tq, HEAD_W), 1)
    zero = jnp.zeros_like(q)
    qz_sc[0:tq] = jnp.where(lane < DIFF_HEAD_DIM, q, zero)
    qz_sc[tq:2 * tq] = jnp.where(lane >= DIFF_HEAD_DIM, q, zero)
    return qz_sc[...]


def _finish_head(acc, lam_refs, sg_ref, gz_ref, o_ref, tq, lam_init):
    lq1_ref, lk1_ref, lq2_ref, lk2_ref = lam_refs
    o = acc[:, 0:HEAD_W] / acc[:, HEAD_W:2 * HEAD_W]
    lam = (jnp.exp(jnp.sum(lq1_ref[...] * lk1_ref[...]))
           - jnp.exp(jnp.sum(lq2_ref[...] * lk2_ref[...])) + lam_init)
    o = o[0:tq] - lam * o[tq:2 * tq]
    o = o * lax.rsqrt(jnp.mean(o * o, axis=-1, keepdims=True) + EPS) * sg_ref[...]
    o_ref[0] = (o * (1.0 - lam_init) * gz_ref[0].astype(F32)).astype(BF16)


def _pipeline_maps(bsz, n_q):
    n_tiles = bsz * DIFF_HEADS * n_q

    def score_tile(g):
        t = jnp.minimum(g, n_tiles - 1)
        return t // (DIFF_HEADS * n_q), (t // n_q) % DIFF_HEADS, t % n_q

    def value_tile(g):
        return score_tile(jnp.maximum(g - 1, 0))

    def score_head(g):
        b, h, _ = score_tile(g)
        return (b, h, 0, 0)

    def score_rows(g):
        b, h, i = score_tile(g)
        return (b, h, i, 0)

    def value_head(g):
        b, h, _ = value_tile(g)
        return (b, h, 0, 0)

    def value_tok(g):
        b, h, i = value_tile(g)
        return (b, i, h)

    return n_tiles, score_head, score_rows, value_head, value_tok


def _attn_bounded_kernel(lq1_ref, lk1_ref, lq2_ref, lk2_ref, q_ref, k_ref, v_ref, gz_ref, sg_ref,
                         kb_ref, o_ref, qz_sc, e0_sc, e1_sc, l0_sc, l1_sc,
                         *, tq, tk, kt, lam_init):
    g = pl.program_id(0)
    n_keys = e0_sc.shape[1]

    @pl.when(g == 0)
    def _():
        e1_sc[...] = jnp.zeros_like(e1_sc)
        l1_sc[...] = jnp.ones_like(l1_sc)

    def step(e_w, l_w, e_r, l_r):
        qz = _stack_q(qz_sc, q_ref, tq)
        qf = qz.astype(F32)
        q_norm = jnp.sqrt(jnp.sum(qf * qf, axis=-1, keepdims=True))
        row = lax.broadcasted_iota(jnp.int32, (2 * tq, 1), 0)
        kb = kb_ref[0, 0]
        bound = q_norm * jnp.where(row < tq, kb[0:1], kb[1:2])
        l = None
        for c in range(n_keys // tk):
            s = lax.dot_general(qz, k_ref[0, 0, c * tk:(c + 1) * tk, :], NT_DIMS,
                                preferred_element_type=F32)
            for t in range(tk // LANES):
                e = jnp.exp2(s[:, t * LANES:(t + 1) * LANES] - bound)
                l = e if l is None else l + e
                e_w[:, c * tk + t * LANES:c * tk + (t + 1) * LANES] = e.astype(BF16)
        l_w[...] = jnp.broadcast_to(jnp.sum(l, axis=-1, keepdims=True), l_w.shape)

        l_prev = l_r[...]
        lam = (jnp.exp(jnp.sum(lq1_ref[...] * lk1_ref[...]))
               - jnp.exp(jnp.sum(lq2_ref[...] * lk2_ref[...])) + lam_init)
        ratio = (lam * l_prev[0:tq] / l_prev[tq:2 * tq]).astype(BF16)
        ratio = jnp.concatenate([ratio] * (kt // LANES), axis=1)
        acc = None
        for c in range(n_keys // kt):
            a = e_r[0:tq, c * kt:(c + 1) * kt] - ratio * e_r[tq:2 * tq, c * kt:(c + 1) * kt]
            vv = v_ref[0, 0, c * kt:(c + 1) * kt, :]
            d = jnp.dot(a, jnp.concatenate([vv, vv], axis=1), preferred_element_type=F32)
            acc = d if acc is None else acc + d
        o = acc[:, 0:HEAD_W] / l_prev[0:tq]
        o = o * lax.rsqrt(jnp.mean(o * o, axis=-1, keepdims=True) + EPS) * sg_ref[...]
        o_ref[0] = (o * (1.0 - lam_init) * gz_ref[0].astype(F32)).astype(BF16)

    @pl.when(g % 2 == 0)
    def _():
        step(e0_sc, l0_sc, e1_sc, l1_sc)

    @pl.when(g % 2 == 1)
    def _():
        step(e1_sc, l1_sc, e0_sc, l0_sc)


def _attention_bounded(q, k, v, gz, lams, subln_g, k_bound, lam_init, *, tk, kt):
    bsz, _, lt, _ = q.shape
    tq = Q_TILE_BOUNDED
    n_q = (lt - CTX_LEN) // tq
    n_tiles, score_head, score_rows, value_head, value_tok = _pipeline_maps(bsz, n_q)
    lam_spec = pl.BlockSpec((1, DIFF_HEAD_DIM), lambda g: (0, 0))
    seq_blk = (1, 1, lt, HEAD_W)
    return pl.pallas_call(
        functools.partial(_attn_bounded_kernel, tq=tq, tk=tk, kt=kt, lam_init=lam_init),
        grid=(n_tiles + 1,),
        in_specs=[
            lam_spec, lam_spec, lam_spec, lam_spec,
            pl.BlockSpec((1, 1, tq, HEAD_W), score_rows),
            pl.BlockSpec(seq_blk, score_head),
            pl.BlockSpec(seq_blk, value_head),
            pl.BlockSpec((1, tq, HEAD_W), value_tok),
            pl.BlockSpec((1, HEAD_W), lambda g: (0, 0)),
            pl.BlockSpec((1, 1, MOD_ROWS, LANES), score_head),
        ],
        out_specs=pl.BlockSpec((1, tq, HEAD_W), value_tok),
        out_shape=jax.ShapeDtypeStruct((bsz, lt, W_B), BF16),
        input_output_aliases={GZ_ARG: 0},
        scratch_shapes=[
            pltpu.VMEM((2 * tq, HEAD_W), BF16),
            pltpu.VMEM((2 * tq, lt), BF16),
            pltpu.VMEM((2 * tq, lt), BF16),
            pltpu.VMEM((2 * tq, LANES), F32),
            pltpu.VMEM((2 * tq, LANES), F32),
        ],
        compiler_params=pltpu.CompilerParams(
            dimension_semantics=("arbitrary",),
            vmem_limit_bytes=VMEM_LIMIT_BYTES),
        name="diff_attention_bounded",
    )(*lams, q, k, v, gz, subln_g, k_bound)


def _attn_pipe_kernel(lq1_ref, lk1_ref, lq2_ref, lk2_ref, q_ref, k_ref, v_ref, gz_ref, sg_ref,
                      o_ref, qz_sc, v_sc, s0_sc, s1_sc, m0_sc, m1_sc,
                      *, tq, tk, kt, n_q, lam_init):
    g = pl.program_id(0)
    n_keys = s0_sc.shape[1]

    @pl.when(g == 0)
    def _():
        s1_sc[...] = jnp.zeros_like(s1_sc)
        m1_sc[...] = jnp.zeros_like(m1_sc)

    @pl.when((g == 0) | ((g - 1) % n_q == 0))
    def _():
        _widen_values(v_sc, v_ref)

    def step(s_w, m_w, s_r, m_r):
        qz = _stack_q(qz_sc, q_ref, tq)
        m = None
        for c in range(n_keys // tk):
            s = lax.dot_general(qz, k_ref[0, 0, c * tk:(c + 1) * tk, :], NT_DIMS,
                                preferred_element_type=F32)
            s_w[:, c * tk:(c + 1) * tk] = s
            for t in range(tk // LANES):
                blk = s[:, t * LANES:(t + 1) * LANES]
                m = blk if m is None else jnp.maximum(m, blk)
        m_w[...] = jnp.broadcast_to(jnp.max(m, axis=-1, keepdims=True), m_w.shape)

        m_prev = jnp.concatenate([m_r[...]] * (kt // LANES), axis=1)
        acc = None
        for c in range(n_keys // kt):
            p = jnp.exp2(s_r[:, c * kt:(c + 1) * kt] - m_prev).astype(BF16)
            d = jnp.dot(p, v_sc[c * kt:(c + 1) * kt, :], preferred_element_type=F32)
            acc = d if acc is None else acc + d
        _finish_head(acc, (lq1_ref, lk1_ref, lq2_ref, lk2_ref), sg_ref, gz_ref, o_ref,
                     tq, lam_init)

    @pl.when(g % 2 == 0)
    def _():
        step(s0_sc, m0_sc, s1_sc, m1_sc)

    @pl.when(g % 2 == 1)
    def _():
        step(s1_sc, m1_sc, s0_sc, m0_sc)


def _attention_pipelined(q, k, v, gz, lams, subln_g, lam_init, *, tk, kt):
    bsz, _, lt, _ = q.shape
    tq = Q_TILE
    n_q = (lt - CTX_LEN) // tq
    n_tiles, score_head, score_rows, value_head, value_tok = _pipeline_maps(bsz, n_q)
    lam_spec = pl.BlockSpec((1, DIFF_HEAD_DIM), lambda g: (0, 0))
    seq_blk = (1, 1, lt, HEAD_W)
    return pl.pallas_call(
        functools.partial(_attn_pipe_kernel, tq=tq, tk=tk, kt=kt, n_q=n_q, lam_init=lam_init),
        grid=(n_tiles + 1,),
        in_specs=[
            lam_spec, lam_spec, lam_spec, lam_spec,
            pl.BlockSpec((1, 1, tq, HEAD_W), score_rows),
            pl.BlockSpec(seq_blk, score_head),
            pl.BlockSpec(seq_blk, value_head),
            pl.BlockSpec((1, tq, HEAD_W), value_tok),
            pl.BlockSpec((1, HEAD_W), lambda g: (0, 0)),
        ],
        out_specs=pl.BlockSpec((1, tq, HEAD_W), value_tok),
        out_shape=jax.ShapeDtypeStruct((bsz, lt, W_B), BF16),
        input_output_aliases={GZ_ARG: 0},
        scratch_shapes=[
            pltpu.VMEM((2 * tq, HEAD_W), BF16),
            pltpu.VMEM((lt, 2 * HEAD_W), BF16),
            pltpu.VMEM((2 * tq, lt), F32),
            pltpu.VMEM((2 * tq, lt), F32),
            pltpu.VMEM((2 * tq, LANES), F32),
            pltpu.VMEM((2 * tq, LANES), F32),
        ],
        compiler_params=pltpu.CompilerParams(
            dimension_semantics=("arbitrary",),
            vmem_limit_bytes=VMEM_LIMIT_BYTES),
        name="diff_attention_latent",
    )(*lams, q, k, v, gz, subln_g)


def _attn_ctx_kernel(lq1_ref, lk1_ref, lq2_ref, lk2_ref, q_ref, k_ref, v_ref, gz_ref, sg_ref,
                     o_ref, qz_sc, v_sc, *, tq, lam_init):
    _widen_values(v_sc, v_ref)
    qz = _stack_q(qz_sc, q_ref, tq)
    s = lax.dot_general(qz, k_ref[0, 0], NT_DIMS, preferred_element_type=F32)
    p = jnp.exp2(s - jnp.max(s, axis=-1, keepdims=True)).astype(BF16)
    acc = jnp.dot(p, v_sc[...], preferred_element_type=F32)
    _finish_head(acc, (lq1_ref, lk1_ref, lq2_ref, lk2_ref), sg_ref, gz_ref, o_ref, tq, lam_init)


def _attention_ctx(q, k, v, gz, lams, subln_g, lam_init):
    bsz, _, lt, _ = q.shape
    tq = CTX_LEN
    blk = (lt - CTX_LEN) // tq
    lam_spec = pl.BlockSpec((1, DIFF_HEAD_DIM), lambda b, h: (0, 0))
    head_blk = pl.BlockSpec((1, 1, tq, HEAD_W), lambda b, h: (b, h, blk, 0))
    tok_blk = pl.BlockSpec((1, tq, HEAD_W), lambda b, h: (b, blk, h))
    return pl.pallas_call(
        functools.partial(_attn_ctx_kernel, tq=tq, lam_init=lam_init),
        grid=(bsz, DIFF_HEADS),
        in_specs=[lam_spec, lam_spec, lam_spec, lam_spec, head_blk, head_blk, head_blk, tok_blk,
                  pl.BlockSpec((1, HEAD_W), lambda b, h: (0, 0))],
        out_specs=tok_blk,
        out_shape=jax.ShapeDtypeStruct((bsz, lt, W_B), BF16),
        input_output_aliases={GZ_ARG: 0},
        scratch_shapes=[
            pltpu.VMEM((2 * tq, HEAD_W), BF16),
            pltpu.VMEM((tq, 2 * HEAD_W), BF16),
        ],
        compiler_params=pltpu.CompilerParams(
            dimension_semantics=("arbitrary", "arbitrary"),
            vmem_limit_bytes=VMEM_LIMIT_BYTES),
        name="diff_attention_ctx",
    )(*lams, q, k, v, gz, subln_g)


N_OUTPROJ_IN = 12


def _outproj_kernel(*refs, tm, lat, lt, final):
    o_ref, w_sc = refs[N_OUTPROJ_IN:]
    _cast_weights_once(w_sc, refs[10])
    o_ref[0] = _outproj_body(*refs[:N_OUTPROJ_IN], w_sc, tm=tm, lat=lat, lt=lt, final=final)


def _outproj_body(x_ref, mod_ref, ya_ref, yb_ref, p_ref, pprev_ref, pnext_ref, gc_ref,
                  cw_ref, cb_ref, w_ref, fg_ref, w_sc, *, tm, lat, lt, final):
    del w_ref
    i = pl.program_id(1)
    row = i * tm + lax.broadcasted_iota(jnp.int32, (tm, 1), 0)
    local = lax.broadcasted_iota(jnp.int32, (tm, 1), 0)
    p = p_ref[0].astype(F32)
    prev_row = pprev_ref[0, HALO - 1:HALO, :].astype(F32)
    next_row = pnext_ref[0, 0:1, :].astype(F32)
    up = jnp.where(local == 0, prev_row, pltpu.roll(p, 1, 0))
    dn = jnp.where(local == tm - 1, next_row, pltpu.roll(p, tm - 1, 0))
    up = jnp.where((row == 0) | (row == lat), 0.0, up)
    dn = jnp.where((row == lat - 1) | (row == lt - 1), 0.0, dn)
    cw = cw_ref[...]
    conv = up * cw[0:1] + p * cw[1:2] + dn * cw[2:3] + cb_ref[...]
    yc = (gc_ref[0].astype(F32) * conv).astype(BF16)

    y = jnp.dot(ya_ref[0], w_sc[0:W_A], preferred_element_type=F32)
    y = y + jnp.dot(yb_ref[0], w_sc[W_A:W_A + W_B], preferred_element_type=F32)
    y = y + jnp.dot(yc, w_sc[W_A + W_B:D_MIX], preferred_element_type=F32)

    mod = mod_ref[0]
    gate = jnp.where(row >= lat, mod[5:6], mod[2:3])
    x = x_ref[0] + gate * y
    if final:
        x = x * lax.rsqrt(jnp.mean(x * x, axis=-1, keepdims=True) + EPS) * fg_ref[...]
    return x


def _midproj_kernel(*refs, tm, lat, lt):
    out_in = refs[:N_OUTPROJ_IN]
    in_in = refs[N_OUTPROJ_IN:N_OUTPROJ_IN + N_INPROJ_IN]
    x_ref = refs[N_OUTPROJ_IN + N_INPROJ_IN]
    in_out = refs[N_OUTPROJ_IN + N_INPROJ_IN + 1:N_OUTPROJ_IN + N_INPROJ_IN + 1 + N_INPROJ_OUT]
    wo_sc, wi_sc = refs[-2:]
    _cast_weights_once(wo_sc, out_in[10])
    _cast_weights_once(wi_sc, in_in[2])
    x = _outproj_body(*out_in, wo_sc, tm=tm, lat=lat, lt=lt, final=False)
    x_ref[0] = x
    _inproj_body(x, *in_in, *in_out, wi_sc, tm=tm, lat=lat)


def _outproj_specs(lt, tm, layer):
    hpt = tm // HALO
    n_halo = lt // HALO
    prev = pl.BlockSpec((1, HALO, W_C),
                        lambda b, i: (b, jnp.maximum(i * hpt - 1, 0), 0))
    nxt = pl.BlockSpec((1, HALO, W_C),
                       lambda b, i: (b, jnp.minimum((i + 1) * hpt, n_halo - 1), 0))
    in_specs = [
        _tok_spec(tm, D_MODEL),
        pl.BlockSpec((1, MOD_ROWS, D_MODEL), lambda b, i: (b, 0, 0)),
        _tok_spec(tm, W_A), _tok_spec(tm, W_B), _tok_spec(tm, W_C), prev, nxt, _tok_spec(tm, W_C),
        _const_spec((3, W_C)), _const_spec((1, W_C)),
        pl.BlockSpec((1, D_MIX, D_MODEL), lambda b, i: (layer, 0, 0)), _const_spec((1, D_MODEL)),
    ]
    assert len(in_specs) == N_OUTPROJ_IN
    return in_specs


def _outproj_final(xs, mod_l, ya, yb, p, gc, conv_w, conv_b, w_out, layer, final_g):
    bsz, lt, _ = xs.shape
    lat = lt - CTX_LEN
    tm = FINAL_OUT_TILE
    return pl.pallas_call(
        functools.partial(_outproj_kernel, tm=tm, lat=lat, lt=lt, final=True),
        grid=(bsz, lat // tm),
        in_specs=_outproj_specs(lt, tm, layer),
        out_specs=_tok_spec(tm, D_MODEL),
        out_shape=jax.ShapeDtypeStruct((bsz, lat, D_MODEL), F32),
        scratch_shapes=[pltpu.VMEM((D_MIX, D_MODEL), BF16)],
        compiler_params=pltpu.CompilerParams(
            dimension_semantics=("arbitrary", "arbitrary"),
            vmem_limit_bytes=VMEM_LIMIT_BYTES),
        name="out_projection",
    )(xs, mod_l, ya, yb, p, p, p, gc, conv_w, conv_b, w_out, final_g)


def _midproj(xs, out_args, in_args, layer):
    bsz, lt, _ = xs.shape
    tm = MID_TILE
    mod_prev, ya, yb, p, gc, conv_w, conv_b, w_out, final_g = out_args
    in_specs, out_specs, out_shape = _inproj_specs(bsz, lt, tm, layer)
    return pl.pallas_call(
        functools.partial(_midproj_kernel, tm=tm, lat=lt - CTX_LEN, lt=lt),
        grid=(bsz, lt // tm),
        in_specs=_outproj_specs(lt, tm, layer - 1) + in_specs,
        out_specs=[_tok_spec(tm, D_MODEL)] + out_specs,
        out_shape=[jax.ShapeDtypeStruct((bsz, lt, D_MODEL), F32)] + out_shape,
        scratch_shapes=[pltpu.VMEM((D_MIX, D_MODEL), BF16), pltpu.VMEM((D_MODEL, D_IN), BF16)],
        compiler_params=pltpu.CompilerParams(
            dimension_semantics=("arbitrary", "arbitrary"),
            vmem_limit_bytes=VMEM_LIMIT_BYTES),
        name="mid_projection",
    )(xs, mod_prev, ya, yb, p, p, p, gc, conv_w, conv_b, w_out, final_g, *in_args)


def _rope_tables(length):
    t = jnp.arange(length)[:, None]
    lane = jnp.arange(HEAD_W)[None, :]
    col_axis = (lane % DIFF_HEAD_DIM) >= 2 * ROPE_PAIRS
    second = (lane % (2 * ROPE_PAIRS)) >= ROPE_PAIRS
    pos = jnp.where(col_axis, t % GRID_W, t // GRID_W).astype(F32)
    inv = ROPE_BASE ** (-jnp.arange(ROPE_PAIRS, dtype=F32) / ROPE_PAIRS)
    ang = pos * jnp.tile(inv, HEAD_W // ROPE_PAIRS)
    cos = jnp.cos(ang)
    sin = jnp.sin(ang)
    sin_lo = jnp.where(second, 0.0, -sin)
    sin_hi = jnp.where(second, sin, 0.0)
    ident = jnp.ones((CTX_LEN, HEAD_W), F32)
    zeros = jnp.zeros((CTX_LEN, HEAD_W), F32)
    return (jnp.concatenate([cos, ident]), jnp.concatenate([sin_lo, zeros]),
            jnp.concatenate([sin_hi, zeros]))


def kernel(x, c, ctx, c_ctx, w_mod, b_mod, norm_g, w_in, w_out, sgu_norm_g, sgu_w, sgu_b,
           lambda_q1, lambda_k1, lambda_q2, lambda_k2, subln_g, conv_w, conv_b, final_g):
    bsz, length, _ = x.shape
    assert ctx.shape[1] == CTX_LEN and bsz + 1 <= MOD_ROWS
    lt = CTX_LEN + length
    assert lt % IN_TILE == 0 and lt % MID_TILE == 0
    assert length % FINAL_OUT_TILE == 0 and length % Q_TILE == 0 and length % Q_TILE_BOUNDED == 0
    assert length % CTX_LEN == 0 and lt % K_TILE == 0

    cond = jnp.zeros((MOD_ROWS, D_MODEL), F32).at[:bsz].set(c).at[bsz].set(c_ctx)
    mod = _modulation(cond, w_mod, b_mod)
    mod_b = jnp.transpose(mod[:, :, :bsz], (0, 2, 1, 3))
    mod_c = jnp.broadcast_to(mod[:, None, :, bsz], (DEPTH, bsz, 3, D_MODEL))
    pad = jnp.zeros((DEPTH, bsz, MOD_ROWS - 6, D_MODEL), F32)
    mod_rows = jnp.concatenate([mod_b, mod_c, pad], axis=2)

    cos, slo, shi = _rope_tables(length)
    ws = sgu_w.reshape(DEPTH, GMLP_HEADS * GMLP_CHUNK, GMLP_CHUNK).astype(BF16)
    bs = jnp.repeat(jnp.transpose(sgu_b, (0, 2, 1)), GMLP_HEAD_DIM, axis=2)

    half_ind = (jnp.arange(W_B)[:, None] // DIFF_HEAD_DIM == jnp.arange(LANES)[None, :]).astype(BF16)
    n_halves = 2 * DIFF_HEADS

    def norm_bounds(n):
        n = jnp.max(n, axis=(1, 2))[:, :n_halves].reshape(bsz, DIFF_HEADS, 2)
        return jnp.sqrt(n) * NORM_SLACK

    def inproj_args(l):
        return (mod_rows[l], norm_g[l][None], w_in, cos, slo, shi,
                sgu_norm_g[l][None], ws[l], bs[l], half_ind)

    xs = jnp.concatenate([x, ctx], axis=1)
    a = inproj_args(0)
    proj = _inproj(xs, *a[:3], 0, *a[3:])
    for l in range(DEPTH):
        lam_init = 0.8 - 0.6 * math.exp(-0.3 * l)
        ya, q, k, v, gz, p, gc, qn, kn = proj
        lams = (lambda_q1[l][None], lambda_k1[l][None], lambda_q2[l][None], lambda_k2[l][None])
        yb = _attention_ctx(q, k, v, gz, lams, subln_g[l][None], lam_init)
        q_norm, k_norm = norm_bounds(qn), norm_bounds(kn)
        k_bound = jnp.zeros((bsz, DIFF_HEADS, MOD_ROWS, LANES), F32).at[:, :, 0:2, :].set(
            jnp.broadcast_to(k_norm[..., None], (bsz, DIFF_HEADS, 2, LANES)))
        bounded = jnp.max(q_norm * k_norm) <= MAX_SCORE_BOUND
        yb = lax.cond(
            bounded,
            lambda q, k, v, yb, kb: _attention_bounded(
                q, k, v, yb, lams, subln_g[l][None], kb, lam_init,
                tk=K_TILE, kt=K_TILE),
            lambda q, k, v, yb, kb: _attention_pipelined(
                q, k, v, yb, lams, subln_g[l][None], lam_init,
                tk=K_TILE, kt=V_TILE),
            q, k, v, yb, k_bound)
        out_args = (mod_rows[l], ya, yb, p, gc, conv_w[l], conv_b[l][None], w_out, final_g[None])
        if l + 1 < DEPTH:
            xs, *proj = _midproj(xs, out_args, inproj_args(l + 1), l + 1)
        else:
            xs = _outproj_final(xs, *out_args[:-1], l, out_args[-1])
    return xs
```

```python
import functools
import math

import jax
import jax.numpy as jnp
from jax import lax
from jax.experimental import pallas as pl
from jax.experimental.pallas import tpu as pltpu

D_MODEL = 1024
DEPTH = 4
CTX_LEN = 256
GRID_W = 64

GMLP_HEADS = 4
GMLP_HEAD_DIM = 64
GMLP_CHUNK = 128
W_A = GMLP_HEADS * GMLP_HEAD_DIM
DIFF_HEADS = 4
DIFF_HEAD_DIM = 64
HEAD_W = 2 * DIFF_HEAD_DIM
W_B = DIFF_HEADS * HEAD_W
W_C = 256
D_MIX = W_A + W_B + W_C
D_IN = 3 * W_A + 4 * W_B + 4 * W_C

COL_A = 0
COL_Q = 3 * W_A
COL_K = COL_Q + W_B
COL_V = COL_K + W_B
COL_C = COL_V + 2 * W_B

ROPE_BASE = 10000.0
ROPE_PAIRS = DIFF_HEAD_DIM // 4
EPS = 1e-6

F32 = jnp.float32
BF16 = jnp.bfloat16

VMEM_LIMIT_BYTES = 56 * 1024 * 1024
MOD_ROWS = 8
LANES = 128

IN_TILE = 768
MID_TILE = 384
FINAL_OUT_TILE = 512
Q_TILE = 256
Q_TILE_BOUNDED = 512
K_TILE = 2816
V_TILE = 256
HALO = 16


def _silu(x):
    return x * jax.nn.sigmoid(x)


def _mod_kernel(cond_ref, w_ref, b_ref, o_ref):
    a = _silu(cond_ref[...])
    o_ref[0, 0] = jnp.dot(a, w_ref[0], preferred_element_type=F32) + b_ref[0, 0]


def _modulation(cond, w_mod, b_mod):
    return pl.pallas_call(
        _mod_kernel,
        grid=(DEPTH, 3),
        in_specs=[
            pl.BlockSpec((MOD_ROWS, D_MODEL), lambda l, j: (0, 0)),
            pl.BlockSpec((1, D_MODEL, D_MODEL), lambda l, j: (l, 0, j)),
            pl.BlockSpec((1, 1, 1, D_MODEL), lambda l, j: (l, j, 0, 0)),
        ],
        out_specs=pl.BlockSpec((1, 1, MOD_ROWS, D_MODEL), lambda l, j: (l, j, 0, 0)),
        out_shape=jax.ShapeDtypeStruct((DEPTH, 3, MOD_ROWS, D_MODEL), F32),
        compiler_params=pltpu.CompilerParams(
            dimension_semantics=("arbitrary", "arbitrary"),
            vmem_limit_bytes=VMEM_LIMIT_BYTES),
        name="adaln_modulation",
    )(cond, w_mod, b_mod.reshape(DEPTH, 3, 1, D_MODEL))


def _rope(t, cos, sin_lo, sin_hi):
    return (t * cos + pltpu.roll(t, HEAD_W - ROPE_PAIRS, 1) * sin_lo
            + pltpu.roll(t, ROPE_PAIRS, 1) * sin_hi)


def _cast_weights_once(w_sc, w_ref):
    @pl.when((pl.program_id(0) == 0) & (pl.program_id(1) == 0))
    def _():
        w_sc[...] = w_ref[0].astype(BF16)


N_INPROJ_IN = 10
N_INPROJ_OUT = 9


def _inproj_kernel(*refs, tm, lat):
    n_sub = tm // CTX_LEN
    sub_refs, ctx_ref = refs[:n_sub], refs[n_sub]
    in_refs = refs[n_sub + 1:n_sub + 1 + N_INPROJ_IN]
    xs_ref = refs[n_sub + 1 + N_INPROJ_IN]
    rest = refs[n_sub + 2 + N_INPROJ_IN:]
    _cast_weights_once(rest[-1], in_refs[2])
    is_last = pl.program_id(1) == pl.num_programs(1) - 1
    tail = jnp.where(is_last, ctx_ref[0], sub_refs[-1][0])
    x = jnp.concatenate([r[0] for r in sub_refs[:-1]] + [tail], axis=0)
    xs_ref[0] = x
    _inproj_body(x, *in_refs, *rest, tm=tm, lat=lat)


def _inproj_body(x, mod_ref, g_ref, w_ref, cos_ref, slo_ref, shi_ref,
                 sgug_ref, ws_ref, bs_ref, half_ref,
                 ya_ref, q_ref, k_ref, v_ref, gz_ref, p_ref, gc_ref, qn_ref, kn_ref,
                 w_sc, *, tm, lat):
    del w_ref
    i = pl.program_id(1)
    row = i * tm + lax.broadcasted_iota(jnp.int32, (tm, 1), 0)
    is_ctx = row >= lat
    mod = mod_ref[0]
    shift = jnp.where(is_ctx, mod[3:4], mod[0:1])
    scale = jnp.where(is_ctx, mod[4:5], mod[1:2])
    h = x * lax.rsqrt(jnp.mean(x * x, axis=-1, keepdims=True) + EPS) * g_ref[...]
    h = (h * (1.0 + scale) + shift).astype(BF16)

    ra = jnp.dot(h, w_sc[:, COL_A:COL_A + 3 * W_A], preferred_element_type=F32)
    u = jax.nn.gelu(ra[:, 0:W_A])
    v = jax.nn.gelu(ra[:, W_A:2 * W_A])
    gate_a = _silu(ra[:, 2 * W_A:3 * W_A])
    vn = v * lax.rsqrt(jnp.mean(v * v, axis=-1, keepdims=True) + EPS) * sgug_ref[...]
    vn = vn.astype(BF16)
    lane = lax.broadcasted_iota(jnp.int32, (GMLP_CHUNK, W_A), 1)
    for c in range(tm // GMLP_CHUNK):
        rows = slice(c * GMLP_CHUNK, (c + 1) * GMLP_CHUNK)
        r = jnp.dot(ws_ref[...], vn[rows], preferred_element_type=F32)
        mixed = r[0:GMLP_CHUNK]
        for hd in range(1, GMLP_HEADS):
            mixed = jnp.where(lane >= hd * GMLP_HEAD_DIM,
                              r[hd * GMLP_CHUNK:(hd + 1) * GMLP_CHUNK], mixed)
        mixed = mixed + bs_ref[...]
        ya_ref[0, rows, :] = (u[rows] * mixed * gate_a[rows]).astype(BF16)

    cos, slo, shi = cos_ref[...], slo_ref[...], shi_ref[...]
    rq = jnp.dot(h, w_sc[:, COL_Q:COL_Q + W_B], preferred_element_type=F32)
    rk = jnp.dot(h, w_sc[:, COL_K:COL_K + W_B], preferred_element_type=F32)
    q_scale = math.log2(math.e) / math.sqrt(DIFF_HEAD_DIM)
    q_heads, k_heads = [], []
    for hd in range(DIFF_HEADS):
        cols = slice(hd * HEAD_W, (hd + 1) * HEAD_W)
        q_heads.append(_rope(rq[:, cols], cos, slo, shi) * q_scale)
        k_heads.append(_rope(rk[:, cols], cos, slo, shi))
        q_ref[0, hd] = q_heads[hd].astype(BF16)
        k_ref[0, hd] = k_heads[hd].astype(BF16)
    for heads, n_ref in ((q_heads, qn_ref), (k_heads, kn_ref)):
        t = jnp.concatenate(heads, axis=1)
        n = jnp.dot((t * t).astype(BF16), half_ref[...], preferred_element_type=F32)
        best = n[0:MOD_ROWS]
        for r in range(1, tm // MOD_ROWS):
            best = jnp.maximum(best, n[r * MOD_ROWS:(r + 1) * MOD_ROWS])
        n_ref[0, 0] = best
    rz = jnp.dot(h, w_sc[:, COL_V + W_B:COL_V + 2 * W_B], preferred_element_type=F32)
    gz_ref[0] = _silu(rz).astype(BF16)

    rc = jnp.dot(h, w_sc[:, COL_C:COL_C + 4 * W_C], preferred_element_type=F32)
    p_ref[0] = (rc[:, W_C:2 * W_C] * rc[:, 2 * W_C:3 * W_C]).astype(BF16)
    gc_ref[0] = (rc[:, 0:W_C] * _silu(rc[:, 3 * W_C:4 * W_C])).astype(BF16)

    rv = jnp.dot(h, w_sc[:, COL_V:COL_V + W_B], preferred_element_type=F32)
    for hd in range(DIFF_HEADS):
        v_ref[0, hd] = rv[:, hd * HEAD_W:(hd + 1) * HEAD_W].astype(BF16)


def _const_spec(shape):
    return pl.BlockSpec(shape, lambda b, i: (0,) * len(shape))


def _tok_spec(tm, width):
    return pl.BlockSpec((1, tm, width), lambda b, i: (b, i, 0))


def _inproj_specs(bsz, lt, tm, layer):
    norm_blk = pl.BlockSpec((1, 1, MOD_ROWS, LANES), lambda b, i: (b, i, 0, 0))
    norm_shape = jax.ShapeDtypeStruct((bsz, lt // tm, MOD_ROWS, LANES), F32)
    head = pl.BlockSpec((1, DIFF_HEADS, tm, HEAD_W), lambda b, i: (b, 0, i, 0))
    tab = pl.BlockSpec((tm, HEAD_W), lambda b, i: (i, 0))
    tok_shape = lambda width: jax.ShapeDtypeStruct((bsz, lt, width), BF16)
    head_shape = jax.ShapeDtypeStruct((bsz, DIFF_HEADS, lt, HEAD_W), BF16)
    in_specs = [
        pl.BlockSpec((1, MOD_ROWS, D_MODEL), lambda b, i: (b, 0, 0)),
        _const_spec((1, D_MODEL)),
        pl.BlockSpec((1, D_MODEL, D_IN), lambda b, i: (layer, 0, 0)),
        tab, tab, tab,
        _const_spec((1, W_A)),
        _const_spec((GMLP_HEADS * GMLP_CHUNK, GMLP_CHUNK)),
        _const_spec((GMLP_CHUNK, W_A)),
        _const_spec((W_B, LANES)),
    ]
    out_specs = [_tok_spec(tm, W_A), head, head, head, _tok_spec(tm, W_B), _tok_spec(tm, W_C),
                 _tok_spec(tm, W_C), norm_blk, norm_blk]
    out_shape = [tok_shape(W_A), head_shape, head_shape, head_shape,
                 tok_shape(W_B), tok_shape(W_C), tok_shape(W_C), norm_shape, norm_shape]
    assert len(in_specs) == N_INPROJ_IN and len(out_specs) == N_INPROJ_OUT
    return in_specs, out_specs, out_shape


def _inproj_first(x, ctx, in_args, layer):
    bsz, lat, _ = x.shape
    lt = lat + CTX_LEN
    tm = IN_TILE
    n_sub = tm // CTX_LEN
    last_blk = lat // CTX_LEN - 1
    sub = [pl.BlockSpec((1, CTX_LEN, D_MODEL),
                        lambda b, i, j=j: (b, jnp.minimum(i * n_sub + j, last_blk), 0))
           for j in range(n_sub)]
    ctx_spec = pl.BlockSpec((1, CTX_LEN, D_MODEL), lambda b, i: (b, 0, 0))
    in_specs, out_specs, out_shape = _inproj_specs(bsz, lt, tm, layer)
    return pl.pallas_call(
        functools.partial(_inproj_kernel, tm=tm, lat=lat),
        grid=(bsz, lt // tm),
        in_specs=sub + [ctx_spec] + in_specs,
        out_specs=[_tok_spec(tm, D_MODEL)] + out_specs,
        out_shape=[jax.ShapeDtypeStruct((bsz, lt, D_MODEL), F32)] + out_shape,
        scratch_shapes=[pltpu.VMEM((D_MODEL, D_IN), BF16)],
        compiler_params=pltpu.CompilerParams(
            dimension_semantics=("arbitrary", "arbitrary"),
            vmem_limit_bytes=VMEM_LIMIT_BYTES),
        name="in_projection",
    )(*([x] * n_sub), ctx, *in_args)


NT_DIMS = (((1,), (1,)), ((), ()))
NORM_SLACK = 1.02
MAX_SCORE_BOUND = 50.0
GZ_ARG = 7


def _widen_values(v_sc, v_ref):
    v_sc[:, 0:HEAD_W] = v_ref[0, 0]
    v_sc[:, HEAD_W:2 * HEAD_W] = jnp.ones((v_sc.shape[0], HEAD_W), BF16)


def _stack_q(qz_sc, q_ref, tq):
    q = q_ref[0, 0]
    lane = lax.broadcasted_iota(jnp.int32, (tq, HEAD_W), 1)
    zero = jnp.zeros_like(q)
    qz_sc[0:tq] = jnp.where(lane < DIFF_HEAD_DIM, q, zero)
    qz_sc[tq:2 * tq] = jnp.where(lane >= DIFF_HEAD_DIM, q, zero)
    return qz_sc[...]


def _finish_head(acc, lam_refs, sg_ref, gz_ref, o_ref, tq, lam_init):
    lq1_ref, lk1_ref, lq2_ref, lk2_ref = lam_refs
    o = acc[:, 0:HEAD_W] / acc[:, HEAD_W:2 * HEAD_W]
    lam = (jnp.exp(jnp.sum(lq1_ref[...] * lk1_ref[...]))
           - jnp.exp(jnp.sum(lq2_ref[...] * lk2_ref[...])) + lam_init)
    o = o[0:tq] - lam * o[tq:2 * tq]
    o = o * lax.rsqrt(jnp.mean(o * o, axis=-1, keepdims=True) + EPS) * sg_ref[...]
    o_ref[0] = (o * (1.0 - lam_init) * gz_ref[0].astype(F32)).astype(BF16)


def _pipeline_maps(bsz, n_q):
    n_tiles = bsz * DIFF_HEADS * n_q

    def score_tile(g):
        t = jnp.minimum(g, n_tiles - 1)
        return t // (DIFF_HEADS * n_q), (t // n_q) % DIFF_HEADS, t % n_q

    def value_tile(g):
        return score_tile(jnp.maximum(g - 1, 0))

    def score_head(g):
        b, h, _ = score_tile(g)
        return (b, h, 0, 0)

    def score_rows(g):
        b, h, i = score_tile(g)
        return (b, h, i, 0)

    def value_head(g):
        b, h, _ = value_tile(g)
        return (b, h, 0, 0)

    def value_tok(g):
        b, h, i = value_tile(g)
        return (b, i, h)

    return n_tiles, score_head, score_rows, value_head, value_tok


def _attn_bounded_kernel(lq1_ref, lk1_ref, lq2_ref, lk2_ref, q_ref, k_ref, v_ref, gz_ref, sg_ref,
                         kb_ref, o_ref, qz_sc, e0_sc, e1_sc, l0_sc, l1_sc,
                         *, tq, tk, kt, lam_init):
    g = pl.program_id(0)
    n_keys = e0_sc.shape[1]

    @pl.when(g == 0)
    def _():
        e1_sc[...] = jnp.zeros_like(e1_sc)
        l1_sc[...] = jnp.ones_like(l1_sc)

    def step(e_w, l_w, e_r, l_r):
        qz = _stack_q(qz_sc, q_ref, tq)
        qf = qz.astype(F32)
        q_norm = jnp.sqrt(jnp.sum(qf * qf, axis=-1, keepdims=True))
        row = lax.broadcasted_iota(jnp.int32, (2 * tq, 1), 0)
        kb = kb_ref[0, 0]
        bound = q_norm * jnp.where(row < tq, kb[0:1], kb[1:2])
        l = None
        for c in range(n_keys // tk):
            s = lax.dot_general(qz, k_ref[0, 0, c * tk:(c + 1) * tk, :], NT_DIMS,
                                preferred_element_type=F32)
            for t in range(tk // LANES):
                e = jnp.exp2(s[:, t * LANES:(t + 1) * LANES] - bound)
                l = e if l is None else l + e
                e_w[:, c * tk + t * LANES:c * tk + (t + 1) * LANES] = e.astype(BF16)
        l_w[...] = jnp.broadcast_to(jnp.sum(l, axis=-1, keepdims=True), l_w.shape)

        l_prev = l_r[...]
        lam = (jnp.exp(jnp.sum(lq1_ref[...] * lk1_ref[...]))
               - jnp.exp(jnp.sum(lq2_ref[...] * lk2_ref[...])) + lam_init)
        ratio = (lam * l_prev[0:tq] / l_prev[tq:2 * tq]).astype(BF16)
        ratio = jnp.concatenate([ratio] * (kt // LANES), axis=1)
        acc = None
        for c in range(n_keys // kt):
            a = e_r[0:tq, c * kt:(c + 1) * kt] - ratio * e_r[tq:2 * tq, c * kt:(c + 1) * kt]
            vv = v_ref[0, 0, c * kt:(c + 1) * kt, :]
            d = jnp.dot(a, jnp.concatenate([vv, vv], axis=1), preferred_element_type=F32)
            acc = d if acc is None else acc + d
        o = acc[:, 0:HEAD_W] / l_prev[0:tq]
        o = o * lax.rsqrt(jnp.mean(o * o, axis=-1, keepdims=True) + EPS) * sg_ref[...]
        o_ref[0] = (o * (1.0 - lam_init) * gz_ref[0].astype(F32)).astype(BF16)

    @pl.when(g % 2 == 0)
    def _():
        step(e0_sc, l0_sc, e1_sc, l1_sc)

    @pl.when(g % 2 == 1)
    def _():
        step(e1_sc, l1_sc, e0_sc, l0_sc)


def _attention_bounded(q, k, v, gz, lams, subln_g, k_bound, lam_init, *, tk, kt):
    bsz, _, lt, _ = q.shape
    tq = Q_TILE_BOUNDED
    n_q = (lt - CTX_LEN) // tq
    n_tiles, score_head, score_rows, value_head, value_tok = _pipeline_maps(bsz, n_q)
    lam_spec = pl.BlockSpec((1, DIFF_HEAD_DIM), lambda g: (0, 0))
    seq_blk = (1, 1, lt, HEAD_W)
    return pl.pallas_call(
        functools.partial(_attn_bounded_kernel, tq=tq, tk=tk, kt=kt, lam_init=lam_init),
        grid=(n_tiles + 1,),
        in_specs=[
            lam_spec, lam_spec, lam_spec, lam_spec,
            pl.BlockSpec((1, 1, tq, HEAD_W), score_rows),
            pl.BlockSpec(seq_blk, score_head),
            pl.BlockSpec(seq_blk, value_head),
            pl.BlockSpec((1, tq, HEAD_W), value_tok),
            pl.BlockSpec((1, HEAD_W), lambda g: (0, 0)),
            pl.BlockSpec((1, 1, MOD_ROWS, LANES), score_head),
        ],
        out_specs=pl.BlockSpec((1, tq, HEAD_W), value_tok),
        out_shape=jax.ShapeDtypeStruct((bsz, lt, W_B), BF16),
        input_output_aliases={GZ_ARG: 0},
        scratch_shapes=[
            pltpu.VMEM((2 * tq, HEAD_W), BF16),
            pltpu.VMEM((2 * tq, lt), BF16),
            pltpu.VMEM((2 * tq, lt), BF16),
            pltpu.VMEM((2 * tq, LANES), F32),
            pltpu.VMEM((2 * tq, LANES), F32),
        ],
        compiler_params=pltpu.CompilerParams(
            dimension_semantics=("arbitrary",),
            vmem_limit_bytes=VMEM_LIMIT_BYTES),
        name="diff_attention_bounded",
    )(*lams, q, k, v, gz, subln_g, k_bound)


def _attn_pipe_kernel(lq1_ref, lk1_ref, lq2_ref, lk2_ref, q_ref, k_ref, v_ref, gz_ref, sg_ref,
                      o_ref, qz_sc, v_sc, s0_sc, s1_sc, m0_sc, m1_sc,
                      *, tq, tk, kt, n_q, lam_init):
    g = pl.program_id(0)
    n_keys = s0_sc.shape[1]

    @pl.when(g == 0)
    def _():
        s1_sc[...] = jnp.zeros_like(s1_sc)
        m1_sc[...] = jnp.zeros_like(m1_sc)

    @pl.when((g == 0) | ((g - 1) % n_q == 0))
    def _():
        _widen_values(v_sc, v_ref)

    def step(s_w, m_w, s_r, m_r):
        qz = _stack_q(qz_sc, q_ref, tq)
        m = None
        for c in range(n_keys // tk):
            s = lax.dot_general(qz, k_ref[0, 0, c * tk:(c + 1) * tk, :], NT_DIMS,
                                preferred_element_type=F32)
            s_w[:, c * tk:(c + 1) * tk] = s
            for t in range(tk // LANES):
                blk = s[:, t * LANES:(t + 1) * LANES]
                m = blk if m is None else jnp.maximum(m, blk)
        m_w[...] = jnp.broadcast_to(jnp.max(m, axis=-1, keepdims=True), m_w.shape)

        m_prev = jnp.concatenate([m_r[...]] * (kt // LANES), axis=1)
        acc = None
        for c in range(n_keys // kt):
            p = jnp.exp2(s_r[:, c * kt:(c + 1) * kt] - m_prev).astype(BF16)
            d = jnp.dot(p, v_sc[c * kt:(c + 1) * kt, :], preferred_element_type=F32)
            acc = d if acc is None else acc + d
        _finish_head(acc, (lq1_ref, lk1_ref, lq2_ref, lk2_ref), sg_ref, gz_ref, o_ref,
                     tq, lam_init)

    @pl.when(g % 2 == 0)
    def _():
        step(s0_sc, m0_sc, s1_sc, m1_sc)

    @pl.when(g % 2 == 1)
    def _():
        step(s1_sc, m1_sc, s0_sc, m0_sc)


def _attention_pipelined(q, k, v, gz, lams, subln_g, lam_init, *, tk, kt):
    bsz, _, lt, _ = q.shape
    tq = Q_TILE
    n_q = (lt - CTX_LEN) // tq
    n_tiles, score_head, score_rows, value_head, value_tok = _pipeline_maps(bsz, n_q)
    lam_spec = pl.BlockSpec((1, DIFF_HEAD_DIM), lambda g: (0, 0))
    seq_blk = (1, 1, lt, HEAD_W)
    return pl.pallas_call(
        functools.partial(_attn_pipe_kernel, tq=tq, tk=tk, kt=kt, n_q=n_q, lam_init=lam_init),
        grid=(n_tiles + 1,),
        in_specs=[
            lam_spec, lam_spec, lam_spec, lam_spec,
            pl.BlockSpec((1, 1, tq, HEAD_W), score_rows),
            pl.BlockSpec(seq_blk, score_head),
            pl.BlockSpec(seq_blk, value_head),
            pl.BlockSpec((1, tq, HEAD_W), value_tok),
            pl.BlockSpec((1, HEAD_W), lambda g: (0, 0)),
        ],
        out_specs=pl.BlockSpec((1, tq, HEAD_W), value_tok),
        out_shape=jax.ShapeDtypeStruct((bsz, lt, W_B), BF16),
        input_output_aliases={GZ_ARG: 0},
        scratch_shapes=[
            pltpu.VMEM((2 * tq, HEAD_W), BF16),
            pltpu.VMEM((lt, 2 * HEAD_W), BF16),
            pltpu.VMEM((2 * tq, lt), F32),
            pltpu.VMEM((2 * tq, lt), F32),
            pltpu.VMEM((2 * tq, LANES), F32),
            pltpu.VMEM((2 * tq, LANES), F32),
        ],
        compiler_params=pltpu.CompilerParams(
            dimension_semantics=("arbitrary",),
            vmem_limit_bytes=VMEM_LIMIT_BYTES),
        name="diff_attention_latent",
    )(*lams, q, k, v, gz, subln_g)


def _attn_ctx_kernel(lq1_ref, lk1_ref, lq2_ref, lk2_ref, q_ref, k_ref, v_ref, gz_ref, sg_ref,
                     o_ref, qz_sc, v_sc, *, tq, lam_init):
    _widen_values(v_sc, v_ref)
    qz = _stack_q(qz_sc, q_ref, tq)
    s = lax.dot_general(qz, k_ref[0, 0], NT_DIMS, preferred_element_type=F32)
    p = jnp.exp2(s - jnp.max(s, axis=-1, keepdims=True)).astype(BF16)
    acc = jnp.dot(p, v_sc[...], preferred_element_type=F32)
    _finish_head(acc, (lq1_ref, lk1_ref, lq2_ref, lk2_ref), sg_ref, gz_ref, o_ref, tq, lam_init)


def _attention_ctx(q, k, v, gz, lams, subln_g, lam_init):
    bsz, _, lt, _ = q.shape
    tq = CTX_LEN
    blk = (lt - CTX_LEN) // tq
    lam_spec = pl.BlockSpec((1, DIFF_HEAD_DIM), lambda b, h: (0, 0))
    head_blk = pl.BlockSpec((1, 1, tq, HEAD_W), lambda b, h: (b, h, blk, 0))
    tok_blk = pl.BlockSpec((1, tq, HEAD_W), lambda b, h: (b, blk, h))
    return pl.pallas_call(
        functools.partial(_attn_ctx_kernel, tq=tq, lam_init=lam_init),
        grid=(bsz, DIFF_HEADS),
        in_specs=[lam_spec, lam_spec, lam_spec, lam_spec, head_blk, head_blk, head_blk, tok_blk,
                  pl.BlockSpec((1, HEAD_W), lambda b, h: (0, 0))],
        out_specs=tok_blk,
        out_shape=jax.ShapeDtypeStruct((bsz, lt, W_B), BF16),
        input_output_aliases={GZ_ARG: 0},
        scratch_shapes=[
            pltpu.VMEM((2 * tq, HEAD_W), BF16),
            pltpu.VMEM((tq, 2 * HEAD_W), BF16),
        ],
        compiler_params=pltpu.CompilerParams(
            dimension_semantics=("arbitrary", "arbitrary"),
            vmem_limit_bytes=VMEM_LIMIT_BYTES),
        name="diff_attention_ctx",
    )(*lams, q, k, v, gz, subln_g)


N_OUTPROJ_IN = 12


def _outproj_kernel(*refs, tm, lat, lt, final):
    o_ref, w_sc = refs[N_OUTPROJ_IN:]
    _cast_weights_once(w_sc, refs[10])
    o_ref[0] = _outproj_body(*refs[:N_OUTPROJ_IN], w_sc, tm=tm, lat=lat, lt=lt, final=final)


def _outproj_body(x_ref, mod_ref, ya_ref, yb_ref, p_ref, pprev_ref, pnext_ref, gc_ref,
                  cw_ref, cb_ref, w_ref, fg_ref, w_sc, *, tm, lat, lt, final):
    del w_ref
    i = pl.program_id(1)
    row = i * tm + lax.broadcasted_iota(jnp.int32, (tm, 1), 0)
    local = lax.broadcasted_iota(jnp.int32, (tm, 1), 0)
    p = p_ref[0].astype(F32)
    prev_row = pprev_ref[0, HALO - 1:HALO, :].astype(F32)
    next_row = pnext_ref[0, 0:1, :].astype(F32)
    up = jnp.where(local == 0, prev_row, pltpu.roll(p, 1, 0))
    dn = jnp.where(local == tm - 1, next_row, pltpu.roll(p, tm - 1, 0))
    up = jnp.where((row == 0) | (row == lat), 0.0, up)
    dn = jnp.where((row == lat - 1) | (row == lt - 1), 0.0, dn)
    cw = cw_ref[...]
    conv = up * cw[0:1] + p * cw[1:2] + dn * cw[2:3] + cb_ref[...]
    yc = (gc_ref[0].astype(F32) * conv).astype(BF16)

    y = jnp.dot(ya_ref[0], w_sc[0:W_A], preferred_element_type=F32)
    y = y + jnp.dot(yb_ref[0], w_sc[W_A:W_A + W_B], preferred_element_type=F32)
    y = y + jnp.dot(yc, w_sc[W_A + W_B:D_MIX], preferred_element_type=F32)

    mod = mod_ref[0]
    gate = jnp.where(row >= lat, mod[5:6], mod[2:3])
    x = x_ref[0] + gate * y
    if final:
        x = x * lax.rsqrt(jnp.mean(x * x, axis=-1, keepdims=True) + EPS) * fg_ref[...]
    return x


def _midproj_kernel(*refs, tm, lat, lt):
    out_in = refs[:N_OUTPROJ_IN]
    in_in = refs[N_OUTPROJ_IN:N_OUTPROJ_IN + N_INPROJ_IN]
    x_ref = refs[N_OUTPROJ_IN + N_INPROJ_IN]
    in_out = refs[N_OUTPROJ_IN + N_INPROJ_IN + 1:N_OUTPROJ_IN + N_INPROJ_IN + 1 + N_INPROJ_OUT]
    wo_sc, wi_sc = refs[-2:]
    _cast_weights_once(wo_sc, out_in[10])
    _cast_weights_once(wi_sc, in_in[2])
    x = _outproj_body(*out_in, wo_sc, tm=tm, lat=lat, lt=lt, final=False)
    x_ref[0] = x
    _inproj_body(x, *in_in, *in_out, wi_sc, tm=tm, lat=lat)


def _outproj_specs(lt, tm, layer):
    hpt = tm // HALO
    n_halo = lt // HALO
    prev = pl.BlockSpec((1, HALO, W_C),
                        lambda b, i: (b, jnp.maximum(i * hpt - 1, 0), 0))
    nxt = pl.BlockSpec((1, HALO, W_C),
                       lambda b, i: (b, jnp.minimum((i + 1) * hpt, n_halo - 1), 0))
    in_specs = [
        _tok_spec(tm, D_MODEL),
        pl.BlockSpec((1, MOD_ROWS, D_MODEL), lambda b, i: (b, 0, 0)),
        _tok_spec(tm, W_A), _tok_spec(tm, W_B), _tok_spec(tm, W_C), prev, nxt, _tok_spec(tm, W_C),
        _const_spec((3, W_C)), _const_spec((1, W_C)),
        pl.BlockSpec((1, D_MIX, D_MODEL), lambda b, i: (layer, 0, 0)), _const_spec((1, D_MODEL)),
    ]
    assert len(in_specs) == N_OUTPROJ_IN
    return in_specs


def _outproj_final(xs, mod_l, ya, yb, p, gc, conv_w, conv_b, w_out, layer, final_g):
    bsz, lt, _ = xs.shape
    lat = lt - CTX_LEN
    tm = FINAL_OUT_TILE
    return pl.pallas_call(
        functools.partial(_outproj_kernel, tm=tm, lat=lat, lt=lt, final=True),
        grid=(bsz, lat // tm),
        in_specs=_outproj_specs(lt, tm, layer),
        out_specs=_tok_spec(tm, D_MODEL),
        out_shape=jax.ShapeDtypeStruct((bsz, lat, D_MODEL), F32),
        scratch_shapes=[pltpu.VMEM((D_MIX, D_MODEL), BF16)],
        compiler_params=pltpu.CompilerParams(
            dimension_semantics=("arbitrary", "arbitrary"),
            vmem_limit_bytes=VMEM_LIMIT_BYTES),
        name="out_projection",
    )(xs, mod_l, ya, yb, p, p, p, gc, conv_w, conv_b, w_out, final_g)


def _midproj(xs, out_args, in_args, layer):
    bsz, lt, _ = xs.shape
    tm = MID_TILE
    mod_prev, ya, yb, p, gc, conv_w, conv_b, w_out, final_g = out_args
    in_specs, out_specs, out_shape = _inproj_specs(bsz, lt, tm, layer)
    return pl.pallas_call(
        functools.partial(_midproj_kernel, tm=tm, lat=lt - CTX_LEN, lt=lt),
        grid=(bsz, lt // tm),
        in_specs=_outproj_specs(lt, tm, layer - 1) + in_specs,
        out_specs=[_tok_spec(tm, D_MODEL)] + out_specs,
        out_shape=[jax.ShapeDtypeStruct((bsz, lt, D_MODEL), F32)] + out_shape,
        scratch_shapes=[pltpu.VMEM((D_MIX, D_MODEL), BF16), pltpu.VMEM((D_MODEL, D_IN), BF16)],
        compiler_params=pltpu.CompilerParams(
            dimension_semantics=("arbitrary", "arbitrary"),
            vmem_limit_bytes=VMEM_LIMIT_BYTES),
        name="mid_projection",
    )(xs, mod_prev, ya, yb, p, p, p, gc, conv_w, conv_b, w_out, final_g, *in_args)


def _rope_tables(length):
    t = jnp.arange(length)
    pos = jnp.stack([t // GRID_W, t % GRID_W], axis=1).astype(F32)
    inv = ROPE_BASE ** (-jnp.arange(ROPE_PAIRS, dtype=F32) / ROPE_PAIRS)
    ang = pos[:, :, None] * inv

    def to_lanes(tab):
        tab = jnp.broadcast_to(tab[:, None, :, None, :], (length, 2, 2, 2, ROPE_PAIRS))
        return tab.reshape(length, HEAD_W)

    cos = to_lanes(jnp.cos(ang))
    sin = to_lanes(jnp.sin(ang))
    second = (jnp.arange(HEAD_W)[None, :] % (2 * ROPE_PAIRS)) >= ROPE_PAIRS
    sin_lo = jnp.where(second, 0.0, -sin)
    sin_hi = jnp.where(second, sin, 0.0)
    ident = jnp.ones((CTX_LEN, HEAD_W), F32)
    zeros = jnp.zeros((CTX_LEN, HEAD_W), F32)
    return (jnp.concatenate([cos, ident]), jnp.concatenate([sin_lo, zeros]),
            jnp.concatenate([sin_hi, zeros]))


def kernel(x, c, ctx, c_ctx, w_mod, b_mod, norm_g, w_in, w_out, sgu_norm_g, sgu_w, sgu_b,
           lambda_q1, lambda_k1, lambda_q2, lambda_k2, subln_g, conv_w, conv_b, final_g):
    bsz, length, _ = x.shape
    assert ctx.shape[1] == CTX_LEN and bsz + 1 <= MOD_ROWS
    lt = CTX_LEN + length
    assert lt % IN_TILE == 0 and lt % MID_TILE == 0 and IN_TILE % CTX_LEN == 0
    assert length % FINAL_OUT_TILE == 0 and length % Q_TILE == 0 and length % Q_TILE_BOUNDED == 0
    assert length % CTX_LEN == 0 and lt % K_TILE == 0

    cond = jnp.zeros((MOD_ROWS, D_MODEL), F32).at[:bsz].set(c).at[bsz].set(c_ctx)
    mod = _modulation(cond, w_mod, b_mod)
    mod_b = jnp.transpose(mod[:, :, :bsz], (0, 2, 1, 3))
    mod_c = jnp.broadcast_to(mod[:, None, :, bsz], (DEPTH, bsz, 3, D_MODEL))
    pad = jnp.zeros((DEPTH, bsz, MOD_ROWS - 6, D_MODEL), F32)
    mod_rows = jnp.concatenate([mod_b, mod_c, pad], axis=2)

    cos, slo, shi = _rope_tables(length)
    ws = sgu_w.reshape(DEPTH, GMLP_HEADS * GMLP_CHUNK, GMLP_CHUNK).astype(BF16)
    bs = jnp.repeat(jnp.transpose(sgu_b, (0, 2, 1)), GMLP_HEAD_DIM, axis=2)

    half_ind = (jnp.arange(W_B)[:, None] // DIFF_HEAD_DIM == jnp.arange(LANES)[None, :]).astype(BF16)
    n_halves = 2 * DIFF_HEADS

    def norm_bounds(n):
        n = jnp.max(n, axis=(1, 2))[:, :n_halves].reshape(bsz, DIFF_HEADS, 2)
        return jnp.sqrt(n) * NORM_SLACK

    def inproj_args(l):
        return (mod_rows[l], norm_g[l][None], w_in, cos, slo, shi,
                sgu_norm_g[l][None], ws[l], bs[l], half_ind)

    xs, *proj = _inproj_first(x, ctx, inproj_args(0), 0)
    for l in range(DEPTH):
        lam_init = 0.8 - 0.6 * math.exp(-0.3 * l)
        ya, q, k, v, gz, p, gc, qn, kn = proj
        lams = (lambda_q1[l][None], lambda_k1[l][None], lambda_q2[l][None], lambda_k2[l][None])
        yb = _attention_ctx(q, k, v, gz, lams, subln_g[l][None], lam_init)
        q_norm, k_norm = norm_bounds(qn), norm_bounds(kn)
        k_bound = jnp.zeros((bsz, DIFF_HEADS, MOD_ROWS, LANES), F32).at[:, :, 0:2, :].set(
            jnp.broadcast_to(k_norm[..., None], (bsz, DIFF_HEADS, 2, LANES)))
        bounded = jnp.max(q_norm * k_norm) <= MAX_SCORE_BOUND
        yb = lax.cond(
            bounded,
            lambda q, k, v, yb, kb: _attention_bounded(
                q, k, v, yb, lams, subln_g[l][None], kb, lam_init,
                tk=K_TILE, kt=K_TILE),
            lambda q, k, v, yb, kb: _attention_pipelined(
                q, k, v, yb, lams, subln_g[l][None], lam_init,
                tk=K_TILE, kt=V_TILE),
            q, k, v, yb, k_bound)
        out_args = (mod_rows[l], ya, yb, p, gc, conv_w[l], conv_b[l][None], w_out, final_g[None])
        if l + 1 < DEPTH:
            xs, *proj = _midproj(xs, out_args, inproj_args(l + 1), l + 1)
        else:
            xs = _outproj_final(xs, *out_args[:-1], l, out_args[-1])
    return xs
```

```python
import functools
import math

import jax
import jax.numpy as jnp
from jax import lax
from jax.experimental import pallas as pl
from jax.experimental.pallas import tpu as pltpu

D_MODEL = 1024
DEPTH = 4
CTX_LEN = 256
GRID_W = 64

GMLP_HEADS = 4
GMLP_HEAD_DIM = 64
GMLP_CHUNK = 128
W_A = GMLP_HEADS * GMLP_HEAD_DIM
DIFF_HEADS = 4
DIFF_HEAD_DIM = 64
HEAD_W = 2 * DIFF_HEAD_DIM
W_B = DIFF_HEADS * HEAD_W
W_C = 256
D_MIX = W_A + W_B + W_C
D_IN = 3 * W_A + 4 * W_B + 4 * W_C

COL_A = 0
COL_Q = 3 * W_A
COL_K = COL_Q + W_B
COL_V = COL_K + W_B
COL_C = COL_V + 2 * W_B

ROPE_BASE = 10000.0
ROPE_PAIRS = DIFF_HEAD_DIM // 4
EPS = 1e-6

F32 = jnp.float32
BF16 = jnp.bfloat16

VMEM_LIMIT_BYTES = 56 * 1024 * 1024
MOD_ROWS = 8
LANES = 128

IN_TILE = 768
MID_TILE = 384
FINAL_OUT_TILE = 512
Q_TILE = 256
Q_TILE_BOUNDED = 512
K_TILE = 2816
V_TILE = 256
HALO = 16


def _silu(x):
    return x * jax.nn.sigmoid(x)


def _mod_kernel(cond_ref, w_ref, b_ref, o_ref):
    a = _silu(cond_ref[...])
    o_ref[0, 0] = jnp.dot(a, w_ref[0], preferred_element_type=F32) + b_ref[0, 0]


def _modulation(cond, w_mod, b_mod):
    return pl.pallas_call(
        _mod_kernel,
        grid=(DEPTH, 3),
        in_specs=[
            pl.BlockSpec((MOD_ROWS, D_MODEL), lambda l, j: (0, 0)),
            pl.BlockSpec((1, D_MODEL, D_MODEL), lambda l, j: (l, 0, j)),
            pl.BlockSpec((1, 1, 1, D_MODEL), lambda l, j: (l, j, 0, 0)),
        ],
        out_specs=pl.BlockSpec((1, 1, MOD_ROWS, D_MODEL), lambda l, j: (l, j, 0, 0)),
        out_shape=jax.ShapeDtypeStruct((DEPTH, 3, MOD_ROWS, D_MODEL), F32),
        compiler_params=pltpu.CompilerParams(
            dimension_semantics=("arbitrary", "arbitrary"),
            vmem_limit_bytes=VMEM_LIMIT_BYTES),
        name="adaln_modulation",
    )(cond, w_mod, b_mod.reshape(DEPTH, 3, 1, D_MODEL))


def _rope(t, cos, sin_lo, sin_hi):
    return (t * cos + pltpu.roll(t, HEAD_W - ROPE_PAIRS, 1) * sin_lo
            + pltpu.roll(t, ROPE_PAIRS, 1) * sin_hi)


def _cast_weights_once(w_sc, w_ref):
    @pl.when((pl.program_id(0) == 0) & (pl.program_id(1) == 0))
    def _():
        w_sc[...] = w_ref[0].astype(BF16)


N_INPROJ_IN = 10
N_INPROJ_OUT = 9


def _inproj_kernel(*refs, tm, lat):
    n_sub = tm // CTX_LEN
    sub_refs, ctx_ref = refs[:n_sub], refs[n_sub]
    in_refs = refs[n_sub + 1:n_sub + 1 + N_INPROJ_IN]
    xs_ref = refs[n_sub + 1 + N_INPROJ_IN]
    rest = refs[n_sub + 2 + N_INPROJ_IN:]
    _cast_weights_once(rest[-1], in_refs[2])
    is_last = pl.program_id(1) == pl.num_programs(1) - 1
    tail = jnp.where(is_last, ctx_ref[0], sub_refs[-1][0])
    x = jnp.concatenate([r[0] for r in sub_refs[:-1]] + [tail], axis=0)
    xs_ref[0] = x
    _inproj_body(x, *in_refs, *rest, tm=tm, lat=lat)


def _inproj_body(x, mod_ref, g_ref, w_ref, cos_ref, slo_ref, shi_ref,
                 sgug_ref, ws_ref, bs_ref, half_ref,
                 ya_ref, q_ref, k_ref, v_ref, gz_ref, p_ref, gc_ref, qn_ref, kn_ref,
                 w_sc, *, tm, lat):
    del w_ref
    i = pl.program_id(1)
    row = i * tm + lax.broadcasted_iota(jnp.int32, (tm, 1), 0)
    is_ctx = row >= lat
    mod = mod_ref[0]
    shift = jnp.where(is_ctx, mod[3:4], mod[0:1])
    scale = jnp.where(is_ctx, mod[4:5], mod[1:2])
    h = x * lax.rsqrt(jnp.mean(x * x, axis=-1, keepdims=True) + EPS) * g_ref[...]
    h = (h * (1.0 + scale) + shift).astype(BF16)

    ra = jnp.dot(h, w_sc[:, COL_A:COL_A + 3 * W_A], preferred_element_type=F32)
    u = jax.nn.gelu(ra[:, 0:W_A])
    v = jax.nn.gelu(ra[:, W_A:2 * W_A])
    gate_a = _silu(ra[:, 2 * W_A:3 * W_A])
    vn = v * lax.rsqrt(jnp.mean(v * v, axis=-1, keepdims=True) + EPS) * sgug_ref[...]
    vn = vn.astype(BF16)
    lane = lax.broadcasted_iota(jnp.int32, (GMLP_CHUNK, W_A), 1)
    for c in range(tm // GMLP_CHUNK):
        rows = slice(c * GMLP_CHUNK, (c + 1) * GMLP_CHUNK)
        r = jnp.dot(ws_ref[...], vn[rows], preferred_element_type=F32)
        mixed = r[0:GMLP_CHUNK]
        for hd in range(1, GMLP_HEADS):
            mixed = jnp.where(lane >= hd * GMLP_HEAD_DIM,
                              r[hd * GMLP_CHUNK:(hd + 1) * GMLP_CHUNK], mixed)
        mixed = mixed + bs_ref[...]
        ya_ref[0, rows, :] = (u[rows] * mixed * gate_a[rows]).astype(BF16)

    cos, slo, shi = cos_ref[...], slo_ref[...], shi_ref[...]
    rq = jnp.dot(h, w_sc[:, COL_Q:COL_Q + W_B], preferred_element_type=F32)
    rk = jnp.dot(h, w_sc[:, COL_K:COL_K + W_B], preferred_element_type=F32)
    q_scale = math.log2(math.e) / math.sqrt(DIFF_HEAD_DIM)
    q_heads, k_heads = [], []
    for hd in range(DIFF_HEADS):
        cols = slice(hd * HEAD_W, (hd + 1) * HEAD_W)
        q_heads.append(_rope(rq[:, cols], cos, slo, shi) * q_scale)
        k_heads.append(_rope(rk[:, cols], cos, slo, shi))
        q_ref[0, hd] = q_heads[hd].astype(BF16)
        k_ref[0, hd] = k_heads[hd].astype(BF16)
    for heads, n_ref in ((q_heads, qn_ref), (k_heads, kn_ref)):
        t = jnp.concatenate(heads, axis=1)
        n = jnp.dot((t * t).astype(BF16), half_ref[...], preferred_element_type=F32)
        best = n[0:MOD_ROWS]
        for r in range(1, tm // MOD_ROWS):
            best = jnp.maximum(best, n[r * MOD_ROWS:(r + 1) * MOD_ROWS])
        n_ref[0, 0] = best
    rz = jnp.dot(h, w_sc[:, COL_V + W_B:COL_V + 2 * W_B], preferred_element_type=F32)
    gz_ref[0] = _silu(rz).astype(BF16)

    rc = jnp.dot(h, w_sc[:, COL_C:COL_C + 4 * W_C], preferred_element_type=F32)
    p_ref[0] = (rc[:, W_C:2 * W_C] * rc[:, 2 * W_C:3 * W_C]).astype(BF16)
    gc_ref[0] = (rc[:, 0:W_C] * _silu(rc[:, 3 * W_C:4 * W_C])).astype(BF16)

    rv = jnp.dot(h, w_sc[:, COL_V:COL_V + W_B], preferred_element_type=F32)
    for hd in range(DIFF_HEADS):
        v_ref[0, hd] = rv[:, hd * HEAD_W:(hd + 1) * HEAD_W].astype(BF16)


def _const_spec(shape):
    return pl.BlockSpec(shape, lambda b, i: (0,) * len(shape))


def _tok_spec(tm, width):
    return pl.BlockSpec((1, tm, width), lambda b, i: (b, i, 0))


def _inproj_specs(bsz, lt, tm, layer):
    norm_blk = pl.BlockSpec((1, 1, MOD_ROWS, LANES), lambda b, i: (b, i, 0, 0))
    norm_shape = jax.ShapeDtypeStruct((bsz, lt // tm, MOD_ROWS, LANES), F32)
    head = pl.BlockSpec((1, DIFF_HEADS, tm, HEAD_W), lambda b, i: (b, 0, i, 0))
    tab = pl.BlockSpec((tm, HEAD_W), lambda b, i: (i, 0))
    tok_shape = lambda width: jax.ShapeDtypeStruct((bsz, lt, width), BF16)
    head_shape = jax.ShapeDtypeStruct((bsz, DIFF_HEADS, lt, HEAD_W), BF16)
    in_specs = [
        pl.BlockSpec((1, MOD_ROWS, D_MODEL), lambda b, i: (b, 0, 0)),
        _const_spec((1, D_MODEL)),
        pl.BlockSpec((1, D_MODEL, D_IN), lambda b, i: (layer, 0, 0)),
        tab, tab, tab,
        _const_spec((1, W_A)),
        _const_spec((GMLP_HEADS * GMLP_CHUNK, GMLP_CHUNK)),
        _const_spec((GMLP_CHUNK, W_A)),
        _const_spec((W_B, LANES)),
    ]
    out_specs = [_tok_spec(tm, W_A), head, head, head, _tok_spec(tm, W_B), _tok_spec(tm, W_C),
                 _tok_spec(tm, W_C), norm_blk, norm_blk]
    out_shape = [tok_shape(W_A), head_shape, head_shape, head_shape,
                 tok_shape(W_B), tok_shape(W_C), tok_shape(W_C), norm_shape, norm_shape]
    assert len(in_specs) == N_INPROJ_IN and len(out_specs) == N_INPROJ_OUT
    return in_specs, out_specs, out_shape


def _inproj_first(x, ctx, in_args, layer):
    bsz, lat, _ = x.shape
    lt = lat + CTX_LEN
    tm = IN_TILE
    n_sub = tm // CTX_LEN
    last_blk = lat // CTX_LEN - 1
    sub = [pl.BlockSpec((1, CTX_LEN, D_MODEL),
                        lambda b, i, j=j: (b, jnp.minimum(i * n_sub + j, last_blk), 0))
           for j in range(n_sub)]
    ctx_spec = pl.BlockSpec((1, CTX_LEN, D_MODEL), lambda b, i: (b, 0, 0))
    in_specs, out_specs, out_shape = _inproj_specs(bsz, lt, tm, layer)
    return pl.pallas_call(
        functools.partial(_inproj_kernel, tm=tm, lat=lat),
        grid=(bsz, lt // tm),
        in_specs=sub + [ctx_spec] + in_specs,
        out_specs=[_tok_spec(tm, D_MODEL)] + out_specs,
        out_shape=[jax.ShapeDtypeStruct((bsz, lt, D_MODEL), F32)] + out_shape,
        scratch_shapes=[pltpu.VMEM((D_MODEL, D_IN), BF16)],
        compiler_params=pltpu.CompilerParams(
            dimension_semantics=("arbitrary", "arbitrary"),
            vmem_limit_bytes=VMEM_LIMIT_BYTES),
        name="in_projection",
    )(*([x] * n_sub), ctx, *in_args)


NT_DIMS = (((1,), (1,)), ((), ()))
NORM_SLACK = 1.02
MAX_SCORE_BOUND = 50.0
GZ_ARG = 7


def _widen_values(v_sc, v_ref):
    v_sc[:, 0:HEAD_W] = v_ref[0, 0]
    v_sc[:, HEAD_W:2 * HEAD_W] = jnp.ones((v_sc.shape[0], HEAD_W), BF16)


def _stack_q(qz_sc, q_ref, tq):
    q = q_ref[0, 0]
    lane = lax.broadcasted_iota(jnp.int32, (tq, HEAD_W), 1)
    zero = jnp.zeros_like(q)
    qz_sc[0:tq] = jnp.where(lane < DIFF_HEAD_DIM, q, zero)
    qz_sc[tq:2 * tq] = jnp.where(lane >= DIFF_HEAD_DIM, q, zero)
    return qz_sc[...]


def _finish_head(acc, lam_refs, sg_ref, gz_ref, o_ref, tq, lam_init):
    lq1_ref, lk1_ref, lq2_ref, lk2_ref = lam_refs
    o = acc[:, 0:HEAD_W] / acc[:, HEAD_W:2 * HEAD_W]
    lam = (jnp.exp(jnp.sum(lq1_ref[...] * lk1_ref[...]))
           - jnp.exp(jnp.sum(lq2_ref[...] * lk2_ref[...])) + lam_init)
    o = o[0:tq] - lam * o[tq:2 * tq]
    o = o * lax.rsqrt(jnp.mean(o * o, axis=-1, keepdims=True) + EPS) * sg_ref[...]
    o_ref[0] = (o * (1.0 - lam_init) * gz_ref[0].astype(F32)).astype(BF16)


def _pipeline_maps(bsz, n_q):
    n_tiles = bsz * DIFF_HEADS * n_q

    def score_tile(g):
        t = jnp.minimum(g, n_tiles - 1)
        return t // (DIFF_HEADS * n_q), (t // n_q) % DIFF_HEADS, t % n_q

    def value_tile(g):
        return score_tile(jnp.maximum(g - 1, 0))

    def score_head(g):
        b, h, _ = score_tile(g)
        return (b, h, 0, 0)

    def score_rows(g):
        b, h, i = score_tile(g)
        return (b, h, i, 0)

    def value_head(g):
        b, h, _ = value_tile(g)
        return (b, h, 0, 0)

    def value_tok(g):
        b, h, i = value_tile(g)
        return (b, i, h)

    return n_tiles, score_head, score_rows, value_head, value_tok


def _attn_bounded_kernel(lq1_ref, lk1_ref, lq2_ref, lk2_ref, q_ref, k_ref, v_ref, gz_ref, sg_ref,
                         kb_ref, o_ref, qz_sc, e0_sc, e1_sc, l0_sc, l1_sc,
                         *, tq, tk, kt, n_tiles, lam_init):
    g = pl.program_id(0)
    n_keys = e0_sc.shape[1]

    def scores(e_w, l_w):
        qz = _stack_q(qz_sc, q_ref, tq)
        qf = qz.astype(F32)
        q_norm = jnp.sqrt(jnp.sum(qf * qf, axis=-1, keepdims=True))
        row = lax.broadcasted_iota(jnp.int32, (2 * tq, 1), 0)
        kb = kb_ref[0, 0]
        bound = q_norm * jnp.where(row < tq, kb[0:1], kb[1:2])
        l = None
        for c in range(n_keys // tk):
            s = lax.dot_general(qz, k_ref[0, 0, c * tk:(c + 1) * tk, :], NT_DIMS,
                                preferred_element_type=F32)
            for t in range(tk // LANES):
                e = jnp.exp2(s[:, t * LANES:(t + 1) * LANES] - bound)
                l = e if l is None else l + e
                e_w[:, c * tk + t * LANES:c * tk + (t + 1) * LANES] = e.astype(BF16)
        l_w[...] = jnp.broadcast_to(jnp.sum(l, axis=-1, keepdims=True), l_w.shape)

    def values(e_r, l_r):
        l_prev = l_r[...]
        lam = (jnp.exp(jnp.sum(lq1_ref[...] * lk1_ref[...]))
               - jnp.exp(jnp.sum(lq2_ref[...] * lk2_ref[...])) + lam_init)
        ratio = (lam * l_prev[0:tq] / l_prev[tq:2 * tq]).astype(BF16)
        ratio = jnp.concatenate([ratio] * (kt // LANES), axis=1)
        acc = None
        for c in range(n_keys // kt):
            a = e_r[0:tq, c * kt:(c + 1) * kt] - ratio * e_r[tq:2 * tq, c * kt:(c + 1) * kt]
            vv = v_ref[0, 0, c * kt:(c + 1) * kt, :]
            d = jnp.dot(a, jnp.concatenate([vv, vv], axis=1), preferred_element_type=F32)
            acc = d if acc is None else acc + d
        o = acc[:, 0:HEAD_W] / l_prev[0:tq]
        o = o * lax.rsqrt(jnp.mean(o * o, axis=-1, keepdims=True) + EPS) * sg_ref[...]
        o_ref[0] = (o * (1.0 - lam_init) * gz_ref[0].astype(F32)).astype(BF16)

    bufs = ((e0_sc, l0_sc), (e1_sc, l1_sc))
    first, last = g == 0, g == n_tiles

    @pl.when(first)
    def _():
        scores(*bufs[0])

    for parity in range(2):
        @pl.when((g % 2 == parity) & jnp.logical_not(first) & jnp.logical_not(last))
        def _():
            scores(*bufs[parity])
            values(*bufs[1 - parity])

    @pl.when(last)
    def _():
        values(*bufs[(n_tiles - 1) % 2])


def _attention_bounded(q, k, v, gz, lams, subln_g, k_bound, lam_init, *, tk, kt):
    bsz, _, lt, _ = q.shape
    tq = Q_TILE_BOUNDED
    n_q = (lt - CTX_LEN) // tq
    n_tiles, score_head, score_rows, value_head, value_tok = _pipeline_maps(bsz, n_q)
    lam_spec = pl.BlockSpec((1, DIFF_HEAD_DIM), lambda g: (0, 0))
    seq_blk = (1, 1, lt, HEAD_W)
    return pl.pallas_call(
        functools.partial(_attn_bounded_kernel, tq=tq, tk=tk, kt=kt, n_tiles=n_tiles,
                          lam_init=lam_init),
        grid=(n_tiles + 1,),
        in_specs=[
            lam_spec, lam_spec, lam_spec, lam_spec,
            pl.BlockSpec((1, 1, tq, HEAD_W), score_rows),
            pl.BlockSpec(seq_blk, score_head),
            pl.BlockSpec(seq_blk, value_head),
            pl.BlockSpec((1, tq, HEAD_W), value_tok),
            pl.BlockSpec((1, HEAD_W), lambda g: (0, 0)),
            pl.BlockSpec((1, 1, MOD_ROWS, LANES), score_head),
        ],
        out_specs=pl.BlockSpec((1, tq, HEAD_W), value_tok),
        out_shape=jax.ShapeDtypeStruct((bsz, lt, W_B), BF16),
        input_output_aliases={GZ_ARG: 0},
        scratch_shapes=[
            pltpu.VMEM((2 * tq, HEAD_W), BF16),
            pltpu.VMEM((2 * tq, lt), BF16),
            pltpu.VMEM((2 * tq, lt), BF16),
            pltpu.VMEM((2 * tq, LANES), F32),
            pltpu.VMEM((2 * tq, LANES), F32),
        ],
        compiler_params=pltpu.CompilerParams(
            dimension_semantics=("arbitrary",),
            vmem_limit_bytes=VMEM_LIMIT_BYTES),
        name="diff_attention_bounded",
    )(*lams, q, k, v, gz, subln_g, k_bound)


def _attn_pipe_kernel(lq1_ref, lk1_ref, lq2_ref, lk2_ref, q_ref, k_ref, v_ref, gz_ref, sg_ref,
                      o_ref, qz_sc, v_sc, s0_sc, s1_sc, m0_sc, m1_sc,
                      *, tq, tk, kt, n_q, lam_init):
    g = pl.program_id(0)
    n_keys = s0_sc.shape[1]

    @pl.when(g == 0)
    def _():
        s1_sc[...] = jnp.zeros_like(s1_sc)
        m1_sc[...] = jnp.zeros_like(m1_sc)

    @pl.when((g == 0) | ((g - 1) % n_q == 0))
    def _():
        _widen_values(v_sc, v_ref)

    def step(s_w, m_w, s_r, m_r):
        qz = _stack_q(qz_sc, q_ref, tq)
        m = None
        for c in range(n_keys // tk):
            s = lax.dot_general(qz, k_ref[0, 0, c * tk:(c + 1) * tk, :], NT_DIMS,
                                preferred_element_type=F32)
            s_w[:, c * tk:(c + 1) * tk] = s
            for t in range(tk // LANES):
                blk = s[:, t * LANES:(t + 1) * LANES]
                m = blk if m is None else jnp.maximum(m, blk)
        m_w[...] = jnp.broadcast_to(jnp.max(m, axis=-1, keepdims=True), m_w.shape)

        m_prev = jnp.concatenate([m_r[...]] * (kt // LANES), axis=1)
        acc = None
        for c in range(n_keys // kt):
            p = jnp.exp2(s_r[:, c * kt:(c + 1) * kt] - m_prev).astype(BF16)
            d = jnp.dot(p, v_sc[c * kt:(c + 1) * kt, :], preferred_element_type=F32)
            acc = d if acc is None else acc + d
        _finish_head(acc, (lq1_ref, lk1_ref, lq2_ref, lk2_ref), sg_ref, gz_ref, o_ref,
                     tq, lam_init)

    @pl.when(g % 2 == 0)
    def _():
        step(s0_sc, m0_sc, s1_sc, m1_sc)

    @pl.when(g % 2 == 1)
    def _():
        step(s1_sc, m1_sc, s0_sc, m0_sc)


def _attention_pipelined(q, k, v, gz, lams, subln_g, lam_init, *, tk, kt):
    bsz, _, lt, _ = q.shape
    tq = Q_TILE
    n_q = (lt - CTX_LEN) // tq
    n_tiles, score_head, score_rows, value_head, value_tok = _pipeline_maps(bsz, n_q)
    lam_spec = pl.BlockSpec((1, DIFF_HEAD_DIM), lambda g: (0, 0))
    seq_blk = (1, 1, lt, HEAD_W)
    return pl.pallas_call(
        functools.partial(_attn_pipe_kernel, tq=tq, tk=tk, kt=kt, n_q=n_q, lam_init=lam_init),
        grid=(n_tiles + 1,),
        in_specs=[
            lam_spec, lam_spec, lam_spec, lam_spec,
            pl.BlockSpec((1, 1, tq, HEAD_W), score_rows),
            pl.BlockSpec(seq_blk, score_head),
            pl.BlockSpec(seq_blk, value_head),
            pl.BlockSpec((1, tq, HEAD_W), value_tok),
            pl.BlockSpec((1, HEAD_W), lambda g: (0, 0)),
        ],
        out_specs=pl.BlockSpec((1, tq, HEAD_W), value_tok),
        out_shape=jax.ShapeDtypeStruct((bsz, lt, W_B), BF16),
        input_output_aliases={GZ_ARG: 0},
        scratch_shapes=[
            pltpu.VMEM((2 * tq, HEAD_W), BF16),
            pltpu.VMEM((lt, 2 * HEAD_W), BF16),
            pltpu.VMEM((2 * tq, lt), F32),
            pltpu.VMEM((2 * tq, lt), F32),
            pltpu.VMEM((2 * tq, LANES), F32),
            pltpu.VMEM((2 * tq, LANES), F32),
        ],
        compiler_params=pltpu.CompilerParams(
            dimension_semantics=("arbitrary",),
            vmem_limit_bytes=VMEM_LIMIT_BYTES),
        name="diff_attention_latent",
    )(*lams, q, k, v, gz, subln_g)


def _attn_ctx_kernel(lq1_ref, lk1_ref, lq2_ref, lk2_ref, q_ref, k_ref, v_ref, gz_ref, sg_ref,
                     o_ref, qz_sc, v_sc, *, tq, lam_init):
    _widen_values(v_sc, v_ref)
    qz = _stack_q(qz_sc, q_ref, tq)
    s = lax.dot_general(qz, k_ref[0, 0], NT_DIMS, preferred_element_type=F32)
    p = jnp.exp2(s - jnp.max(s, axis=-1, keepdims=True)).astype(BF16)
    acc = jnp.dot(p, v_sc[...], preferred_element_type=F32)
    _finish_head(acc, (lq1_ref, lk1_ref, lq2_ref, lk2_ref), sg_ref, gz_ref, o_ref, tq, lam_init)


def _attention_ctx(q, k, v, gz, lams, subln_g, lam_init):
    bsz, _, lt, _ = q.shape
    tq = CTX_LEN
    blk = (lt - CTX_LEN) // tq
    lam_spec = pl.BlockSpec((1, DIFF_HEAD_DIM), lambda b, h: (0, 0))
    head_blk = pl.BlockSpec((1, 1, tq, HEAD_W), lambda b, h: (b, h, blk, 0))
    tok_blk = pl.BlockSpec((1, tq, HEAD_W), lambda b, h: (b, blk, h))
    return pl.pallas_call(
        functools.partial(_attn_ctx_kernel, tq=tq, lam_init=lam_init),
        grid=(bsz, DIFF_HEADS),
        in_specs=[lam_spec, lam_spec, lam_spec, lam_spec, head_blk, head_blk, head_blk, tok_blk,
                  pl.BlockSpec((1, HEAD_W), lambda b, h: (0, 0))],
        out_specs=tok_blk,
        out_shape=jax.ShapeDtypeStruct((bsz, lt, W_B), BF16),
        input_output_aliases={GZ_ARG: 0},
        scratch_shapes=[
            pltpu.VMEM((2 * tq, HEAD_W), BF16),
            pltpu.VMEM((tq, 2 * HEAD_W), BF16),
        ],
        compiler_params=pltpu.CompilerParams(
            dimension_semantics=("arbitrary", "arbitrary"),
            vmem_limit_bytes=VMEM_LIMIT_BYTES),
        name="diff_attention_ctx",
    )(*lams, q, k, v, gz, subln_g)


N_OUTPROJ_IN = 12


def _outproj_kernel(*refs, tm, lat, lt, final):
    o_ref, w_sc = refs[N_OUTPROJ_IN:]
    _cast_weights_once(w_sc, refs[10])
    o_ref[0] = _outproj_body(*refs[:N_OUTPROJ_IN], w_sc, tm=tm, lat=lat, lt=lt, final=final)


def _outproj_body(x_ref, mod_ref, ya_ref, yb_ref, p_ref, pprev_ref, pnext_ref, gc_ref,
                  cw_ref, cb_ref, w_ref, fg_ref, w_sc, *, tm, lat, lt, final):
    del w_ref
    i = pl.program_id(1)
    row = i * tm + lax.broadcasted_iota(jnp.int32, (tm, 1), 0)
    local = lax.broadcasted_iota(jnp.int32, (tm, 1), 0)
    p = p_ref[0].astype(F32)
    prev_row = pprev_ref[0, HALO - 1:HALO, :].astype(F32)
    next_row = pnext_ref[0, 0:1, :].astype(F32)
    up = jnp.where(local == 0, prev_row, pltpu.roll(p, 1, 0))
    dn = jnp.where(local == tm - 1, next_row, pltpu.roll(p, tm - 1, 0))
    up = jnp.where((row == 0) | (row == lat), 0.0, up)
    dn = jnp.where((row == lat - 1) | (row == lt - 1), 0.0, dn)
    cw = cw_ref[...]
    conv = up * cw[0:1] + p * cw[1:2] + dn * cw[2:3] + cb_ref[...]
    yc = (gc_ref[0].astype(F32) * conv).astype(BF16)

    y = jnp.dot(ya_ref[0], w_sc[0:W_A], preferred_element_type=F32)
    y = y + jnp.dot(yb_ref[0], w_sc[W_A:W_A + W_B], preferred_element_type=F32)
    y = y + jnp.dot(yc, w_sc[W_A + W_B:D_MIX], preferred_element_type=F32)

    mod = mod_ref[0]
    gate = jnp.where(row >= lat, mod[5:6], mod[2:3])
    x = x_ref[0] + gate * y
    if final:
        x = x * lax.rsqrt(jnp.mean(x * x, axis=-1, keepdims=True) + EPS) * fg_ref[...]
    return x


def _midproj_kernel(*refs, tm, lat, lt):
    out_in = refs[:N_OUTPROJ_IN]
    in_in = refs[N_OUTPROJ_IN:N_OUTPROJ_IN + N_INPROJ_IN]
    x_ref = refs[N_OUTPROJ_IN + N_INPROJ_IN]
    in_out = refs[N_OUTPROJ_IN + N_INPROJ_IN + 1:N_OUTPROJ_IN + N_INPROJ_IN + 1 + N_INPROJ_OUT]
    wo_sc, wi_sc = refs[-2:]
    _cast_weights_once(wo_sc, out_in[10])
    _cast_weights_once(wi_sc, in_in[2])
    x = _outproj_body(*out_in, wo_sc, tm=tm, lat=lat, lt=lt, final=False)
    x_ref[0] = x
    _inproj_body(x, *in_in, *in_out, wi_sc, tm=tm, lat=lat)


def _outproj_specs(lt, tm, layer):
    hpt = tm // HALO
    n_halo = lt // HALO
    prev = pl.BlockSpec((1, HALO, W_C),
                        lambda b, i: (b, jnp.maximum(i * hpt - 1, 0), 0))
    nxt = pl.BlockSpec((1, HALO, W_C),
                       lambda b, i: (b, jnp.minimum((i + 1) * hpt, n_halo - 1), 0))
    in_specs = [
        _tok_spec(tm, D_MODEL),
        pl.BlockSpec((1, MOD_ROWS, D_MODEL), lambda b, i: (b, 0, 0)),
        _tok_spec(tm, W_A), _tok_spec(tm, W_B), _tok_spec(tm, W_C), prev, nxt, _tok_spec(tm, W_C),
        _const_spec((3, W_C)), _const_spec((1, W_C)),
        pl.BlockSpec((1, D_MIX, D_MODEL), lambda b, i: (layer, 0, 0)), _const_spec((1, D_MODEL)),
    ]
    assert len(in_specs) == N_OUTPROJ_IN
    return in_specs


def _outproj_final(xs, mod_l, ya, yb, p, gc, conv_w, conv_b, w_out, layer, final_g):
    bsz, lt, _ = xs.shape
    lat = lt - CTX_LEN
    tm = FINAL_OUT_TILE
    return pl.pallas_call(
        functools.partial(_outproj_kernel, tm=tm, lat=lat, lt=lt, final=True),
        grid=(bsz, lat // tm),
        in_specs=_outproj_specs(lt, tm, layer),
        out_specs=_tok_spec(tm, D_MODEL),
        out_shape=jax.ShapeDtypeStruct((bsz, lat, D_MODEL), F32),
        scratch_shapes=[pltpu.VMEM((D_MIX, D_MODEL), BF16)],
        compiler_params=pltpu.CompilerParams(
            dimension_semantics=("arbitrary", "arbitrary"),
            vmem_limit_bytes=VMEM_LIMIT_BYTES),
        name="out_projection",
    )(xs, mod_l, ya, yb, p, p, p, gc, conv_w, conv_b, w_out, final_g)


def _midproj(xs, out_args, in_args, layer):
    bsz, lt, _ = xs.shape
    tm = MID_TILE
    mod_prev, ya, yb, p, gc, conv_w, conv_b, w_out, final_g = out_args
    in_specs, out_specs, out_shape = _inproj_specs(bsz, lt, tm, layer)
    return pl.pallas_call(
        functools.partial(_midproj_kernel, tm=tm, lat=lt - CTX_LEN, lt=lt),
        grid=(bsz, lt // tm),
        in_specs=_outproj_specs(lt, tm, layer - 1) + in_specs,
        out_specs=[_tok_spec(tm, D_MODEL)] + out_specs,
        out_shape=[jax.ShapeDtypeStruct((bsz, lt, D_MODEL), F32)] + out_shape,
        scratch_shapes=[pltpu.VMEM((D_MIX, D_MODEL), BF16), pltpu.VMEM((D_MODEL, D_IN), BF16)],
        compiler_params=pltpu.CompilerParams(
            dimension_semantics=("arbitrary", "arbitrary"),
            vmem_limit_bytes=VMEM_LIMIT_BYTES),
        name="mid_projection",
    )(xs, mod_prev, ya, yb, p, p, p, gc, conv_w, conv_b, w_out, final_g, *in_args)


def _rope_tables(length):
    t = jnp.arange(length)
    pos = jnp.stack([t // GRID_W, t % GRID_W], axis=1).astype(F32)
    inv = ROPE_BASE ** (-jnp.arange(ROPE_PAIRS, dtype=F32) / ROPE_PAIRS)
    ang = pos[:, :, None] * inv

    def to_lanes(tab):
        tab = jnp.broadcast_to(tab[:, None, :, None, :], (length, 2, 2, 2, ROPE_PAIRS))
        return tab.reshape(length, HEAD_W)

    cos = to_lanes(jnp.cos(ang))
    sin = to_lanes(jnp.sin(ang))
    second = (jnp.arange(HEAD_W)[None, :] % (2 * ROPE_PAIRS)) >= ROPE_PAIRS
    sin_lo = jnp.where(second, 0.0, -sin)
    sin_hi = jnp.where(second, sin, 0.0)
    ident = jnp.ones((CTX_LEN, HEAD_W), F32)
    zeros = jnp.zeros((CTX_LEN, HEAD_W), F32)
    return (jnp.concatenate([cos, ident]), jnp.concatenate([sin_lo, zeros]),
            jnp.concatenate([sin_hi, zeros]))


def kernel(x, c, ctx, c_ctx, w_mod, b_mod, norm_g, w_in, w_out, sgu_norm_g, sgu_w, sgu_b,
           lambda_q1, lambda_k1, lambda_q2, lambda_k2, subln_g, conv_w, conv_b, final_g):
    bsz, length, _ = x.shape
    assert ctx.shape[1] == CTX_LEN and bsz + 1 <= MOD_ROWS
    lt = CTX_LEN + length
    assert lt % IN_TILE == 0 and lt % MID_TILE == 0 and IN_TILE % CTX_LEN == 0
    assert length % FINAL_OUT_TILE == 0 and length % Q_TILE == 0 and length % Q_TILE_BOUNDED == 0
    assert length % CTX_LEN == 0 and lt % K_TILE == 0

    cond = jnp.zeros((MOD_ROWS, D_MODEL), F32).at[:bsz].set(c).at[bsz].set(c_ctx)
    mod = _modulation(cond, w_mod, b_mod)
    mod_b = jnp.transpose(mod[:, :, :bsz], (0, 2, 1, 3))
    mod_c = jnp.broadcast_to(mod[:, None, :, bsz], (DEPTH, bsz, 3, D_MODEL))
    pad = jnp.zeros((DEPTH, bsz, MOD_ROWS - 6, D_MODEL), F32)
    mod_rows = jnp.concatenate([mod_b, mod_c, pad], axis=2)

    cos, slo, shi = _rope_tables(length)
    ws = sgu_w.reshape(DEPTH, GMLP_HEADS * GMLP_CHUNK, GMLP_CHUNK).astype(BF16)
    bs = jnp.repeat(jnp.transpose(sgu_b, (0, 2, 1)), GMLP_HEAD_DIM, axis=2)

    half_ind = (jnp.arange(W_B)[:, None] // DIFF_HEAD_DIM == jnp.arange(LANES)[None, :]).astype(BF16)
    n_halves = 2 * DIFF_HEADS

    def norm_bounds(n):
        n = jnp.max(n, axis=(1, 2))[:, :n_halves].reshape(bsz, DIFF_HEADS, 2)
        return jnp.sqrt(n) * NORM_SLACK

    def inproj_args(l):
        return (mod_rows[l], norm_g[l][None], w_in, cos, slo, shi,
                sgu_norm_g[l][None], ws[l], bs[l], half_ind)

    xs, *proj = _inproj_first(x, ctx, inproj_args(0), 0)
    for l in range(DEPTH):
        lam_init = 0.8 - 0.6 * math.exp(-0.3 * l)
        ya, q, k, v, gz, p, gc, qn, kn = proj
        lams = (lambda_q1[l][None], lambda_k1[l][None], lambda_q2[l][None], lambda_k2[l][None])
        yb = _attention_ctx(q, k, v, gz, lams, subln_g[l][None], lam_init)
        q_norm, k_norm = norm_bounds(qn), norm_bounds(kn)
        k_bound = jnp.zeros((bsz, DIFF_HEADS, MOD_ROWS, LANES), F32).at[:, :, 0:2, :].set(
            jnp.broadcast_to(k_norm[..., None], (bsz, DIFF_HEADS, 2, LANES)))
        bounded = jnp.max(q_norm * k_norm) <= MAX_SCORE_BOUND
        yb = lax.cond(
            bounded,
            lambda q, k, v, yb, kb: _attention_bounded(
                q, k, v, yb, lams, subln_g[l][None], kb, lam_init,
                tk=K_TILE, kt=K_TILE),
            lambda q, k, v, yb, kb: _attention_pipelined(
                q, k, v, yb, lams, subln_g[l][None], lam_init,
                tk=K_TILE, kt=V_TILE),
            q, k, v, yb, k_bound)
        out_args = (mod_rows[l], ya, yb, p, gc, conv_w[l], conv_b[l][None], w_out, final_g[None])
        if l + 1 < DEPTH:
            xs, *proj = _midproj(xs, out_args, inproj_args(l + 1), l + 1)
        else:
            xs = _outproj_final(xs, *out_args[:-1], l, out_args[-1])
    return xs
```

```python
import functools
import math

import jax
import jax.numpy as jnp
from jax import lax
from jax.experimental import pallas as pl
from jax.experimental.pallas import tpu as pltpu

D_MODEL = 1024
DEPTH = 4
CTX_LEN = 256
GRID_W = 64

GMLP_HEADS = 4
GMLP_HEAD_DIM = 64
GMLP_CHUNK = 128
W_A = GMLP_HEADS * GMLP_HEAD_DIM
DIFF_HEADS = 4
DIFF_HEAD_DIM = 64
HEAD_W = 2 * DIFF_HEAD_DIM
W_B = DIFF_HEADS * HEAD_W
W_C = 256
D_MIX = W_A + W_B + W_C
D_IN = 3 * W_A + 4 * W_B + 4 * W_C

COL_A = 0
COL_Q = 3 * W_A
COL_K = COL_Q + W_B
COL_V = COL_K + W_B
COL_C = COL_V + 2 * W_B

ROPE_BASE = 10000.0
ROPE_PAIRS = DIFF_HEAD_DIM // 4
EPS = 1e-6

F32 = jnp.float32
BF16 = jnp.bfloat16

VMEM_LIMIT_BYTES = 56 * 1024 * 1024
MOD_ROWS = 8
LANES = 128

IN_TILE = 768
MID_TILE = 384
FINAL_OUT_TILE = 512
Q_TILE = 256
Q_TILE_BOUNDED = 512
K_TILE = 2816
V_TILE = 256
HALO = 16


def _silu(x):
    return x * jax.nn.sigmoid(x)


def _mod_kernel(cond_ref, w_ref, b_ref, o_ref):
    a = _silu(cond_ref[...])
    o_ref[0, 0] = jnp.dot(a, w_ref[0], preferred_element_type=F32) + b_ref[0, 0]


def _modulation(cond, w_mod, b_mod):
    return pl.pallas_call(
        _mod_kernel,
        grid=(DEPTH, 3),
        in_specs=[
            pl.BlockSpec((MOD_ROWS, D_MODEL), lambda l, j: (0, 0)),
            pl.BlockSpec((1, D_MODEL, D_MODEL), lambda l, j: (l, 0, j)),
            pl.BlockSpec((1, 1, 1, D_MODEL), lambda l, j: (l, j, 0, 0)),
        ],
        out_specs=pl.BlockSpec((1, 1, MOD_ROWS, D_MODEL), lambda l, j: (l, j, 0, 0)),
        out_shape=jax.ShapeDtypeStruct((DEPTH, 3, MOD_ROWS, D_MODEL), F32),
        compiler_params=pltpu.CompilerParams(
            dimension_semantics=("arbitrary", "arbitrary"),
            vmem_limit_bytes=VMEM_LIMIT_BYTES),
        name="adaln_modulation",
    )(cond, w_mod, b_mod.reshape(DEPTH, 3, 1, D_MODEL))


def _rope(t, cos, sin_lo, sin_hi):
    return (t * cos + pltpu.roll(t, HEAD_W - ROPE_PAIRS, 1) * sin_lo
            + pltpu.roll(t, ROPE_PAIRS, 1) * sin_hi)


def _cast_weights_once(w_sc, w_ref):
    @pl.when((pl.program_id(0) == 0) & (pl.program_id(1) == 0))
    def _():
        w_sc[...] = w_ref[0].astype(BF16)


N_INPROJ_IN = 10
N_INPROJ_OUT = 9


def _inproj_kernel(*refs, tm, lat):
    n_sub = tm // CTX_LEN
    sub_refs, ctx_ref = refs[:n_sub], refs[n_sub]
    in_refs = refs[n_sub + 1:n_sub + 1 + N_INPROJ_IN]
    xs_ref = refs[n_sub + 1 + N_INPROJ_IN]
    rest = refs[n_sub + 2 + N_INPROJ_IN:]
    _cast_weights_once(rest[-1], in_refs[2])
    is_last = pl.program_id(1) == pl.num_programs(1) - 1
    tail = jnp.where(is_last, ctx_ref[0], sub_refs[-1][0])
    x = jnp.concatenate([r[0] for r in sub_refs[:-1]] + [tail], axis=0)
    xs_ref[0] = x
    _inproj_body(x, *in_refs, *rest, tm=tm, lat=lat)


def _inproj_body(x, mod_ref, g_ref, w_ref, cos_ref, slo_ref, shi_ref,
                 sgug_ref, ws_ref, bs_ref, half_ref,
                 ya_ref, q_ref, k_ref, v_ref, gz_ref, p_ref, gc_ref, qn_ref, kn_ref,
                 w_sc, *, tm, lat):
    del w_ref
    i = pl.program_id(1)
    row = i * tm + lax.broadcasted_iota(jnp.int32, (tm, 1), 0)
    is_ctx = row >= lat
    mod = mod_ref[0]
    shift = jnp.where(is_ctx, mod[3:4], mod[0:1])
    scale = jnp.where(is_ctx, mod[4:5], mod[1:2])
    h = x * lax.rsqrt(jnp.mean(x * x, axis=-1, keepdims=True) + EPS) * g_ref[...]
    h = (h * (1.0 + scale) + shift).astype(BF16)

    ra = jnp.dot(h, w_sc[:, COL_A:COL_A + 3 * W_A], preferred_element_type=F32)
    u = jax.nn.gelu(ra[:, 0:W_A])
    v = jax.nn.gelu(ra[:, W_A:2 * W_A])
    gate_a = _silu(ra[:, 2 * W_A:3 * W_A])
    vn = v * lax.rsqrt(jnp.mean(v * v, axis=-1, keepdims=True) + EPS) * sgug_ref[...]
    vn = vn.astype(BF16)
    lane = lax.broadcasted_iota(jnp.int32, (GMLP_CHUNK, W_A), 1)
    for c in range(tm // GMLP_CHUNK):
        rows = slice(c * GMLP_CHUNK, (c + 1) * GMLP_CHUNK)
        r = jnp.dot(ws_ref[...], vn[rows], preferred_element_type=F32)
        mixed = r[0:GMLP_CHUNK]
        for hd in range(1, GMLP_HEADS):
            mixed = jnp.where(lane >= hd * GMLP_HEAD_DIM,
                              r[hd * GMLP_CHUNK:(hd + 1) * GMLP_CHUNK], mixed)
        mixed = mixed + bs_ref[...]
        ya_ref[0, rows, :] = (u[rows] * mixed * gate_a[rows]).astype(BF16)

    cos, slo, shi = cos_ref[...], slo_ref[...], shi_ref[...]
    rq = jnp.dot(h, w_sc[:, COL_Q:COL_Q + W_B], preferred_element_type=F32)
    rk = jnp.dot(h, w_sc[:, COL_K:COL_K + W_B], preferred_element_type=F32)
    q_scale = math.log2(math.e) / math.sqrt(DIFF_HEAD_DIM)
    q_heads, k_heads = [], []
    for hd in range(DIFF_HEADS):
        cols = slice(hd * HEAD_W, (hd + 1) * HEAD_W)
        q_heads.append(_rope(rq[:, cols], cos, slo, shi) * q_scale)
        k_heads.append(_rope(rk[:, cols], cos, slo, shi))
        q_ref[0, hd] = q_heads[hd].astype(BF16)
        k_ref[0, hd] = k_heads[hd].astype(BF16)
    for heads, n_ref in ((q_heads, qn_ref), (k_heads, kn_ref)):
        t = jnp.concatenate(heads, axis=1)
        n = jnp.dot((t * t).astype(BF16), half_ref[...], preferred_element_type=F32)
        best = n[0:MOD_ROWS]
        for r in range(1, tm // MOD_ROWS):
            best = jnp.maximum(best, n[r * MOD_ROWS:(r + 1) * MOD_ROWS])
        n_ref[0, 0] = best
    rz = jnp.dot(h, w_sc[:, COL_V + W_B:COL_V + 2 * W_B], preferred_element_type=F32)
    gz_ref[0] = _silu(rz).astype(BF16)

    rc = jnp.dot(h, w_sc[:, COL_C:COL_C + 4 * W_C], preferred_element_type=F32)
    p_ref[0] = (rc[:, W_C:2 * W_C] * rc[:, 2 * W_C:3 * W_C]).astype(BF16)
    gc_ref[0] = (rc[:, 0:W_C] * _silu(rc[:, 3 * W_C:4 * W_C])).astype(BF16)

    rv = jnp.dot(h, w_sc[:, COL_V:COL_V + W_B], preferred_element_type=F32)
    for hd in range(DIFF_HEADS):
        v_ref[0, hd] = rv[:, hd * HEAD_W:(hd + 1) * HEAD_W].astype(BF16)


def _const_spec(shape):
    return pl.BlockSpec(shape, lambda b, i: (0,) * len(shape))


def _tok_spec(tm, width):
    return pl.BlockSpec((1, tm, width), lambda b, i: (b, i, 0))


def _inproj_specs(bsz, lt, tm, layer):
    norm_blk = pl.BlockSpec((1, 1, MOD_ROWS, LANES), lambda b, i: (b, i, 0, 0))
    norm_shape = jax.ShapeDtypeStruct((bsz, lt // tm, MOD_ROWS, LANES), F32)
    head = pl.BlockSpec((1, DIFF_HEADS, tm, HEAD_W), lambda b, i: (b, 0, i, 0))
    tab = pl.BlockSpec((tm, HEAD_W), lambda b, i: (i, 0))
    tok_shape = lambda width: jax.ShapeDtypeStruct((bsz, lt, width), BF16)
    head_shape = jax.ShapeDtypeStruct((bsz, DIFF_HEADS, lt, HEAD_W), BF16)
    in_specs = [
        pl.BlockSpec((1, MOD_ROWS, D_MODEL), lambda b, i: (b, 0, 0)),
        _const_spec((1, D_MODEL)),
        pl.BlockSpec((1, D_MODEL, D_IN), lambda b, i: (layer, 0, 0)),
        tab, tab, tab,
        _const_spec((1, W_A)),
        _const_spec((GMLP_HEADS * GMLP_CHUNK, GMLP_CHUNK)),
        _const_spec((GMLP_CHUNK, W_A)),
        _const_spec((W_B, LANES)),
    ]
    out_specs = [_tok_spec(tm, W_A), head, head, head, _tok_spec(tm, W_B), _tok_spec(tm, W_C),
                 _tok_spec(tm, W_C), norm_blk, norm_blk]
    out_shape = [tok_shape(W_A), head_shape, head_shape, head_shape,
                 tok_shape(W_B), tok_shape(W_C), tok_shape(W_C), norm_shape, norm_shape]
    assert len(in_specs) == N_INPROJ_IN and len(out_specs) == N_INPROJ_OUT
    return in_specs, out_specs, out_shape


def _inproj_first(x, ctx, in_args, layer):
    bsz, lat, _ = x.shape
    lt = lat + CTX_LEN
    tm = IN_TILE
    n_sub = tm // CTX_LEN
    last_blk = lat // CTX_LEN - 1
    sub = [pl.BlockSpec((1, CTX_LEN, D_MODEL),
                        lambda b, i, j=j: (b, jnp.minimum(i * n_sub + j, last_blk), 0))
           for j in range(n_sub)]
    ctx_spec = pl.BlockSpec((1, CTX_LEN, D_MODEL), lambda b, i: (b, 0, 0))
    in_specs, out_specs, out_shape = _inproj_specs(bsz, lt, tm, layer)
    return pl.pallas_call(
        functools.partial(_inproj_kernel, tm=tm, lat=lat),
        grid=(bsz, lt // tm),
        in_specs=sub + [ctx_spec] + in_specs,
        out_specs=[_tok_spec(tm, D_MODEL)] + out_specs,
        out_shape=[jax.ShapeDtypeStruct((bsz, lt, D_MODEL), F32)] + out_shape,
        scratch_shapes=[pltpu.VMEM((D_MODEL, D_IN), BF16)],
        compiler_params=pltpu.CompilerParams(
            dimension_semantics=("arbitrary", "arbitrary"),
            vmem_limit_bytes=VMEM_LIMIT_BYTES),
        name="in_projection",
    )(*([x] * n_sub), ctx, *in_args)


NT_DIMS = (((1,), (1,)), ((), ()))
NORM_SLACK = 1.02
MAX_SCORE_BOUND = 50.0
GZ_ARG = 7


def _widen_values(v_sc, v_ref):
    v_sc[:, 0:HEAD_W] = v_ref[0, 0]
    v_sc[:, HEAD_W:2 * HEAD_W] = jnp.ones((v_sc.shape[0], HEAD_W), BF16)


def _stack_q(qz_sc, q_ref, tq):
    q = q_ref[0, 0]
    lane = lax.broadcasted_iota(jnp.int32, (tq, HEAD_W), 1)
    zero = jnp.zeros_like(q)
    qz_sc[0:tq] = jnp.where(lane < DIFF_HEAD_DIM, q, zero)
    qz_sc[tq:2 * tq] = jnp.where(lane >= DIFF_HEAD_DIM, q, zero)
    return qz_sc[...]


def _finish_head(acc, lam_refs, sg_ref, gz_ref, o_ref, tq, lam_init):
    lq1_ref, lk1_ref, lq2_ref, lk2_ref = lam_refs
    o = acc[:, 0:HEAD_W] / acc[:, HEAD_W:2 * HEAD_W]
    lam = (jnp.exp(jnp.sum(lq1_ref[...] * lk1_ref[...]))
           - jnp.exp(jnp.sum(lq2_ref[...] * lk2_ref[...])) + lam_init)
    o = o[0:tq] - lam * o[tq:2 * tq]
    o = o * lax.rsqrt(jnp.mean(o * o, axis=-1, keepdims=True) + EPS) * sg_ref[...]
    o_ref[0] = (o * (1.0 - lam_init) * gz_ref[0].astype(F32)).astype(BF16)


def _split_index(t, n):
    if n & (n - 1) == 0:
        return lax.shift_right_logical(t, jnp.asarray(n.bit_length() - 1, t.dtype)), t & (n - 1)
    return t // n, t % n


def _pipeline_maps(bsz, n_q):
    n_tiles = bsz * DIFF_HEADS * n_q

    def score_tile(g):
        t = jnp.minimum(g, n_tiles - 1)
        bh, i = _split_index(t, n_q)
        b, h = _split_index(bh, DIFF_HEADS)
        return b, h, i

    def value_tile(g):
        return score_tile(jnp.maximum(g - 1, 0))

    def score_head(g):
        b, h, _ = score_tile(g)
        return (b, h, 0, 0)

    def score_rows(g):
        b, h, i = score_tile(g)
        return (b, h, i, 0)

    def value_head(g):
        b, h, _ = value_tile(g)
        return (b, h, 0, 0)

    def value_tok(g):
        b, h, i = value_tile(g)
        return (b, i, h)

    return n_tiles, score_head, score_rows, value_head, value_tok


def _attn_bounded_kernel(lq1_ref, lk1_ref, lq2_ref, lk2_ref, q_ref, k_ref, v_ref, gz_ref, sg_ref,
                         kb_ref, o_ref, qz_sc, e0_sc, e1_sc, l0_sc, l1_sc,
                         *, tq, tk, kt, lam_init):
    g = pl.program_id(0)
    n_keys = e0_sc.shape[1]

    @pl.when(g == 0)
    def _():
        e1_sc[...] = jnp.zeros_like(e1_sc)
        l1_sc[...] = jnp.ones_like(l1_sc)

    def step(e_w, l_w, e_r, l_r):
        qz = _stack_q(qz_sc, q_ref, tq)
        qf = qz.astype(F32)
        q_norm = jnp.sqrt(jnp.sum(qf * qf, axis=-1, keepdims=True))
        row = lax.broadcasted_iota(jnp.int32, (2 * tq, 1), 0)
        kb = kb_ref[0, 0]
        bound = q_norm * jnp.where(row < tq, kb[0:1], kb[1:2])
        l = None
        for c in range(n_keys // tk):
            s = lax.dot_general(qz, k_ref[0, 0, c * tk:(c + 1) * tk, :], NT_DIMS,
                                preferred_element_type=F32)
            for t in range(tk // LANES):
                e = jnp.exp2(s[:, t * LANES:(t + 1) * LANES] - bound)
                l = e if l is None else l + e
                e_w[:, c * tk + t * LANES:c * tk + (t + 1) * LANES] = e.astype(BF16)
        l_w[...] = jnp.broadcast_to(jnp.sum(l, axis=-1, keepdims=True), l_w.shape)

        l_prev = l_r[...]
        lam = (jnp.exp(jnp.sum(lq1_ref[...] * lk1_ref[...]))
               - jnp.exp(jnp.sum(lq2_ref[...] * lk2_ref[...])) + lam_init)
        ratio = (lam * l_prev[0:tq] / l_prev[tq:2 * tq]).astype(BF16)
        ratio = jnp.concatenate([ratio] * (kt // LANES), axis=1)
        acc = None
        for c in range(n_keys // kt):
            a = e_r[0:tq, c * kt:(c + 1) * kt] - ratio * e_r[tq:2 * tq, c * kt:(c + 1) * kt]
            vv = v_ref[0, 0, c * kt:(c + 1) * kt, :]
            d = jnp.dot(a, jnp.concatenate([vv, vv], axis=1), preferred_element_type=F32)
            acc = d if acc is None else acc + d
        o = acc[:, 0:HEAD_W] / l_prev[0:tq]
        o = o * lax.rsqrt(jnp.mean(o * o, axis=-1, keepdims=True) + EPS) * sg_ref[...]
        o_ref[0] = (o * (1.0 - lam_init) * gz_ref[0].astype(F32)).astype(BF16)

    @pl.when((g & 1) == 0)
    def _():
        step(e0_sc, l0_sc, e1_sc, l1_sc)

    @pl.when((g & 1) == 1)
    def _():
        step(e1_sc, l1_sc, e0_sc, l0_sc)


def _attention_bounded(q, k, v, gz, lams, subln_g, k_bound, lam_init, *, tk, kt):
    bsz, _, lt, _ = q.shape
    tq = Q_TILE_BOUNDED
    n_q = (lt - CTX_LEN) // tq
    n_tiles, score_head, score_rows, value_head, value_tok = _pipeline_maps(bsz, n_q)
    lam_spec = pl.BlockSpec((1, DIFF_HEAD_DIM), lambda g: (0, 0))
    seq_blk = (1, 1, lt, HEAD_W)
    return pl.pallas_call(
        functools.partial(_attn_bounded_kernel, tq=tq, tk=tk, kt=kt, lam_init=lam_init),
        grid=(n_tiles + 1,),
        in_specs=[
            lam_spec, lam_spec, lam_spec, lam_spec,
            pl.BlockSpec((1, 1, tq, HEAD_W), score_rows),
            pl.BlockSpec(seq_blk, score_head),
            pl.BlockSpec(seq_blk, value_head),
            pl.BlockSpec((1, tq, HEAD_W), value_tok),
            pl.BlockSpec((1, HEAD_W), lambda g: (0, 0)),
            pl.BlockSpec((1, 1, MOD_ROWS, LANES), score_head),
        ],
        out_specs=pl.BlockSpec((1, tq, HEAD_W), value_tok),
        out_shape=jax.ShapeDtypeStruct((bsz, lt, W_B), BF16),
        input_output_aliases={GZ_ARG: 0},
        scratch_shapes=[
            pltpu.VMEM((2 * tq, HEAD_W), BF16),
            pltpu.VMEM((2 * tq, lt), BF16),
            pltpu.VMEM((2 * tq, lt), BF16),
            pltpu.VMEM((2 * tq, LANES), F32),
            pltpu.VMEM((2 * tq, LANES), F32),
        ],
        compiler_params=pltpu.CompilerParams(
            dimension_semantics=("arbitrary",),
            vmem_limit_bytes=VMEM_LIMIT_BYTES),
        name="diff_attention_bounded",
    )(*lams, q, k, v, gz, subln_g, k_bound)


def _attn_pipe_kernel(lq1_ref, lk1_ref, lq2_ref, lk2_ref, q_ref, k_ref, v_ref, gz_ref, sg_ref,
                      o_ref, qz_sc, v_sc, s0_sc, s1_sc, m0_sc, m1_sc,
                      *, tq, tk, kt, n_q, lam_init):
    g = pl.program_id(0)
    n_keys = s0_sc.shape[1]

    @pl.when(g == 0)
    def _():
        s1_sc[...] = jnp.zeros_like(s1_sc)
        m1_sc[...] = jnp.zeros_like(m1_sc)

    @pl.when((g == 0) | ((g - 1) % n_q == 0))
    def _():
        _widen_values(v_sc, v_ref)

    def step(s_w, m_w, s_r, m_r):
        qz = _stack_q(qz_sc, q_ref, tq)
        m = None
        for c in range(n_keys // tk):
            s = lax.dot_general(qz, k_ref[0, 0, c * tk:(c + 1) * tk, :], NT_DIMS,
                                preferred_element_type=F32)
            s_w[:, c * tk:(c + 1) * tk] = s
            for t in range(tk // LANES):
                blk = s[:, t * LANES:(t + 1) * LANES]
                m = blk if m is None else jnp.maximum(m, blk)
        m_w[...] = jnp.broadcast_to(jnp.max(m, axis=-1, keepdims=True), m_w.shape)

        m_prev = jnp.concatenate([m_r[...]] * (kt // LANES), axis=1)
        acc = None
        for c in range(n_keys // kt):
            p = jnp.exp2(s_r[:, c * kt:(c + 1) * kt] - m_prev).astype(BF16)
            d = jnp.dot(p, v_sc[c * kt:(c + 1) * kt, :], preferred_element_type=F32)
            acc = d if acc is None else acc + d
        _finish_head(acc, (lq1_ref, lk1_ref, lq2_ref, lk2_ref), sg_ref, gz_ref, o_ref,
                     tq, lam_init)

    @pl.when((g & 1) == 0)
    def _():
        step(s0_sc, m0_sc, s1_sc, m1_sc)

    @pl.when((g & 1) == 1)
    def _():
        step(s1_sc, m1_sc, s0_sc, m0_sc)


def _attention_pipelined(q, k, v, gz, lams, subln_g, lam_init, *, tk, kt):
    bsz, _, lt, _ = q.shape
    tq = Q_TILE
    n_q = (lt - CTX_LEN) // tq
    n_tiles, score_head, score_rows, value_head, value_tok = _pipeline_maps(bsz, n_q)
    lam_spec = pl.BlockSpec((1, DIFF_HEAD_DIM), lambda g: (0, 0))
    seq_blk = (1, 1, lt, HEAD_W)
    return pl.pallas_call(
        functools.partial(_attn_pipe_kernel, tq=tq, tk=tk, kt=kt, n_q=n_q, lam_init=lam_init),
        grid=(n_tiles + 1,),
        in_specs=[
            lam_spec, lam_spec, lam_spec, lam_spec,
            pl.BlockSpec((1, 1, tq, HEAD_W), score_rows),
            pl.BlockSpec(seq_blk, score_head),
            pl.BlockSpec(seq_blk, value_head),
            pl.BlockSpec((1, tq, HEAD_W), value_tok),
            pl.BlockSpec((1, HEAD_W), lambda g: (0, 0)),
        ],
        out_specs=pl.BlockSpec((1, tq, HEAD_W), value_tok),
        out_shape=jax.ShapeDtypeStruct((bsz, lt, W_B), BF16),
        input_output_aliases={GZ_ARG: 0},
        scratch_shapes=[
            pltpu.VMEM((2 * tq, HEAD_W), BF16),
            pltpu.VMEM((lt, 2 * HEAD_W), BF16),
            pltpu.VMEM((2 * tq, lt), F32),
            pltpu.VMEM((2 * tq, lt), F32),
            pltpu.VMEM((2 * tq, LANES), F32),
            pltpu.VMEM((2 * tq, LANES), F32),
        ],
        compiler_params=pltpu.CompilerParams(
            dimension_semantics=("arbitrary",),
            vmem_limit_bytes=VMEM_LIMIT_BYTES),
        name="diff_attention_latent",
    )(*lams, q, k, v, gz, subln_g)


def _attn_ctx_kernel(lq1_ref, lk1_ref, lq2_ref, lk2_ref, q_ref, k_ref, v_ref, gz_ref, sg_ref,
                     o_ref, qz_sc, v_sc, *, tq, lam_init):
    _widen_values(v_sc, v_ref)
    qz = _stack_q(qz_sc, q_ref, tq)
    s = lax.dot_general(qz, k_ref[0, 0], NT_DIMS, preferred_element_type=F32)
    p = jnp.exp2(s - jnp.max(s, axis=-1, keepdims=True)).astype(BF16)
    acc = jnp.dot(p, v_sc[...], preferred_element_type=F32)
    _finish_head(acc, (lq1_ref, lk1_ref, lq2_ref, lk2_ref), sg_ref, gz_ref, o_ref, tq, lam_init)


def _attention_ctx(q, k, v, gz, lams, subln_g, lam_init):
    bsz, _, lt, _ = q.shape
    tq = CTX_LEN
    blk = (lt - CTX_LEN) // tq
    lam_spec = pl.BlockSpec((1, DIFF_HEAD_DIM), lambda b, h: (0, 0))
    head_blk = pl.BlockSpec((1, 1, tq, HEAD_W), lambda b, h: (b, h, blk, 0))
    tok_blk = pl.BlockSpec((1, tq, HEAD_W), lambda b, h: (b, blk, h))
    return pl.pallas_call(
        functools.partial(_attn_ctx_kernel, tq=tq, lam_init=lam_init),
        grid=(bsz, DIFF_HEADS),
        in_specs=[lam_spec, lam_spec, lam_spec, lam_spec, head_blk, head_blk, head_blk, tok_blk,
                  pl.BlockSpec((1, HEAD_W), lambda b, h: (0, 0))],
        out_specs=tok_blk,
        out_shape=jax.ShapeDtypeStruct((bsz, lt, W_B), BF16),
        input_output_aliases={GZ_ARG: 0},
        scratch_shapes=[
            pltpu.VMEM((2 * tq, HEAD_W), BF16),
            pltpu.VMEM((tq, 2 * HEAD_W), BF16),
        ],
        compiler_params=pltpu.CompilerParams(
            dimension_semantics=("arbitrary", "arbitrary"),
            vmem_limit_bytes=VMEM_LIMIT_BYTES),
        name="diff_attention_ctx",
    )(*lams, q, k, v, gz, subln_g)


N_OUTPROJ_IN = 12


def _outproj_kernel(*refs, tm, lat, lt, final):
    o_ref, w_sc = refs[N_OUTPROJ_IN:]
    _cast_weights_once(w_sc, refs[10])
    o_ref[0] = _outproj_body(*refs[:N_OUTPROJ_IN], w_sc, tm=tm, lat=lat, lt=lt, final=final)


def _outproj_body(x_ref, mod_ref, ya_ref, yb_ref, p_ref, pprev_ref, pnext_ref, gc_ref,
                  cw_ref, cb_ref, w_ref, fg_ref, w_sc, *, tm, lat, lt, final):
    del w_ref
    i = pl.program_id(1)
    row = i * tm + lax.broadcasted_iota(jnp.int32, (tm, 1), 0)
    local = lax.broadcasted_iota(jnp.int32, (tm, 1), 0)
    p = p_ref[0].astype(F32)
    prev_row = pprev_ref[0, HALO - 1:HALO, :].astype(F32)
    next_row = pnext_ref[0, 0:1, :].astype(F32)
    up = jnp.where(local == 0, prev_row, pltpu.roll(p, 1, 0))
    dn = jnp.where(local == tm - 1, next_row, pltpu.roll(p, tm - 1, 0))
    up = jnp.where((row == 0) | (row == lat), 0.0, up)
    dn = jnp.where((row == lat - 1) | (row == lt - 1), 0.0, dn)
    cw = cw_ref[...]
    conv = up * cw[0:1] + p * cw[1:2] + dn * cw[2:3] + cb_ref[...]
    yc = (gc_ref[0].astype(F32) * conv).astype(BF16)

    y = jnp.dot(ya_ref[0], w_sc[0:W_A], preferred_element_type=F32)
    y = y + jnp.dot(yb_ref[0], w_sc[W_A:W_A + W_B], preferred_element_type=F32)
    y = y + jnp.dot(yc, w_sc[W_A + W_B:D_MIX], preferred_element_type=F32)

    mod = mod_ref[0]
    gate = jnp.where(row >= lat, mod[5:6], mod[2:3])
    x = x_ref[0] + gate * y
    if final:
        x = x * lax.rsqrt(jnp.mean(x * x, axis=-1, keepdims=True) + EPS) * fg_ref[...]
    return x


def _midproj_kernel(*refs, tm, lat, lt):
    out_in = refs[:N_OUTPROJ_IN]
    in_in = refs[N_OUTPROJ_IN:N_OUTPROJ_IN + N_INPROJ_IN]
    x_ref = refs[N_OUTPROJ_IN + N_INPROJ_IN]
    in_out = refs[N_OUTPROJ_IN + N_INPROJ_IN + 1:N_OUTPROJ_IN + N_INPROJ_IN + 1 + N_INPROJ_OUT]
    wo_sc, wi_sc = refs[-2:]
    _cast_weights_once(wo_sc, out_in[10])
    _cast_weights_once(wi_sc, in_in[2])
    x = _outproj_body(*out_in, wo_sc, tm=tm, lat=lat, lt=lt, final=False)
    x_ref[0] = x
    _inproj_body(x, *in_in, *in_out, wi_sc, tm=tm, lat=lat)


def _outproj_specs(lt, tm, layer):
    hpt = tm // HALO
    n_halo = lt // HALO
    prev = pl.BlockSpec((1, HALO, W_C),
                        lambda b, i: (b, jnp.maximum(i * hpt - 1, 0), 0))
    nxt = pl.BlockSpec((1, HALO, W_C),
                       lambda b, i: (b, jnp.minimum((i + 1) * hpt, n_halo - 1), 0))
    in_specs = [
        _tok_spec(tm, D_MODEL),
        pl.BlockSpec((1, MOD_ROWS, D_MODEL), lambda b, i: (b, 0, 0)),
        _tok_spec(tm, W_A), _tok_spec(tm, W_B), _tok_spec(tm, W_C), prev, nxt, _tok_spec(tm, W_C),
        _const_spec((3, W_C)), _const_spec((1, W_C)),
        pl.BlockSpec((1, D_MIX, D_MODEL), lambda b, i: (layer, 0, 0)), _const_spec((1, D_MODEL)),
    ]
    assert len(in_specs) == N_OUTPROJ_IN
    return in_specs


def _outproj_final(xs, mod_l, ya, yb, p, gc, conv_w, conv_b, w_out, layer, final_g):
    bsz, lt, _ = xs.shape
    lat = lt - CTX_LEN
    tm = FINAL_OUT_TILE
    return pl.pallas_call(
        functools.partial(_outproj_kernel, tm=tm, lat=lat, lt=lt, final=True),
        grid=(bsz, lat // tm),
        in_specs=_outproj_specs(lt, tm, layer),
        out_specs=_tok_spec(tm, D_MODEL),
        out_shape=jax.ShapeDtypeStruct((bsz, lat, D_MODEL), F32),
        scratch_shapes=[pltpu.VMEM((D_MIX, D_MODEL), BF16)],
        compiler_params=pltpu.CompilerParams(
            dimension_semantics=("arbitrary", "arbitrary"),
            vmem_limit_bytes=VMEM_LIMIT_BYTES),
        name="out_projection",
    )(xs, mod_l, ya, yb, p, p, p, gc, conv_w, conv_b, w_out, final_g)


def _midproj(xs, out_args, in_args, layer):
    bsz, lt, _ = xs.shape
    tm = MID_TILE
    mod_prev, ya, yb, p, gc, conv_w, conv_b, w_out, final_g = out_args
    in_specs, out_specs, out_shape = _inproj_specs(bsz, lt, tm, layer)
    return pl.pallas_call(
        functools.partial(_midproj_kernel, tm=tm, lat=lt - CTX_LEN, lt=lt),
        grid=(bsz, lt // tm),
        in_specs=_outproj_specs(lt, tm, layer - 1) + in_specs,
        out_specs=[_tok_spec(tm, D_MODEL)] + out_specs,
        out_shape=[jax.ShapeDtypeStruct((bsz, lt, D_MODEL), F32)] + out_shape,
        scratch_shapes=[pltpu.VMEM((D_MIX, D_MODEL), BF16), pltpu.VMEM((D_MODEL, D_IN), BF16)],
        compiler_params=pltpu.CompilerParams(
            dimension_semantics=("arbitrary", "arbitrary"),
            vmem_limit_bytes=VMEM_LIMIT_BYTES),
        name="mid_projection",
    )(xs, mod_prev, ya, yb, p, p, p, gc, conv_w, conv_b, w_out, final_g, *in_args)


def _rope_tables(length):
    t = jnp.arange(length)
    pos = jnp.stack([t // GRID_W, t % GRID_W], axis=1).astype(F32)
    inv = ROPE_BASE ** (-jnp.arange(ROPE_PAIRS, dtype=F32) / ROPE_PAIRS)
    ang = pos[:, :, None] * inv

    def to_lanes(tab):
        tab = jnp.broadcast_to(tab[:, None, :, None, :], (length, 2, 2, 2, ROPE_PAIRS))
        return tab.reshape(length, HEAD_W)

    cos = to_lanes(jnp.cos(ang))
    sin = to_lanes(jnp.sin(ang))
    second = (jnp.arange(HEAD_W)[None, :] % (2 * ROPE_PAIRS)) >= ROPE_PAIRS
    sin_lo = jnp.where(second, 0.0, -sin)
    sin_hi = jnp.where(second, sin, 0.0)
    ident = jnp.ones((CTX_LEN, HEAD_W), F32)
    zeros = jnp.zeros((CTX_LEN, HEAD_W), F32)
    return (jnp.concatenate([cos, ident]), jnp.concatenate([sin_lo, zeros]),
            jnp.concatenate([sin_hi, zeros]))


def kernel(x, c, ctx, c_ctx, w_mod, b_mod, norm_g, w_in, w_out, sgu_norm_g, sgu_w, sgu_b,
           lambda_q1, lambda_k1, lambda_q2, lambda_k2, subln_g, conv_w, conv_b, final_g):
    bsz, length, _ = x.shape
    assert ctx.shape[1] == CTX_LEN and bsz + 1 <= MOD_ROWS
    lt = CTX_LEN + length
    assert lt % IN_TILE == 0 and lt % MID_TILE == 0 and IN_TILE % CTX_LEN == 0
    assert length % FINAL_OUT_TILE == 0 and length % Q_TILE == 0 and length % Q_TILE_BOUNDED == 0
    assert length % CTX_LEN == 0 and lt % K_TILE == 0

    cond = jnp.zeros((MOD_ROWS, D_MODEL), F32).at[:bsz].set(c).at[bsz].set(c_ctx)
    mod = _modulation(cond, w_mod, b_mod)
    mod_b = jnp.transpose(mod[:, :, :bsz], (0, 2, 1, 3))
    mod_c = jnp.broadcast_to(mod[:, None, :, bsz], (DEPTH, bsz, 3, D_MODEL))
    pad = jnp.zeros((DEPTH, bsz, MOD_ROWS - 6, D_MODEL), F32)
    mod_rows = jnp.concatenate([mod_b, mod_c, pad], axis=2)

    cos, slo, shi = _rope_tables(length)
    ws = sgu_w.reshape(DEPTH, GMLP_HEADS * GMLP_CHUNK, GMLP_CHUNK).astype(BF16)
    bs = jnp.repeat(jnp.transpose(sgu_b, (0, 2, 1)), GMLP_HEAD_DIM, axis=2)

    half_ind = (jnp.arange(W_B)[:, None] // DIFF_HEAD_DIM == jnp.arange(LANES)[None, :]).astype(BF16)
    n_halves = 2 * DIFF_HEADS

    def norm_bounds(n):
        n = jnp.max(n, axis=(1, 2))[:, :n_halves].reshape(bsz, DIFF_HEADS, 2)
        return jnp.sqrt(n) * NORM_SLACK

    def inproj_args(l):
        return (mod_rows[l], norm_g[l][None], w_in, cos, slo, shi,
                sgu_norm_g[l][None], ws[l], bs[l], half_ind)

    xs, *proj = _inproj_first(x, ctx, inproj_args(0), 0)
    for l in range(DEPTH):
        lam_init = 0.8 - 0.6 * math.exp(-0.3 * l)
        ya, q, k, v, gz, p, gc, qn, kn = proj
        lams = (lambda_q1[l][None], lambda_k1[l][None], lambda_q2[l][None], lambda_k2[l][None])
        yb = _attention_ctx(q, k, v, gz, lams, subln_g[l][None], lam_init)
        q_norm, k_norm = norm_bounds(qn), norm_bounds(kn)
        k_bound = jnp.zeros((bsz, DIFF_HEADS, MOD_ROWS, LANES), F32).at[:, :, 0:2, :].set(
            jnp.broadcast_to(k_norm[..., None], (bsz, DIFF_HEADS, 2, LANES)))
        bounded = jnp.max(q_norm * k_norm) <= MAX_SCORE_BOUND
        yb = lax.cond(
            bounded,
            lambda q, k, v, yb, kb: _attention_bounded(
                q, k, v, yb, lams, subln_g[l][None], kb, lam_init,
                tk=K_TILE, kt=K_TILE),
            lambda q, k, v, yb, kb: _attention_pipelined(
                q, k, v, yb, lams, subln_g[l][None], lam_init,
                tk=K_TILE, kt=V_TILE),
            q, k, v, yb, k_bound)
        out_args = (mod_rows[l], ya, yb, p, gc, conv_w[l], conv_b[l][None], w_out, final_g[None])
        if l + 1 < DEPTH:
            xs, *proj = _midproj(xs, out_args, inproj_args(l + 1), l + 1)
        else:
            xs = _outproj_final(xs, *out_args[:-1], l, out_args[-1])
    return xs
```

```python
import functools
import math

import jax
import jax.numpy as jnp
from jax import lax
from jax.experimental import pallas as pl
from jax.experimental.pallas import tpu as pltpu

D_MODEL = 1024
DEPTH = 4
CTX_LEN = 256
GRID_W = 64

GMLP_HEADS = 4
GMLP_HEAD_DIM = 64
GMLP_CHUNK = 128
W_A = GMLP_HEADS * GMLP_HEAD_DIM
DIFF_HEADS = 4
DIFF_HEAD_DIM = 64
HEAD_W = 2 * DIFF_HEAD_DIM
W_B = DIFF_HEADS * HEAD_W
W_C = 256
D_MIX = W_A + W_B + W_C
D_IN = 3 * W_A + 4 * W_B + 4 * W_C

COL_A = 0
COL_Q = 3 * W_A
COL_K = COL_Q + W_B
COL_V = COL_K + W_B
COL_C = COL_V + 2 * W_B

ROPE_BASE = 10000.0
ROPE_PAIRS = DIFF_HEAD_DIM // 4
EPS = 1e-6

F32 = jnp.float32
BF16 = jnp.bfloat16

VMEM_LIMIT_BYTES = 56 * 1024 * 1024
MOD_ROWS = 8
LANES = 128

IN_TILE = 768
MID_TILE = 384
FINAL_OUT_TILE = 512
Q_TILE = 256
Q_TILE_BOUNDED = 512
K_TILE = 2816
V_TILE = 256
HALO = 16


def _silu(x):
    return x * jax.nn.sigmoid(x)


def _mod_kernel(cond_ref, w_ref, b_ref, o_ref):
    a = _silu(cond_ref[...])
    o_ref[0, 0] = jnp.dot(a, w_ref[0], preferred_element_type=F32) + b_ref[0, 0]


def _modulation(cond, w_mod, b_mod):
    return pl.pallas_call(
        _mod_kernel,
        grid=(DEPTH, 3),
        in_specs=[
            pl.BlockSpec((MOD_ROWS, D_MODEL), lambda l, j: (0, 0)),
            pl.BlockSpec((1, D_MODEL, D_MODEL), lambda l, j: (l, 0, j)),
            pl.BlockSpec((1, 1, 1, D_MODEL), lambda l, j: (l, j, 0, 0)),
        ],
        out_specs=pl.BlockSpec((1, 1, MOD_ROWS, D_MODEL), lambda l, j: (l, j, 0, 0)),
        out_shape=jax.ShapeDtypeStruct((DEPTH, 3, MOD_ROWS, D_MODEL), F32),
        compiler_params=pltpu.CompilerParams(
            dimension_semantics=("arbitrary", "arbitrary"),
            vmem_limit_bytes=VMEM_LIMIT_BYTES),
        name="adaln_modulation",
    )(cond, w_mod, b_mod.reshape(DEPTH, 3, 1, D_MODEL))


def _rope(t, cos, sin_lo, sin_hi):
    return (t * cos + pltpu.roll(t, HEAD_W - ROPE_PAIRS, 1) * sin_lo
            + pltpu.roll(t, ROPE_PAIRS, 1) * sin_hi)


def _cast_weights_once(w_sc, w_ref):
    @pl.when((pl.program_id(0) == 0) & (pl.program_id(1) == 0))
    def _():
        w_sc[...] = w_ref[0].astype(BF16)


N_INPROJ_IN = 10
N_INPROJ_OUT = 9


def _inproj_kernel(*refs, tm, lat):
    n_sub = tm // CTX_LEN
    sub_refs, ctx_ref = refs[:n_sub], refs[n_sub]
    in_refs = refs[n_sub + 1:n_sub + 1 + N_INPROJ_IN]
    xs_ref = refs[n_sub + 1 + N_INPROJ_IN]
    rest = refs[n_sub + 2 + N_INPROJ_IN:]
    _cast_weights_once(rest[-1], in_refs[2])
    is_last = pl.program_id(1) == pl.num_programs(1) - 1
    tail = jnp.where(is_last, ctx_ref[0], sub_refs[-1][0])
    x = jnp.concatenate([r[0] for r in sub_refs[:-1]] + [tail], axis=0)
    xs_ref[0] = x
    _inproj_body(x, *in_refs, *rest, tm=tm, lat=lat)


def _inproj_body(x, mod_ref, g_ref, w_ref, cos_ref, slo_ref, shi_ref,
                 sgug_ref, ws_ref, bs_ref, half_ref,
                 ya_ref, q_ref, k_ref, v_ref, gz_ref, p_ref, gc_ref, qn_ref, kn_ref,
                 w_sc, *, tm, lat):
    del w_ref
    i = pl.program_id(1)
    row = i * tm + lax.broadcasted_iota(jnp.int32, (tm, 1), 0)
    is_ctx = row >= lat
    mod = mod_ref[0]
    shift = jnp.where(is_ctx, mod[3:4], mod[0:1])
    scale = jnp.where(is_ctx, mod[4:5], mod[1:2])
    h = x * lax.rsqrt(jnp.mean(x * x, axis=-1, keepdims=True) + EPS) * g_ref[...]
    h = (h * (1.0 + scale) + shift).astype(BF16)

    ra = jnp.dot(h, w_sc[:, COL_A:COL_A + 3 * W_A], preferred_element_type=F32)
    u = jax.nn.gelu(ra[:, 0:W_A])
    v = jax.nn.gelu(ra[:, W_A:2 * W_A])
    gate_a = _silu(ra[:, 2 * W_A:3 * W_A])
    vn = v * lax.rsqrt(jnp.mean(v * v, axis=-1, keepdims=True) + EPS) * sgug_ref[...]
    vn = vn.astype(BF16)
    lane = lax.broadcasted_iota(jnp.int32, (GMLP_CHUNK, W_A), 1)
    for c in range(tm // GMLP_CHUNK):
        rows = slice(c * GMLP_CHUNK, (c + 1) * GMLP_CHUNK)
        r = jnp.dot(ws_ref[...], vn[rows], preferred_element_type=F32)
        mixed = r[0:GMLP_CHUNK]
        for hd in range(1, GMLP_HEADS):
            mixed = jnp.where(lane >= hd * GMLP_HEAD_DIM,
                              r[hd * GMLP_CHUNK:(hd + 1) * GMLP_CHUNK], mixed)
        mixed = mixed + bs_ref[...]
        ya_ref[0, rows, :] = (u[rows] * mixed * gate_a[rows]).astype(BF16)

    cos, slo, shi = cos_ref[...], slo_ref[...], shi_ref[...]
    rq = jnp.dot(h, w_sc[:, COL_Q:COL_Q + W_B], preferred_element_type=F32)
    rk = jnp.dot(h, w_sc[:, COL_K:COL_K + W_B], preferred_element_type=F32)
    q_scale = math.log2(math.e) / math.sqrt(DIFF_HEAD_DIM)
    q_heads, k_heads = [], []
    for hd in range(DIFF_HEADS):
        cols = slice(hd * HEAD_W, (hd + 1) * HEAD_W)
        q_heads.append(_rope(rq[:, cols], cos, slo, shi) * q_scale)
        k_heads.append(_rope(rk[:, cols], cos, slo, shi))
        q_ref[0, hd] = q_heads[hd].astype(BF16)
        k_ref[0, hd] = k_heads[hd].astype(BF16)
    for heads, n_ref in ((q_heads, qn_ref), (k_heads, kn_ref)):
        t = jnp.concatenate(heads, axis=1)
        n = jnp.dot((t * t).astype(BF16), half_ref[...], preferred_element_type=F32)
        best = n[0:MOD_ROWS]
        for r in range(1, tm // MOD_ROWS):
            best = jnp.maximum(best, n[r * MOD_ROWS:(r + 1) * MOD_ROWS])
        n_ref[0, 0] = best
    rz = jnp.dot(h, w_sc[:, COL_V + W_B:COL_V + 2 * W_B], preferred_element_type=F32)
    gz_ref[0] = _silu(rz).astype(BF16)

    rc = jnp.dot(h, w_sc[:, COL_C:COL_C + 4 * W_C], preferred_element_type=F32)
    p_ref[0] = (rc[:, W_C:2 * W_C] * rc[:, 2 * W_C:3 * W_C]).astype(BF16)
    gc_ref[0] = (rc[:, 0:W_C] * _silu(rc[:, 3 * W_C:4 * W_C])).astype(BF16)

    rv = jnp.dot(h, w_sc[:, COL_V:COL_V + W_B], preferred_element_type=F32)
    for hd in range(DIFF_HEADS):
        v_ref[0, hd] = rv[:, hd * HEAD_W:(hd + 1) * HEAD_W].astype(BF16)


def _const_spec(shape):
    return pl.BlockSpec(shape, lambda b, i: (0,) * len(shape))


def _tok_spec(tm, width):
    return pl.BlockSpec((1, tm, width), lambda b, i: (b, i, 0))


def _inproj_specs(bsz, lt, tm, layer):
    norm_blk = pl.BlockSpec((1, 1, MOD_ROWS, LANES), lambda b, i: (b, i, 0, 0))
    norm_shape = jax.ShapeDtypeStruct((bsz, lt // tm, MOD_ROWS, LANES), F32)
    head = pl.BlockSpec((1, DIFF_HEADS, tm, HEAD_W), lambda b, i: (b, 0, i, 0))
    tab = pl.BlockSpec((tm, HEAD_W), lambda b, i: (i, 0))
    tok_shape = lambda width: jax.ShapeDtypeStruct((bsz, lt, width), BF16)
    head_shape = jax.ShapeDtypeStruct((bsz, DIFF_HEADS, lt, HEAD_W), BF16)
    in_specs = [
        pl.BlockSpec((1, MOD_ROWS, D_MODEL), lambda b, i: (b, 0, 0)),
        _const_spec((1, D_MODEL)),
        pl.BlockSpec((1, D_MODEL, D_IN), lambda b, i: (layer, 0, 0)),
        tab, tab, tab,
        _const_spec((1, W_A)),
        _const_spec((GMLP_HEADS * GMLP_CHUNK, GMLP_CHUNK)),
        _const_spec((GMLP_CHUNK, W_A)),
        _const_spec((W_B, LANES)),
    ]
    out_specs = [_tok_spec(tm, W_A), head, head, head, _tok_spec(tm, W_B), _tok_spec(tm, W_C),
                 _tok_spec(tm, W_C), norm_blk, norm_blk]
    out_shape = [tok_shape(W_A), head_shape, head_shape, head_shape,
                 tok_shape(W_B), tok_shape(W_C), tok_shape(W_C), norm_shape, norm_shape]
    assert len(in_specs) == N_INPROJ_IN and len(out_specs) == N_INPROJ_OUT
    return in_specs, out_specs, out_shape


def _inproj_first(x, ctx, in_args, layer):
    bsz, lat, _ = x.shape
    lt = lat + CTX_LEN
    tm = IN_TILE
    n_sub = tm // CTX_LEN
    last_blk = lat // CTX_LEN - 1
    sub = [pl.BlockSpec((1, CTX_LEN, D_MODEL),
                        lambda b, i, j=j: (b, jnp.minimum(i * n_sub + j, last_blk), 0))
           for j in range(n_sub)]
    ctx_spec = pl.BlockSpec((1, CTX_LEN, D_MODEL), lambda b, i: (b, 0, 0))
    in_specs, out_specs, out_shape = _inproj_specs(bsz, lt, tm, layer)
    return pl.pallas_call(
        functools.partial(_inproj_kernel, tm=tm, lat=lat),
        grid=(bsz, lt // tm),
        in_specs=sub + [ctx_spec] + in_specs,
        out_specs=[_tok_spec(tm, D_MODEL)] + out_specs,
        out_shape=[jax.ShapeDtypeStruct((bsz, lt, D_MODEL), F32)] + out_shape,
        scratch_shapes=[pltpu.VMEM((D_MODEL, D_IN), BF16)],
        compiler_params=pltpu.CompilerParams(
            dimension_semantics=("arbitrary", "arbitrary"),
            vmem_limit_bytes=VMEM_LIMIT_BYTES),
        name="in_projection",
    )(*([x] * n_sub), ctx, *in_args)


NT_DIMS = (((1,), (1,)), ((), ()))
NORM_SLACK = 1.02
MAX_SCORE_BOUND = 50.0
GZ_ARG = 7


def _widen_values(v_sc, v_ref):
    v_sc[:, 0:HEAD_W] = v_ref[0, 0]
    v_sc[:, HEAD_W:2 * HEAD_W] = jnp.ones((v_sc.shape[0], HEAD_W), BF16)


def _stack_q(qz_sc, q_ref, tq):
    q = q_ref[0, 0]
    lane = lax.broadcasted_iota(jnp.int32, (tq, HEAD_W), 1)
    zero = jnp.zeros_like(q)
    qz_sc[0:tq] = jnp.where(lane < DIFF_HEAD_DIM, q, zero)
    qz_sc[tq:2 * tq] = jnp.where(lane >= DIFF_HEAD_DIM, q, zero)
    return qz_sc[...]


def _finish_head(acc, lam_refs, sg_ref, gz_ref, o_ref, tq, lam_init):
    lq1_ref, lk1_ref, lq2_ref, lk2_ref = lam_refs
    o = acc[:, 0:HEAD_W] / acc[:, HEAD_W:2 * HEAD_W]
    lam = (jnp.exp(jnp.sum(lq1_ref[...] * lk1_ref[...]))
           - jnp.exp(jnp.sum(lq2_ref[...] * lk2_ref[...])) + lam_init)
    o = o[0:tq] - lam * o[tq:2 * tq]
    o = o * lax.rsqrt(jnp.mean(o * o, axis=-1, keepdims=True) + EPS) * sg_ref[...]
    o_ref[0] = (o * (1.0 - lam_init) * gz_ref[0].astype(F32)).astype(BF16)


def _split_index(t, n):
    if n & (n - 1) == 0:
        return lax.shift_right_logical(t, jnp.asarray(n.bit_length() - 1, t.dtype)), t & (n - 1)
    return t // n, t % n


def _pipeline_maps(bsz, n_q):
    n_tiles = bsz * DIFF_HEADS * n_q

    def score_tile(g):
        t = jnp.minimum(g, n_tiles - 1)
        bh, i = _split_index(t, n_q)
        b, h = _split_index(bh, DIFF_HEADS)
        return b, h, i

    def value_tile(g):
        return score_tile(jnp.maximum(g - 1, 0))

    def score_head(g):
        b, h, _ = score_tile(g)
        return (b, h, 0, 0)

    def score_rows(g):
        b, h, i = score_tile(g)
        return (b, h, i, 0)

    def value_head(g):
        b, h, _ = value_tile(g)
        return (b, h, 0, 0)

    def value_tok(g):
        b, h, i = value_tile(g)
        return (b, i, h)

    return n_tiles, score_head, score_rows, value_head, value_tok


def _attn_bounded_kernel(lq1_ref, lk1_ref, lq2_ref, lk2_ref, q_ref, k_ref, v_ref, gz_ref, sg_ref,
                         kb_ref, o_ref, qz_sc, e0_sc, e1_sc, l0_sc, l1_sc,
                         *, tq, tk, kt, lam_init):
    g = pl.program_id(0)
    n_keys = e0_sc.shape[1]

    @pl.when(g == 0)
    def _():
        e1_sc[...] = jnp.zeros_like(e1_sc)
        l1_sc[...] = jnp.ones_like(l1_sc)

    def step(e_w, l_w, e_r, l_r):
        qz = _stack_q(qz_sc, q_ref, tq)
        qf = qz.astype(F32)
        q_norm = jnp.sqrt(jnp.sum(qf * qf, axis=-1, keepdims=True))
        row = lax.broadcasted_iota(jnp.int32, (2 * tq, 1), 0)
        kb = kb_ref[0, 0]
        bound = q_norm * jnp.where(row < tq, kb[0:1], kb[1:2])
        l = None
        for c in range(n_keys // tk):
            s = lax.dot_general(qz, k_ref[0, 0, c * tk:(c + 1) * tk, :], NT_DIMS,
                                preferred_element_type=F32)
            for t in range(tk // LANES):
                e = jnp.exp2(s[:, t * LANES:(t + 1) * LANES] - bound)
                l = e if l is None else l + e
                e_w[:, c * tk + t * LANES:c * tk + (t + 1) * LANES] = e.astype(BF16)
        l_w[...] = jnp.broadcast_to(jnp.sum(l, axis=-1, keepdims=True), l_w.shape)

        l_prev = l_r[...]
        lam = (jnp.exp(jnp.sum(lq1_ref[...] * lk1_ref[...]))
               - jnp.exp(jnp.sum(lq2_ref[...] * lk2_ref[...])) + lam_init)
        ratio = (lam * l_prev[0:tq] / l_prev[tq:2 * tq]).astype(BF16)
        ratio = jnp.concatenate([ratio] * (kt // LANES), axis=1)
        acc = None
        for c in range(n_keys // kt):
            a = e_r[0:tq, c * kt:(c + 1) * kt] - ratio * e_r[tq:2 * tq, c * kt:(c + 1) * kt]
            vv = v_ref[0, 0, c * kt:(c + 1) * kt, :]
            d = jnp.dot(a, jnp.concatenate([vv, vv], axis=1), preferred_element_type=F32)
            acc = d if acc is None else acc + d
        o = acc[:, 0:HEAD_W] / l_prev[0:tq]
        o = o * lax.rsqrt(jnp.mean(o * o, axis=-1, keepdims=True) + EPS) * sg_ref[...]
        o_ref[0] = (o * (1.0 - lam_init) * gz_ref[0].astype(F32)).astype(BF16)

    @pl.when((g & 1) == 0)
    def _():
        step(e0_sc, l0_sc, e1_sc, l1_sc)

    @pl.when((g & 1) == 1)
    def _():
        step(e1_sc, l1_sc, e0_sc, l0_sc)


def _attention_bounded(q, k, v, gz, lams, subln_g, k_bound, lam_init, *, tk, kt):
    bsz, _, lt, _ = q.shape
    tq = Q_TILE_BOUNDED
    n_q = (lt - CTX_LEN) // tq
    n_tiles, score_head, score_rows, value_head, value_tok = _pipeline_maps(bsz, n_q)
    lam_spec = pl.BlockSpec((1, DIFF_HEAD_DIM), lambda g: (0, 0))
    seq_blk = (1, 1, lt, HEAD_W)
    return pl.pallas_call(
        functools.partial(_attn_bounded_kernel, tq=tq, tk=tk, kt=kt, lam_init=lam_init),
        grid=(n_tiles + 1,),
        in_specs=[
            lam_spec, lam_spec, lam_spec, lam_spec,
            pl.BlockSpec((1, 1, tq, HEAD_W), score_rows),
            pl.BlockSpec(seq_blk, score_head),
            pl.BlockSpec(seq_blk, value_head),
            pl.BlockSpec((1, tq, HEAD_W), value_tok),
            pl.BlockSpec((1, HEAD_W), lambda g: (0, 0)),
            pl.BlockSpec((1, 1, MOD_ROWS, LANES), score_head),
        ],
        out_specs=pl.BlockSpec((1, tq, HEAD_W), value_tok),
        out_shape=jax.ShapeDtypeStruct((bsz, lt, W_B), BF16),
        input_output_aliases={GZ_ARG: 0},
        scratch_shapes=[
            pltpu.VMEM((2 * tq, HEAD_W), BF16),
            pltpu.VMEM((2 * tq, lt), BF16),
            pltpu.VMEM((2 * tq, lt), BF16),
            pltpu.VMEM((2 * tq, LANES), F32),
            pltpu.VMEM((2 * tq, LANES), F32),
        ],
        compiler_params=pltpu.CompilerParams(
            dimension_semantics=("arbitrary",),
            vmem_limit_bytes=VMEM_LIMIT_BYTES),
        name="diff_attention_bounded",
    )(*lams, q, k, v, gz, subln_g, k_bound)


def _attn_pipe_kernel(lq1_ref, lk1_ref, lq2_ref, lk2_ref, q_ref, k_ref, v_ref, gz_ref, sg_ref,
                      o_ref, qz_sc, v_sc, s0_sc, s1_sc, m0_sc, m1_sc,
                      *, tq, tk, kt, n_q, lam_init):
    g = pl.program_id(0)
    n_keys = s0_sc.shape[1]

    @pl.when(g == 0)
    def _():
        s1_sc[...] = jnp.zeros_like(s1_sc)
        m1_sc[...] = jnp.zeros_like(m1_sc)

    @pl.when((g == 0) | ((g - 1) % n_q == 0))
    def _():
        _widen_values(v_sc, v_ref)

    def step(s_w, m_w, s_r, m_r):
        qz = _stack_q(qz_sc, q_ref, tq)
        m = None
        for c in range(n_keys // tk):
            s = lax.dot_general(qz, k_ref[0, 0, c * tk:(c + 1) * tk, :], NT_DIMS,
                                preferred_element_type=F32)
            s_w[:, c * tk:(c + 1) * tk] = s
            for t in range(tk // LANES):
                blk = s[:, t * LANES:(t + 1) * LANES]
                m = blk if m is None else jnp.maximum(m, blk)
        m_w[...] = jnp.broadcast_to(jnp.max(m, axis=-1, keepdims=True), m_w.shape)

        m_prev = jnp.concatenate([m_r[...]] * (kt // LANES), axis=1)
        acc = None
        for c in range(n_keys // kt):
            p = jnp.exp2(s_r[:, c * kt:(c + 1) * kt] - m_prev).astype(BF16)
            d = jnp.dot(p, v_sc[c * kt:(c + 1) * kt, :], preferred_element_type=F32)
            acc = d if acc is None else acc + d
        _finish_head(acc, (lq1_ref, lk1_ref, lq2_ref, lk2_ref), sg_ref, gz_ref, o_ref,
                     tq, lam_init)

    @pl.when((g & 1) == 0)
    def _():
        step(s0_sc, m0_sc, s1_sc, m1_sc)

    @pl.when((g & 1) == 1)
    def _():
        step(s1_sc, m1_sc, s0_sc, m0_sc)


def _attention_pipelined(q, k, v, gz, lams, subln_g, lam_init, *, tk, kt):
    bsz, _, lt, _ = q.shape
    tq = Q_TILE
    n_q = (lt - CTX_LEN) // tq
    n_tiles, score_head, score_rows, value_head, value_tok = _pipeline_maps(bsz, n_q)
    lam_spec = pl.BlockSpec((1, DIFF_HEAD_DIM), lambda g: (0, 0))
    seq_blk = (1, 1, lt, HEAD_W)
    return pl.pallas_call(
        functools.partial(_attn_pipe_kernel, tq=tq, tk=tk, kt=kt, n_q=n_q, lam_init=lam_init),
        grid=(n_tiles + 1,),
        in_specs=[
            lam_spec, lam_spec, lam_spec, lam_spec,
            pl.BlockSpec((1, 1, tq, HEAD_W), score_rows),
            pl.BlockSpec(seq_blk, score_head),
            pl.BlockSpec(seq_blk, value_head),
            pl.BlockSpec((1, tq, HEAD_W), value_tok),
            pl.BlockSpec((1, HEAD_W), lambda g: (0, 0)),
        ],
        out_specs=pl.BlockSpec((1, tq, HEAD_W), value_tok),
        out_shape=jax.ShapeDtypeStruct((bsz, lt, W_B), BF16),
        input_output_aliases={GZ_ARG: 0},
        scratch_shapes=[
            pltpu.VMEM((2 * tq, HEAD_W), BF16),
            pltpu.VMEM((lt, 2 * HEAD_W), BF16),
            pltpu.VMEM((2 * tq, lt), F32),
            pltpu.VMEM((2 * tq, lt), F32),
            pltpu.VMEM((2 * tq, LANES), F32),
            pltpu.VMEM((2 * tq, LANES), F32),
        ],
        compiler_params=pltpu.CompilerParams(
            dimension_semantics=("arbitrary",),
            vmem_limit_bytes=VMEM_LIMIT_BYTES),
        name="diff_attention_latent",
    )(*lams, q, k, v, gz, subln_g)


def _attn_ctx_kernel(lq1_ref, lk1_ref, lq2_ref, lk2_ref, q_ref, k_ref, v_ref, gz_ref, sg_ref,
                     o_ref, *, tq, lam_init):
    lam = (jnp.exp(jnp.sum(lq1_ref[...] * lk1_ref[...]))
           - jnp.exp(jnp.sum(lq2_ref[...] * lk2_ref[...])) + lam_init)
    lane = lax.broadcasted_iota(jnp.int32, (tq, HEAD_W), 1)
    ones = jnp.ones((tq, HEAD_W), BF16)
    for hd in range(DIFF_HEADS):
        q = q_ref[0, hd]
        zero = jnp.zeros_like(q)
        qz = jnp.concatenate([jnp.where(lane < DIFF_HEAD_DIM, q, zero),
                              jnp.where(lane >= DIFF_HEAD_DIM, q, zero)], axis=0)
        s = lax.dot_general(qz, k_ref[0, hd], NT_DIMS, preferred_element_type=F32)
        p = jnp.exp2(s - jnp.max(s, axis=-1, keepdims=True)).astype(BF16)
        acc = jnp.dot(p, jnp.concatenate([v_ref[0, hd], ones], axis=1),
                      preferred_element_type=F32)
        o = acc[:, 0:HEAD_W] / acc[:, HEAD_W:2 * HEAD_W]
        o = o[0:tq] - lam * o[tq:2 * tq]
        o = o * lax.rsqrt(jnp.mean(o * o, axis=-1, keepdims=True) + EPS) * sg_ref[...]
        cols = slice(hd * HEAD_W, (hd + 1) * HEAD_W)
        o_ref[0, :, cols] = (o * (1.0 - lam_init) * gz_ref[0, :, cols].astype(F32)).astype(BF16)


def _attention_ctx(q, k, v, gz, lams, subln_g, lam_init):
    bsz, _, lt, _ = q.shape
    tq = CTX_LEN
    blk = (lt - CTX_LEN) // tq
    lam_spec = pl.BlockSpec((1, DIFF_HEAD_DIM), lambda b: (0, 0))
    head_blk = pl.BlockSpec((1, DIFF_HEADS, tq, HEAD_W), lambda b: (b, 0, blk, 0))
    tok_blk = pl.BlockSpec((1, tq, W_B), lambda b: (b, blk, 0))
    return pl.pallas_call(
        functools.partial(_attn_ctx_kernel, tq=tq, lam_init=lam_init),
        grid=(bsz,),
        in_specs=[lam_spec, lam_spec, lam_spec, lam_spec, head_blk, head_blk, head_blk, tok_blk,
                  pl.BlockSpec((1, HEAD_W), lambda b: (0, 0))],
        out_specs=tok_blk,
        out_shape=jax.ShapeDtypeStruct((bsz, lt, W_B), BF16),
        input_output_aliases={GZ_ARG: 0},
        compiler_params=pltpu.CompilerParams(
            dimension_semantics=("arbitrary",),
            vmem_limit_bytes=VMEM_LIMIT_BYTES),
        name="diff_attention_ctx",
    )(*lams, q, k, v, gz, subln_g)


N_OUTPROJ_IN = 12


def _outproj_kernel(*refs, tm, lat, lt, final):
    o_ref, w_sc = refs[N_OUTPROJ_IN:]
    _cast_weights_once(w_sc, refs[10])
    o_ref[0] = _outproj_body(*refs[:N_OUTPROJ_IN], w_sc, tm=tm, lat=lat, lt=lt, final=final)


def _outproj_body(x_ref, mod_ref, ya_ref, yb_ref, p_ref, pprev_ref, pnext_ref, gc_ref,
                  cw_ref, cb_ref, w_ref, fg_ref, w_sc, *, tm, lat, lt, final):
    del w_ref
    i = pl.program_id(1)
    row = i * tm + lax.broadcasted_iota(jnp.int32, (tm, 1), 0)
    local = lax.broadcasted_iota(jnp.int32, (tm, 1), 0)
    p = p_ref[0].astype(F32)
    prev_row = pprev_ref[0, HALO - 1:HALO, :].astype(F32)
    next_row = pnext_ref[0, 0:1, :].astype(F32)
    up = jnp.where(local == 0, prev_row, pltpu.roll(p, 1, 0))
    dn = jnp.where(local == tm - 1, next_row, pltpu.roll(p, tm - 1, 0))
    up = jnp.where((row == 0) | (row == lat), 0.0, up)
    dn = jnp.where((row == lat - 1) | (row == lt - 1), 0.0, dn)
    cw = cw_ref[...]
    conv = up * cw[0:1] + p * cw[1:2] + dn * cw[2:3] + cb_ref[...]
    yc = (gc_ref[0].astype(F32) * conv).astype(BF16)

    y = jnp.dot(ya_ref[0], w_sc[0:W_A], preferred_element_type=F32)
    y = y + jnp.dot(yb_ref[0], w_sc[W_A:W_A + W_B], preferred_element_type=F32)
    y = y + jnp.dot(yc, w_sc[W_A + W_B:D_MIX], preferred_element_type=F32)

    mod = mod_ref[0]
    gate = jnp.where(row >= lat, mod[5:6], mod[2:3])
    x = x_ref[0] + gate * y
    if final:
        x = x * lax.rsqrt(jnp.mean(x * x, axis=-1, keepdims=True) + EPS) * fg_ref[...]
    return x


def _midproj_kernel(*refs, tm, lat, lt):
    out_in = refs[:N_OUTPROJ_IN]
    in_in = refs[N_OUTPROJ_IN:N_OUTPROJ_IN + N_INPROJ_IN]
    x_ref = refs[N_OUTPROJ_IN + N_INPROJ_IN]
    in_out = refs[N_OUTPROJ_IN + N_INPROJ_IN + 1:N_OUTPROJ_IN + N_INPROJ_IN + 1 + N_INPROJ_OUT]
    wo_sc, wi_sc = refs[-2:]
    _cast_weights_once(wo_sc, out_in[10])
    _cast_weights_once(wi_sc, in_in[2])
    x = _outproj_body(*out_in, wo_sc, tm=tm, lat=lat, lt=lt, final=False)
    x_ref[0] = x
    _inproj_body(x, *in_in, *in_out, wi_sc, tm=tm, lat=lat)


def _outproj_specs(lt, tm, layer):
    hpt = tm // HALO
    n_halo = lt // HALO
    prev = pl.BlockSpec((1, HALO, W_C),
                        lambda b, i: (b, jnp.maximum(i * hpt - 1, 0), 0))
    nxt = pl.BlockSpec((1, HALO, W_C),
                       lambda b, i: (b, jnp.minimum((i + 1) * hpt, n_halo - 1), 0))
    in_specs = [
        _tok_spec(tm, D_MODEL),
        pl.BlockSpec((1, MOD_ROWS, D_MODEL), lambda b, i: (b, 0, 0)),
        _tok_spec(tm, W_A), _tok_spec(tm, W_B), _tok_spec(tm, W_C), prev, nxt, _tok_spec(tm, W_C),
        _const_spec((3, W_C)), _const_spec((1, W_C)),
        pl.BlockSpec((1, D_MIX, D_MODEL), lambda b, i: (layer, 0, 0)), _const_spec((1, D_MODEL)),
    ]
    assert len(in_specs) == N_OUTPROJ_IN
    return in_specs


def _outproj_final(xs, mod_l, ya, yb, p, gc, conv_w, conv_b, w_out, layer, final_g):
    bsz, lt, _ = xs.shape
    lat = lt - CTX_LEN
    tm = FINAL_OUT_TILE
    return pl.pallas_call(
        functools.partial(_outproj_kernel, tm=tm, lat=lat, lt=lt, final=True),
        grid=(bsz, lat // tm),
        in_specs=_outproj_specs(lt, tm, layer),
        out_specs=_tok_spec(tm, D_MODEL),
        out_shape=jax.ShapeDtypeStruct((bsz, lat, D_MODEL), F32),
        scratch_shapes=[pltpu.VMEM((D_MIX, D_MODEL), BF16)],
        compiler_params=pltpu.CompilerParams(
            dimension_semantics=("arbitrary", "arbitrary"),
            vmem_limit_bytes=VMEM_LIMIT_BYTES),
        name="out_projection",
    )(xs, mod_l, ya, yb, p, p, p, gc, conv_w, conv_b, w_out, final_g)


def _midproj(xs, out_args, in_args, layer):
    bsz, lt, _ = xs.shape
    tm = MID_TILE
    mod_prev, ya, yb, p, gc, conv_w, conv_b, w_out, final_g = out_args
    in_specs, out_specs, out_shape = _inproj_specs(bsz, lt, tm, layer)
    return pl.pallas_call(
        functools.partial(_midproj_kernel, tm=tm, lat=lt - CTX_LEN, lt=lt),
        grid=(bsz, lt // tm),
        in_specs=_outproj_specs(lt, tm, layer - 1) + in_specs,
        out_specs=[_tok_spec(tm, D_MODEL)] + out_specs,
        out_shape=[jax.ShapeDtypeStruct((bsz, lt, D_MODEL), F32)] + out_shape,
        scratch_shapes=[pltpu.VMEM((D_MIX, D_MODEL), BF16), pltpu.VMEM((D_MODEL, D_IN), BF16)],
        compiler_params=pltpu.CompilerParams(
            dimension_semantics=("arbitrary", "arbitrary"),
            vmem_limit_bytes=VMEM_LIMIT_BYTES),
        name="mid_projection",
    )(xs, mod_prev, ya, yb, p, p, p, gc, conv_w, conv_b, w_out, final_g, *in_args)


def _rope_tables(length):
    t = jnp.arange(length)
    pos = jnp.stack([t // GRID_W, t % GRID_W], axis=1).astype(F32)
    inv = ROPE_BASE ** (-jnp.arange(ROPE_PAIRS, dtype=F32) / ROPE_PAIRS)
    ang = pos[:, :, None] * inv

    def to_lanes(tab):
        tab = jnp.broadcast_to(tab[:, None, :, None, :], (length, 2, 2, 2, ROPE_PAIRS))
        return tab.reshape(length, HEAD_W)

    cos = to_lanes(jnp.cos(ang))
    sin = to_lanes(jnp.sin(ang))
    second = (jnp.arange(HEAD_W)[None, :] % (2 * ROPE_PAIRS)) >= ROPE_PAIRS
    sin_lo = jnp.where(second, 0.0, -sin)
    sin_hi = jnp.where(second, sin, 0.0)
    ident = jnp.ones((CTX_LEN, HEAD_W), F32)
    zeros = jnp.zeros((CTX_LEN, HEAD_W), F32)
    return (jnp.concatenate([cos, ident]), jnp.concatenate([sin_lo, zeros]),
            jnp.concatenate([sin_hi, zeros]))


def kernel(x, c, ctx, c_ctx, w_mod, b_mod, norm_g, w_in, w_out, sgu_norm_g, sgu_w, sgu_b,
           lambda_q1, lambda_k1, lambda_q2, lambda_k2, subln_g, conv_w, conv_b, final_g):
    bsz, length, _ = x.shape
    assert ctx.shape[1] == CTX_LEN and bsz + 1 <= MOD_ROWS
    lt = CTX_LEN + length
    assert lt % IN_TILE == 0 and lt % MID_TILE == 0 and IN_TILE % CTX_LEN == 0
    assert length % FINAL_OUT_TILE == 0 and length % Q_TILE == 0 and length % Q_TILE_BOUNDED == 0
    assert length % CTX_LEN == 0 and lt % K_TILE == 0

    cond = jnp.zeros((MOD_ROWS, D_MODEL), F32).at[:bsz].set(c).at[bsz].set(c_ctx)
    mod = _modulation(cond, w_mod, b_mod)
    mod_b = jnp.transpose(mod[:, :, :bsz], (0, 2, 1, 3))
    mod_c = jnp.broadcast_to(mod[:, None, :, bsz], (DEPTH, bsz, 3, D_MODEL))
    pad = jnp.zeros((DEPTH, bsz, MOD_ROWS - 6, D_MODEL), F32)
    mod_rows = jnp.concatenate([mod_b, mod_c, pad], axis=2)

    cos, slo, shi = _rope_tables(length)
    ws = sgu_w.reshape(DEPTH, GMLP_HEADS * GMLP_CHUNK, GMLP_CHUNK).astype(BF16)
    bs = jnp.repeat(jnp.transpose(sgu_b, (0, 2, 1)), GMLP_HEAD_DIM, axis=2)

    half_ind = (jnp.arange(W_B)[:, None] // DIFF_HEAD_DIM == jnp.arange(LANES)[None, :]).astype(BF16)
    n_halves = 2 * DIFF_HEADS

    def norm_bounds(n):
        n = jnp.max(n, axis=(1, 2))[:, :n_halves].reshape(bsz, DIFF_HEADS, 2)
        return jnp.sqrt(n) * NORM_SLACK

    def inproj_args(l):
        return (mod_rows[l], norm_g[l][None], w_in, cos, slo, shi,
                sgu_norm_g[l][None], ws[l], bs[l], half_ind)

    xs, *proj = _inproj_first(x, ctx, inproj_args(0), 0)
    for l in range(DEPTH):
        lam_init = 0.8 - 0.6 * math.exp(-0.3 * l)
        ya, q, k, v, gz, p, gc, qn, kn = proj
        lams = (lambda_q1[l][None], lambda_k1[l][None], lambda_q2[l][None], lambda_k2[l][None])
        yb = _attention_ctx(q, k, v, gz, lams, subln_g[l][None], lam_init)
        q_norm, k_norm = norm_bounds(qn), norm_bounds(kn)
        k_bound = jnp.zeros((bsz, DIFF_HEADS, MOD_ROWS, LANES), F32).at[:, :, 0:2, :].set(
            jnp.broadcast_to(k_norm[..., None], (bsz, DIFF_HEADS, 2, LANES)))
        bounded = jnp.max(q_norm * k_norm) <= MAX_SCORE_BOUND
        yb = lax.cond(
            bounded,
            lambda q, k, v, yb, kb: _attention_bounded(
                q, k, v, yb, lams, subln_g[l][None], kb, lam_init,
                tk=K_TILE, kt=K_TILE),
            lambda q, k, v, yb, kb: _attention_pipelined(
                q, k, v, yb, lams, subln_g[l][None], lam_init,
                tk=K_TILE, kt=V_TILE),
            q, k, v, yb, k_bound)
        out_args = (mod_rows[l], ya, yb, p, gc, conv_w[l], conv_b[l][None], w_out, final_g[None])
        if l + 1 < DEPTH:
            xs, *proj = _midproj(xs, out_args, inproj_args(l + 1), l + 1)
        else:
            xs = _outproj_final(xs, *out_args[:-1], l, out_args[-1])
    return xs
```
